```python
import jax
import jax.numpy as jnp
from jax import lax
import numpy as np

D_MODEL = 2048
BATCH = 2
SEQ = 4096
DEPTH = 2
DEC_BATCH = 128
DEC_SEQ = 1
PAST_LEN = 2048
PAGE_SIZE = 128

N_HEADS = 8
HEAD_DIM = 128
D_ATTN = N_HEADS * HEAD_DIM
N_KV = 2
Q_PER_KV = N_HEADS // N_KV
KV_WIDTH = N_KV * HEAD_DIM
ROT_DIM = HEAD_DIM // 4
ROPE_THETA = 500000.0
CMP_BLOCK = 32
SEL_BLOCK = 64
TOP_N = 16
WINDOW = 512
Q_BLOCK = 128
LRU_WIDTH = D_MODEL - D_ATTN
LRU_BLOCKS = 8
LRU_BLOCK_DIM = LRU_WIDTH // LRU_BLOCKS
CONV_W = 4
LRU_C = 8.0
D_IN = D_ATTN + 6 * KV_WIDTH + 3 * N_HEADS + 2 * LRU_WIDTH
D_FF = 256 * ((8 * D_MODEL // 3 + 255) // 256)
N_EXPERTS = 8
TOP_K = 2
D_FF_EXPERT = 7 * D_MODEL // 2
EPS = 1e-6
SCALE = HEAD_DIM ** -0.5
NEG_INF = -1e30

kernel_name = 'hybrid_nsa_rglru_decode_step'


def rmsnorm(x, g):
    xf = x.astype(jnp.float32)
    y = xf * lax.rsqrt(jnp.mean(xf * xf, axis=-1, keepdims=True) + EPS)
    return (y * g.astype(jnp.float32)).astype(x.dtype)


def rope(x, pos):
    half = ROT_DIM // 2
    inv_freq = ROPE_THETA ** (-2.0 * jnp.arange(half, dtype=jnp.float32) / ROT_DIM)
    ang = pos.astype(jnp.float32)[:, None] * inv_freq
    cos, sin = jnp.cos(ang)[:, None, :], jnp.sin(ang)[:, None, :]
    xf = x.astype(jnp.float32)
    x1, x2, rest = xf[..., :half], xf[..., half:ROT_DIM], xf[..., ROT_DIM:]
    return jnp.concatenate([x1 * cos - x2 * sin, x2 * cos + x1 * sin, rest], axis=-1).astype(x.dtype)


def masked_softmax(s, mask):
    p = jax.nn.softmax(jnp.where(mask, s, NEG_INF), axis=-1)
    return jnp.where(mask, p, 0.0)


def pad_rows(x, multiple):
    extra = (-x.shape[1]) % multiple
    return jnp.pad(x, [(0, 0), (0, extra)] + [(0, 0)] * (x.ndim - 2))


def project(hn, w_in, q_gain, k_gain, pos):
    B, T, _ = hn.shape
    cuts = np.cumsum([D_ATTN] + [KV_WIDTH] * 6 + [3 * N_HEADS, LRU_WIDTH]).tolist()
    q, kc, vc, ks, vs, kw, vw, g, ux, uy = jnp.split(hn @ w_in, cuts, axis=-1)
    heads = lambda t, n: t.reshape(B, T, n, HEAD_DIM)
    q = rope(rmsnorm(heads(q, N_HEADS), q_gain), pos)
    ks = rope(rmsnorm(heads(ks, N_KV), k_gain[1]), pos)
    kw = rope(rmsnorm(heads(kw, N_KV), k_gain[2]), pos)
    cmp_rows = jnp.stack([heads(kc, N_KV), heads(vc, N_KV)], axis=2)
    sel_rows = jnp.stack([ks, heads(vs, N_KV)], axis=2)
    win_rows = jnp.stack([kw, heads(vw, N_KV)], axis=2)
    gates = jax.nn.sigmoid(g.astype(jnp.float32)).reshape(B, T, N_HEADS, 3).astype(hn.dtype)
    return q, cmp_rows, sel_rows, win_rows, gates, ux, uy


def nsa_keys(cmp_rows, sel_rows, k_gain_c, pe, w1, w2):
    B, S = cmp_rows.shape[:2]
    nc, ns = S // CMP_BLOCK, S // SEL_BLOCK
    blocks = cmp_rows.reshape(B, nc, CMP_BLOCK, 2, N_KV, HEAD_DIM)
    blocks = blocks + jnp.transpose(pe, (1, 0, 2))[:, :, None, :]
    flat = blocks.transpose(3, 0, 1, 4, 2, 5).reshape(2, B, nc, N_KV, CMP_BLOCK * HEAD_DIM)
    hid = jax.nn.gelu(jnp.einsum('nbcgf,nfe->nbcge', flat, w1))
    comp = jnp.einsum('nbcge,ned->nbcgd', hid, w2)
    end_pos = (jnp.arange(nc) + 1) * CMP_BLOCK - 1
    k_cmp = rope(rmsnorm(comp[0], k_gain_c), end_pos)
    v_cmp = comp[1]
    sel = sel_rows.reshape(B, ns, SEL_BLOCK, 2, N_KV, HEAD_DIM).transpose(3, 0, 4, 1, 2, 5)
    return k_cmp, v_cmp, sel[0], sel[1]


def cmp_sel_attend(q, q_pos, k_cmp, v_cmp, k_sel, v_sel):
    B, Tq = q.shape[:2]
    nc, ns = k_cmp.shape[1], k_sel.shape[2]
    qg = q.reshape(B, Tq, N_KV, Q_PER_KV, HEAD_DIM)
    c_end = (jnp.arange(nc) + 1) * CMP_BLOCK - 1
    c_mask = (c_end[None, :] <= q_pos[:, None])[None, :, None, None, :]
    s_c = jnp.einsum('btghd,bcgd->btghc', qg, k_cmp).astype(jnp.float32) * SCALE
    p_c = masked_softmax(s_c, c_mask)
    o_c = jnp.einsum('btghc,bcgd->btghd', p_c.astype(q.dtype), v_cmp)
    imp = p_c.sum(axis=3).reshape(B, Tq, N_KV, ns, SEL_BLOCK // CMP_BLOCK).sum(-1)
    blk = jnp.arange(ns)[None, :]
    allowed = blk * SEL_BLOCK <= q_pos[:, None]
    forced = (blk == 0) | (blk == q_pos[:, None] // SEL_BLOCK)
    score = jnp.where(forced[None, :, None, :], jnp.inf, jnp.where(allowed[None, :, None, :], imp, -jnp.inf))
    _, idx = lax.top_k(score, min(TOP_N, ns))
    n_sel = idx.shape[-1]
    bi = jnp.arange(B)[:, None, None, None]
    gi = jnp.arange(N_KV)[None, None, :, None]
    kb = k_sel[bi, gi, idx]
    vb = v_sel[bi, gi, idx]
    kpos = idx[..., None] * SEL_BLOCK + jnp.arange(SEL_BLOCK)
    s_mask = (kpos <= q_pos[None, :, None, None, None])[:, :, :, None]
    s_s = jnp.einsum('btghd,btgksd->btghks', qg, kb).astype(jnp.float32) * SCALE
    flat = lambda t: t.reshape(t.shape[:4] + (n_sel * SEL_BLOCK,))
    p_s = masked_softmax(flat(s_s), flat(s_mask)).reshape(s_s.shape).astype(q.dtype)
    o_s = jnp.einsum('btghks,btgksd->btghd', p_s, vb)
    return o_c.reshape(B, Tq, N_HEADS, HEAD_DIM), o_s.reshape(B, Tq, N_HEADS, HEAD_DIM)


def window_attend(q, q_pos, rows, k_pos):
    B, Tq = q.shape[:2]
    qg = q.reshape(B, Tq, N_KV, Q_PER_KV, HEAD_DIM)
    d = q_pos[:, None] - k_pos[None, :]
    mask = ((d >= 0) & (d <= WINDOW) & (k_pos[None, :] >= 0))[None, :, None, None, :]
    s = jnp.einsum('btghd,bsgd->btghs', qg, rows[:, :, 0]).astype(jnp.float32) * SCALE
    p = masked_softmax(s, mask).astype(q.dtype)
    return jnp.einsum('btghs,bsgd->btghd', p, rows[:, :, 1]).reshape(B, Tq, N_HEADS, HEAD_DIM)


def rglru_branch(ux, conv_buf, h0, conv_w, conv_b, wa, ba, wx, bx, lam):
    B, T, W = ux.shape
    xp = jnp.concatenate([conv_buf.astype(ux.dtype), ux], axis=1)
    xc = conv_b
    for j in range(CONV_W):
        xc = xc + xp[:, j:j + T] * conv_w[j]
    xb = xc.reshape(B, T, LRU_BLOCKS, LRU_BLOCK_DIM)
    r = jax.nn.sigmoid((jnp.einsum('btni,nio->btno', xb, wa).reshape(B, T, W) + ba).astype(jnp.float32))
    i = jax.nn.sigmoid((jnp.einsum('btni,nio->btno', xb, wx).reshape(B, T, W) + bx).astype(jnp.float32))
    log_a = -LRU_C * r * jax.nn.softplus(-lam.astype(jnp.float32))
    a = jnp.exp(log_a)
    b = jnp.sqrt(-jnp.expm1(2.0 * log_a)) * (i * xc.astype(jnp.float32))

    def step(h, ab):
        h = ab[0] * h + ab[1]
        return h, h

    h_last, hs = lax.scan(step, h0.astype(jnp.float32), (a.swapaxes(0, 1), b.swapaxes(0, 1)))
    return hs.swapaxes(0, 1).astype(ux.dtype), h_last.astype(ux.dtype), xp[:, xp.shape[1] - (CONV_W - 1):]


def mix_out(o_c, o_s, o_w, gates, y_lru, uy, g_attn, g_lru, w_out):
    B, T = o_c.shape[:2]
    o = gates[..., 0:1] * o_c + gates[..., 1:2] * o_s + gates[..., 2:3] * o_w
    o = rmsnorm(o.reshape(B, T, D_ATTN), g_attn)
    r = rmsnorm(y_lru * jax.nn.gelu(uy), g_lru)
    return jnp.concatenate([o, r], axis=-1) @ w_out


def swiglu(h, wg, wu, wd):
    return (jax.nn.silu(h @ wg) * (h @ wu)) @ wd


def moe_ffn(h, router, wg, wu, wd):
    logits = (h @ router).astype(jnp.float32)
    top_v, top_i = lax.top_k(logits, TOP_K)
    w = jax.nn.softmax(top_v, axis=-1)
    gate = jnp.sum(jax.nn.one_hot(top_i, N_EXPERTS, dtype=jnp.float32) * w[..., None], axis=-2).astype(h.dtype)
    y = jnp.zeros_like(h)
    for e in range(N_EXPERTS):
        y = y + gate[..., e:e + 1] * swiglu(h, wg[e], wu[e], wd[e])
    return y


def setup_inputs(seed: int = 0) -> dict:
    key = jax.random.key(seed)
    keys = iter(jax.random.split(key, 48))

    def normal(shape, scale=1.0):
        return jax.random.normal(next(keys), shape, jnp.float32) * scale

    def gain(shape):
        return 1.0 + normal(shape, 0.05)

    n_pages = PAST_LEN // PAGE_SIZE
    n_used = DEC_BATCH * n_pages
    n_phys = n_used + n_used // 4
    win_buf = min(WINDOW, PAST_LEN)
    n_dense, n_moe = (DEPTH + 1) // 2, DEPTH // 2
    kv_row = (2, N_KV, HEAD_DIM)
    page_table = jax.random.permutation(next(keys), n_phys)[:n_used].reshape(DEC_BATCH, n_pages).astype(jnp.int32)
    u = jax.random.uniform(next(keys), (DEPTH, LRU_WIDTH), jnp.float32, 0.9, 0.999)
    s = u ** (1.0 / LRU_C)
    lru_lambda = jnp.log(s) - jnp.log1p(-s)
    return {
        'x_prompt': normal((BATCH, SEQ, D_MODEL)),
        'x_sample': normal((DEC_BATCH, DEC_SEQ, D_MODEL)),
        'cache_cmp': normal((DEPTH, n_phys, PAGE_SIZE) + kv_row),
        'cache_sel': normal((DEPTH, n_phys, PAGE_SIZE) + kv_row),
        'state_win': normal((DEPTH, DEC_BATCH, win_buf) + kv_row),
        'state_lru': normal((DEPTH, DEC_BATCH, LRU_WIDTH), 0.5),
        'state_conv': normal((DEPTH, DEC_BATCH, CONV_W - 1, LRU_WIDTH)),
        'page_table': page_table,
        'ln_mix': gain((DEPTH, D_MODEL)),
        'w_in': normal((DEPTH, D_MODEL, D_IN), D_MODEL ** -0.5),
        'q_norm': gain((DEPTH, HEAD_DIM)),
        'k_norm': gain((DEPTH, 3, HEAD_DIM)),
        'cmp_pe': normal((DEPTH, 2, CMP_BLOCK, HEAD_DIM), 0.1),
        'cmp_w1': normal((DEPTH, 2, CMP_BLOCK * HEAD_DIM, HEAD_DIM), (CMP_BLOCK * HEAD_DIM) ** -0.5),
        'cmp_w2': normal((DEPTH, 2, HEAD_DIM, HEAD_DIM), HEAD_DIM ** -0.5),
        'conv_w': normal((DEPTH, CONV_W, LRU_WIDTH), CONV_W ** -0.5),
        'conv_b': normal((DEPTH, LRU_WIDTH), 0.02),
        'lru_wa': normal((DEPTH, LRU_BLOCKS, LRU_BLOCK_DIM, LRU_BLOCK_DIM), LRU_BLOCK_DIM ** -0.5),
        'lru_ba': normal((DEPTH, LRU_WIDTH), 0.02),
        'lru_wx': normal((DEPTH, LRU_BLOCKS, LRU_BLOCK_DIM, LRU_BLOCK_DIM), LRU_BLOCK_DIM ** -0.5),
        'lru_bx': normal((DEPTH, LRU_WIDTH), 0.02),
        'lru_lambda': lru_lambda,
        'out_norm_attn': gain((DEPTH, D_ATTN)),
        'out_norm_lru': gain((DEPTH, LRU_WIDTH)),
        'w_out': normal((DEPTH, D_MODEL, D_MODEL), D_MODEL ** -0.5),
        'ln_ffn': gain((DEPTH, D_MODEL)),
        'ffn_w_gate': normal((n_dense, D_MODEL, D_FF), D_MODEL ** -0.5),
        'ffn_w_up': normal((n_dense, D_MODEL, D_FF), D_MODEL ** -0.5),
        'ffn_w_down': normal((n_dense, D_FF, D_MODEL), D_FF ** -0.5),
        'moe_router': normal((n_moe, D_MODEL, N_EXPERTS), D_MODEL ** -0.5),
        'moe_w_gate': normal((n_moe, N_EXPERTS, D_MODEL, D_FF_EXPERT), D_MODEL ** -0.5),
        'moe_w_up': normal((n_moe, N_EXPERTS, D_MODEL, D_FF_EXPERT), D_MODEL ** -0.5),
        'moe_w_down': normal((n_moe, N_EXPERTS, D_FF_EXPERT, D_MODEL), D_FF_EXPERT ** -0.5),
    }


def reference(x_prompt, x_sample, cache_cmp, cache_sel, state_win, state_lru, state_conv, page_table,
              ln_mix, w_in, q_norm, k_norm, cmp_pe, cmp_w1, cmp_w2,
              conv_w, conv_b, lru_wa, lru_ba, lru_wx, lru_bx, lru_lambda,
              out_norm_attn, out_norm_lru, w_out, ln_ffn,
              ffn_w_gate, ffn_w_up, ffn_w_down,
              moe_router, moe_w_gate, moe_w_up, moe_w_down):
    B, T = x_prompt.shape[0], x_prompt.shape[1]
    DB, DS = x_sample.shape[0], x_sample.shape[1]
    past_len = page_table.shape[1] * cache_cmp.shape[2]
    win_buf = state_win.shape[2]
    pos_p = jnp.arange(T)
    pos_s = past_len + jnp.arange(DS)
    win_pos_s = jnp.concatenate([past_len - win_buf + jnp.arange(win_buf), pos_s])
    n_qb = T // Q_BLOCK
    unblock = lambda t: jnp.moveaxis(t, 0, 1).reshape(B, T, N_HEADS, HEAD_DIM)

    xp, xs = x_prompt, x_sample
    cmp_p, cmp_s, sel_p, sel_s, win_p, win_s = [], [], [], [], [], []
    lru_p, lru_s, conv_p, conv_s = [], [], [], []
    for l in range(DEPTH):
        cmp_args = (k_norm[l, 0], cmp_pe[l], cmp_w1[l], cmp_w2[l])
        lru_args = (conv_w[l], conv_b[l], lru_wa[l], lru_ba[l], lru_wx[l], lru_bx[l], lru_lambda[l])
        out_args = (out_norm_attn[l], out_norm_lru[l], w_out[l])

        q, c_rows, s_rows, w_rows, gates, ux, uy = project(rmsnorm(xp, ln_mix[l]), w_in[l], q_norm[l], k_norm[l], pos_p)
        k_c, v_c, k_s, v_s = nsa_keys(pad_rows(c_rows, SEL_BLOCK), pad_rows(s_rows, SEL_BLOCK), *cmp_args)
        w_pad = jnp.pad(w_rows, ((0, 0), (WINDOW, 0), (0, 0), (0, 0), (0, 0)))

        def prompt_block(i):
            q0 = i * Q_BLOCK
            qb = lax.dynamic_slice_in_dim(q, q0, Q_BLOCK, axis=1)
            qpos = q0 + jnp.arange(Q_BLOCK)
            o_c, o_s = cmp_sel_attend(qb, qpos, k_c, v_c, k_s, v_s)
            wb = lax.dynamic_slice_in_dim(w_pad, q0, WINDOW + Q_BLOCK, axis=1)
            o_w = window_attend(qb, qpos, wb, q0 - WINDOW + jnp.arange(WINDOW + Q_BLOCK))
            return o_c, o_s, o_w

        oc, os_, ow = lax.map(prompt_block, jnp.arange(n_qb))
        y_lru, h_new, cv_new = rglru_branch(ux, jnp.zeros((B, CONV_W - 1, LRU_WIDTH), ux.dtype),
                                            jnp.zeros((B, LRU_WIDTH), jnp.float32), *lru_args)
        xp = xp + mix_out(unblock(oc), unblock(os_), unblock(ow), gates, y_lru, uy, *out_args)
        cmp_p.append(c_rows)
        sel_p.append(s_rows)
        win_p.append(w_pad[:, w_pad.shape[1] - win_buf:])
        lru_p.append(h_new)
        conv_p.append(cv_new)

        q, c_rows, s_rows, w_rows, gates, ux, uy = project(rmsnorm(xs, ln_mix[l]), w_in[l], q_norm[l], k_norm[l], pos_s)
        gather = lambda pool: pool[l, page_table].reshape((DB, past_len) + pool.shape[3:])
        c_all = pad_rows(jnp.concatenate([gather(cache_cmp), c_rows], axis=1), SEL_BLOCK)
        s_all = pad_rows(jnp.concatenate([gather(cache_sel), s_rows], axis=1), SEL_BLOCK)
        k_c, v_c, k_s, v_s = nsa_keys(c_all, s_all, *cmp_args)
        o_c, o_s = cmp_sel_attend(q, pos_s, k_c, v_c, k_s, v_s)
        w_all = jnp.concatenate([state_win[l].astype(w_rows.dtype), w_rows], axis=1)
        o_w = window_attend(q, pos_s, w_all, win_pos_s)
        y_lru, h_new, cv_new = rglru_branch(ux, state_conv[l], state_lru[l], *lru_args)
        xs = xs + mix_out(o_c, o_s, o_w, gates, y_lru, uy, *out_args)
        cmp_s.append(c_rows)
        sel_s.append(s_rows)
        win_s.append(w_all[:, w_all.shape[1] - win_buf:])
        lru_s.append(h_new)
        conv_s.append(cv_new)

        if l % 2 == 0:
            ffn = lambda h: swiglu(h, ffn_w_gate[l // 2], ffn_w_up[l // 2], ffn_w_down[l // 2])
        else:
            ffn = lambda h: moe_ffn(h, moe_router[l // 2], moe_w_gate[l // 2], moe_w_up[l // 2], moe_w_down[l // 2])
        xp = xp + ffn(rmsnorm(xp, ln_ffn[l]))
        xs = xs + ffn(rmsnorm(xs, ln_ffn[l]))

    return (xp, xs,
            jnp.stack(cmp_p), jnp.stack(cmp_s),
            jnp.stack(sel_p), jnp.stack(sel_s),
            jnp.stack(win_p), jnp.stack(win_s),
            jnp.stack(lru_p), jnp.stack(lru_s),
            jnp.stack(conv_p), jnp.stack(conv_s))
```

```python
import functools

import jax
import jax.numpy as jnp
from jax import lax
from jax.experimental import pallas as pl
from jax.experimental.pallas import tpu as pltpu

N_HEADS = 8
HEAD_DIM = 128
N_KV = 2
Q_PER_KV = N_HEADS // N_KV
D_ATTN = N_HEADS * HEAD_DIM
KV_WIDTH = N_KV * HEAD_DIM
ROW_WIDTH = 2 * KV_WIDTH
ROT_DIM = HEAD_DIM // 4
ROPE_THETA = 500000.0
CMP_BLOCK = 32
SEL_BLOCK = 64
TOP_N = 16
WINDOW = 512
Q_BLOCK = 128
LRU_BLOCKS = 8
CONV_W = 4
LRU_C = 8.0
TOP_K = 2
EPS = 1e-6
SCALE = HEAD_DIM ** -0.5
NEG_INF = -1e30

LANES = 128
SUBLANES = 8
VMEM_LIMIT = 56 * 1024 * 1024

F32 = jnp.float32
BF16 = jnp.bfloat16


def _row_tile(n, target, mult=16):
    best = None
    for t in range(mult, min(n, target) + 1, mult):
        if n % t == 0:
            best = t
    return best if best is not None else n


def _params(semantics):
    return pltpu.CompilerParams(dimension_semantics=semantics, vmem_limit_bytes=VMEM_LIMIT)


def _full(shape):
    zeros = (0,) * len(shape)
    return pl.BlockSpec(shape, lambda *_: zeros)


def _rms(x, gain):
    return x * lax.rsqrt(jnp.mean(x * x, axis=-1, keepdims=True) + EPS) * gain


def _rope_apply(y, cos_t, sin_lo, sin_hi):
    half = ROT_DIM // 2
    return y * cos_t + pltpu.roll(y, HEAD_DIM - half, 1) * sin_lo + pltpu.roll(y, half, 1) * sin_hi


def _softplus(x):
    return jnp.maximum(x, 0.0) + jnp.log1p(jnp.exp(-jnp.abs(x)))


def _topk_mask(score, n_pick):
    n_lanes = score.shape[-1]
    lane = lax.broadcasted_iota(jnp.int32, score.shape, 1).astype(F32)
    sel = jnp.zeros(score.shape, F32)
    for _ in range(n_pick):
        m = jnp.max(score, axis=-1, keepdims=True)
        idx = jnp.min(jnp.where(score == m, lane, float(n_lanes)), axis=-1, keepdims=True)
        pick = lane == idx
        sel = jnp.where(pick, 1.0, sel)
        score = jnp.where(pick, -2.0, score)
    return sel


def _masked_softmax_rows(s, mask):
    sm = jnp.where(mask, s, NEG_INF)
    m = jnp.max(sm, axis=-1, keepdims=True)
    e = jnp.where(mask, jnp.exp(sm - m), 0.0)
    l = jnp.sum(e, axis=-1, keepdims=True)
    return e / jnp.where(l > 0.0, l, 1.0)


def _dot_nt(a, b):
    return lax.dot_general(a, b, (((1,), (1,)), ((), ())), preferred_element_type=F32)


def _rmsnorm_kernel(x_ref, g_ref, o_ref):
    o_ref[...] = _rms(x_ref[...], g_ref[...]).astype(o_ref.dtype)


def _rmsnorm(x, gain):
    n, d = x.shape
    tm = _row_tile(n, 832)
    return pl.pallas_call(
        _rmsnorm_kernel,
        grid=(n // tm,),
        in_specs=[pl.BlockSpec((tm, d), lambda i: (i, 0)), _full((1, d))],
        out_specs=pl.BlockSpec((tm, d), lambda i: (i, 0)),
        out_shape=jax.ShapeDtypeStruct((n, d), BF16),
        compiler_params=_params(("parallel",)),
        name="rmsnorm",
    )(x, gain.reshape(1, d))


def _matmul_kernel(*refs, n_lhs, has_res):
    o_ref = refs[-1]
    acc = jnp.dot(refs[0][...], refs[n_lhs][...], preferred_element_type=F32)
    for k in range(1, n_lhs):
        acc = acc + jnp.dot(refs[k][...], refs[n_lhs + k][...], preferred_element_type=F32)
    if has_res:
        acc = acc + refs[2 * n_lhs][...]
    o_ref[...] = acc


def _matmul(xs, ws, res=None, *, tm_target, tn_target):
    n = xs[0].shape[0]
    d_out = ws[0].shape[1]
    tm = _row_tile(n, tm_target)
    tn = _row_tile(d_out, tn_target, LANES)
    in_specs = [pl.BlockSpec((tm, x.shape[1]), lambda i, j: (i, 0)) for x in xs]
    in_specs += [pl.BlockSpec((w.shape[0], tn), lambda i, j: (0, j)) for w in ws]
    args = list(xs) + list(ws)
    if res is not None:
        in_specs.append(pl.BlockSpec((tm, tn), lambda i, j: (i, j)))
        args.append(res)
    return pl.pallas_call(
        functools.partial(_matmul_kernel, n_lhs=len(xs), has_res=res is not None),
        grid=(n // tm, d_out // tn),
        in_specs=in_specs,
        out_specs=pl.BlockSpec((tm, tn), lambda i, j: (i, j)),
        out_shape=jax.ShapeDtypeStruct((n, d_out), F32),
        compiler_params=_params(("parallel", "parallel")),
        name="matmul",
    )(*args)


PROJ_WIDTH = 5120
COL_UX, COL_UY, COL_CMP, COL_SEL, COL_WIN, COL_GATE = 1024, 2048, 3072, 3584, 4096, 4608
GATE_WIDTH = 2 * LANES


def _reorder_w_in(w):
    lru = D_ATTN
    c_kv = D_ATTN
    c_g = c_kv + 3 * ROW_WIDTH
    c_ux = c_g + 3 * N_HEADS
    c_uy = c_ux + lru
    n_g = 3 * Q_PER_KV
    zpad = lambda k: jnp.zeros((w.shape[0], k), w.dtype)
    parts = [w[:, :D_ATTN], w[:, c_ux:c_uy], w[:, c_uy:c_uy + lru], w[:, c_kv:c_g],
             w[:, c_g:c_g + n_g], zpad(LANES - n_g), w[:, c_g + n_g:c_ux], zpad(LANES - n_g)]
    width = sum(p.shape[1] for p in parts)
    parts.append(zpad(PROJ_WIDTH - width))
    return jnp.concatenate(parts, axis=1).astype(BF16)


def _postproj_kernel(q_ref, sel_ref, win_ref, gate_ref, cos_ref, slo_ref, shi_ref,
                     qg_ref, ksg_ref, kwg_ref,
                     qo_ref, sel_f_ref, sel_b_ref, win_f_ref, win_b_ref, gate_o_ref):
    cos_t, slo, shi = cos_ref[...], slo_ref[...], shi_ref[...]
    for h in range(N_HEADS):
        cols = slice(h * HEAD_DIM, (h + 1) * HEAD_DIM)
        y = _rope_apply(_rms(q_ref[:, cols], qg_ref[...]), cos_t, slo, shi)
        qo_ref[:, cols] = y.astype(qo_ref.dtype)
    for src, gain_ref, out_f, out_b in ((sel_ref, ksg_ref, sel_f_ref, sel_b_ref),
                                        (win_ref, kwg_ref, win_f_ref, win_b_ref)):
        for g in range(N_KV):
            cols = slice(g * HEAD_DIM, (g + 1) * HEAD_DIM)
            y = _rope_apply(_rms(src[:, cols], gain_ref[...]), cos_t, slo, shi)
            out_f[:, cols] = y
            out_b[:, cols] = y.astype(out_b.dtype)
        v = src[:, KV_WIDTH:]
        out_f[:, KV_WIDTH:] = v
        out_b[:, KV_WIDTH:] = v.astype(out_b.dtype)
    gate_o_ref[...] = jax.nn.sigmoid(gate_ref[...])


def _postproj(proj, tables, q_gain, ks_gain, kw_gain):
    n = proj.shape[0]
    tm = _row_tile(n, 416)
    rows = lambda width, col: pl.BlockSpec((tm, width), lambda i: (i, col))
    tab = pl.BlockSpec((tm, HEAD_DIM), lambda i: (i, 0))
    gain = _full((1, HEAD_DIM))
    out = lambda width: pl.BlockSpec((tm, width), lambda i: (i, 0))
    sds = jax.ShapeDtypeStruct
    return pl.pallas_call(
        _postproj_kernel,
        grid=(n // tm,),
        in_specs=[rows(D_ATTN, 0), rows(ROW_WIDTH, COL_SEL // ROW_WIDTH), rows(ROW_WIDTH, COL_WIN // ROW_WIDTH),
                  rows(GATE_WIDTH, COL_GATE // GATE_WIDTH), tab, tab, tab, gain, gain, gain],
        out_specs=[out(D_ATTN), out(ROW_WIDTH), out(ROW_WIDTH), out(ROW_WIDTH), out(ROW_WIDTH), out(GATE_WIDTH)],
        out_shape=[sds((n, D_ATTN), BF16), sds((n, ROW_WIDTH), F32), sds((n, ROW_WIDTH), BF16),
                   sds((n, ROW_WIDTH), F32), sds((n, ROW_WIDTH), BF16), sds((n, GATE_WIDTH), F32)],
        compiler_params=_params(("parallel",)),
        name="postproj",
    )(proj, proj, proj, proj, *tables, q_gain.reshape(1, -1), ks_gain.reshape(1, -1), kw_gain.reshape(1, -1))


def _rope_tables(pos):
    half = ROT_DIM // 2
    inv_freq = ROPE_THETA ** (-2.0 * jnp.arange(half, dtype=F32) / ROT_DIM)
    ang = pos.astype(F32)[:, None] * inv_freq
    cos, sin = jnp.cos(ang), jnp.sin(ang)
    n = pos.shape[0]
    ones = jnp.ones((n, HEAD_DIM - ROT_DIM), F32)
    zeros = jnp.zeros((n, HEAD_DIM - half), F32)
    cos_t = jnp.concatenate([cos, cos, ones], axis=1)
    sin_lo = jnp.concatenate([-sin, zeros], axis=1)
    sin_hi = jnp.concatenate([jnp.zeros((n, half), F32), sin, jnp.zeros((n, HEAD_DIM - ROT_DIM), F32)], axis=1)
    return cos_t, sin_lo, sin_hi


CMP_ROW = CMP_BLOCK * ROW_WIDTH


def _compress_rows(load_cols, n_rows, pe_ref, w1_ref, w2_ref, gain_ref, cos_ref, slo_ref, shi_ref,
                   k_ref, v_ref):
    for n in range(2):
        acc = jnp.zeros((N_KV * n_rows, HEAD_DIM), F32)
        for t in range(CMP_BLOCK):
            base = t * ROW_WIDTH + n * KV_WIDTH
            pe = pe_ref[n, t:t + 1, :]
            lhs = jnp.concatenate([(load_cols(base + g * HEAD_DIM) + pe).astype(BF16) for g in range(N_KV)], axis=0)
            acc = acc + jnp.dot(lhs, w1_ref[n, t], preferred_element_type=F32)
        hid = jax.nn.gelu(acc).astype(BF16)
        comp = jnp.dot(hid, w2_ref[n], preferred_element_type=F32)
        for g in range(N_KV):
            part = comp[g * n_rows:(g + 1) * n_rows]
            cols = slice(g * HEAD_DIM, (g + 1) * HEAD_DIM)
            if n == 0:
                part = _rope_apply(_rms(part, gain_ref[...]), cos_ref[...], slo_ref[...], shi_ref[...])
                k_ref[:, cols] = part.astype(k_ref.dtype)
            else:
                v_ref[:, cols] = part.astype(v_ref.dtype)


def _compress_prompt_kernel(x_ref, *rest):
    n_rows = x_ref.shape[0]
    _compress_rows(lambda c0: x_ref[:, c0:c0 + HEAD_DIM], n_rows, *rest)


def _compress_sample_kernel(pt_ref, *refs, n_page_refs, rows_per_page):
    del pt_ref
    pages = refs[:n_page_refs]
    rest = refs[n_page_refs:-1]
    x_s = refs[-1]
    for k, page in enumerate(pages):
        x_s[k * rows_per_page:(k + 1) * rows_per_page, :] = page[0]
    n_rows = x_s.shape[0]
    _compress_rows(lambda c0: x_s[:, c0:c0 + HEAD_DIM], n_rows, *rest)


def _compress_weight_specs():
    return [_full((2, CMP_BLOCK, HEAD_DIM)), _full((2, CMP_BLOCK, HEAD_DIM, HEAD_DIM)),
            _full((2, HEAD_DIM, HEAD_DIM)), _full((1, HEAD_DIM))]


def _compress_prompt(c_rows, n_batch, nc, weights, end_tables):
    tr = _row_tile(nc, 128)
    per_b = nc // tr
    tab = pl.BlockSpec((tr, HEAD_DIM), lambda i: (i % per_b, 0))
    out = pl.BlockSpec((tr, KV_WIDTH), lambda i: (i, 0))
    sds = jax.ShapeDtypeStruct((n_batch * nc, KV_WIDTH), BF16)
    return pl.pallas_call(
        _compress_prompt_kernel,
        grid=(n_batch * per_b,),
        in_specs=[pl.BlockSpec((tr, CMP_ROW), lambda i: (i, 0))] + _compress_weight_specs() + [tab, tab, tab],
        out_specs=[out, out],
        out_shape=[sds, sds],
        compiler_params=_params(("parallel",)),
        name="compress_prompt",
    )(c_rows, *weights, *end_tables)


def _compress_sample(cache, layer, page_table, weights, end_tables, nb):
    depth, n_phys, page = cache.shape[:3]
    n_batch, n_pages = page_table.shape
    rows_per_page = page // CMP_BLOCK
    nc = n_pages * rows_per_page
    pages = cache.reshape(depth * n_phys, rows_per_page, CMP_ROW)
    base = layer * n_phys
    page_specs = [pl.BlockSpec((1, rows_per_page, CMP_ROW),
                               lambda i, pt, bb=bb, p=p: (base + pt[i * nb + bb, p], 0, 0))
                  for bb in range(nb) for p in range(n_pages)]
    n_rows = nb * nc
    tab = pl.BlockSpec((n_rows, HEAD_DIM), lambda i, pt: (0, 0))
    wspecs = [pl.BlockSpec(s.block_shape, lambda i, pt, z=(0,) * len(s.block_shape): z) for s in _compress_weight_specs()]
    out = pl.BlockSpec((n_rows, KV_WIDTH), lambda i, pt: (i, 0))
    sds = jax.ShapeDtypeStruct((n_batch * nc, KV_WIDTH), BF16)
    tables = [jnp.tile(t, (nb, 1)) for t in end_tables]
    return pl.pallas_call(
        functools.partial(_compress_sample_kernel, n_page_refs=nb * n_pages, rows_per_page=rows_per_page),
        grid_spec=pltpu.PrefetchScalarGridSpec(
            num_scalar_prefetch=1,
            grid=(n_batch // nb,),
            in_specs=page_specs + wspecs + [tab, tab, tab],
            out_specs=[out, out],
            scratch_shapes=[pltpu.VMEM((n_rows, CMP_ROW), F32)],
        ),
        out_shape=[sds, sds],
        compiler_params=_params(("arbitrary",)),
        name="compress_sample",
    )(page_table, *([pages] * (nb * n_pages)), *weights, *tables)


def _deinterleave_blocks(x, n_batch, nc):
    x = x.reshape(n_batch, nc, -1)
    return jnp.concatenate([x[:, 0::2], x[:, 1::2]], axis=1)


SEL_CHUNK = 256
WIN_SPAN = WINDOW + Q_BLOCK


def _attn_prompt_kernel(q_ref, kc_ref, vc_ref, sel_ref, win_ref, gate_ref, gain_ref, o_ref, *, seq_len):
    nc = seq_len // CMP_BLOCK
    ns = seq_len // SEL_BLOCK
    i = pl.program_id(1)
    q0 = i * Q_BLOCK
    tpos = q0 + lax.broadcasted_iota(jnp.int32, (Q_BLOCK, 1), 0)
    gates = gate_ref[...]
    slabs = []
    for g in range(N_KV):
        kcol = slice(g * HEAD_DIM, (g + 1) * HEAD_DIM)
        vcol = slice(KV_WIDTH + g * HEAD_DIM, KV_WIDTH + (g + 1) * HEAD_DIM)
        qs = jnp.concatenate([q_ref[:, (g * Q_PER_KV + h) * HEAD_DIM:(g * Q_PER_KV + h + 1) * HEAD_DIM]
                              for h in range(Q_PER_KV)], axis=0)

        col = lax.broadcasted_iota(jnp.int32, (Q_BLOCK, nc), 1)
        blk = jnp.where(col < ns, 2 * col, 2 * (col - ns) + 1)
        c_mask = ((blk + 1) * CMP_BLOCK - 1) <= tpos
        s_c = (_dot_nt(qs, kc_ref[0, :, kcol]) * SCALE).reshape(Q_PER_KV, Q_BLOCK, nc)
        p_c = _masked_softmax_rows(s_c, c_mask[None])
        o_c = jnp.dot(p_c.reshape(Q_PER_KV * Q_BLOCK, nc).astype(BF16), vc_ref[0, :, kcol],
                      preferred_element_type=F32)
        p_tok = p_c[0] + p_c[1] + p_c[2] + p_c[3]
        imp = p_tok[:, :ns] + p_tok[:, ns:]

        sblk = lax.broadcasted_iota(jnp.int32, (Q_BLOCK, ns), 1)
        allowed = sblk * SEL_BLOCK <= tpos
        forced = (sblk == 0) | (sblk == tpos // SEL_BLOCK)
        score = jnp.where(forced, 1e30, jnp.where(allowed, imp, -1.0))
        sel = _topk_mask(score, min(TOP_N, ns)).astype(BF16)

        def sel_step(j, carry):
            m, l, acc = carry
            k0 = pl.multiple_of(j * SEL_CHUNK, SEL_CHUNK)
            kpos = k0 + lax.broadcasted_iota(jnp.int32, (1, SEL_CHUNK), 1)
            expand = (kpos // SEL_BLOCK == lax.broadcasted_iota(jnp.int32, (ns, SEL_CHUNK), 0)).astype(BF16)
            chosen = jnp.dot(sel, expand, preferred_element_type=F32) > 0.5
            mask = (chosen & (kpos <= tpos))[None]
            s = (_dot_nt(qs, sel_ref[pl.ds(k0, SEL_CHUNK), kcol]) * SCALE).reshape(Q_PER_KV, Q_BLOCK, SEL_CHUNK)
            sm = jnp.where(mask, s, NEG_INF)
            m_new = jnp.maximum(m, jnp.max(sm, axis=-1, keepdims=True))
            p = jnp.where(mask, jnp.exp(sm - m_new), 0.0)
            alpha = jnp.exp(m - m_new)
            l = alpha * l + jnp.sum(p, axis=-1, keepdims=True)
            pv = jnp.dot(p.reshape(Q_PER_KV * Q_BLOCK, SEL_CHUNK).astype(BF16),
                         sel_ref[pl.ds(k0, SEL_CHUNK), vcol], preferred_element_type=F32)
            acc = alpha * acc + pv.reshape(Q_PER_KV, Q_BLOCK, HEAD_DIM)
            return m_new, l, acc

        n_steps = (q0 + Q_BLOCK + SEL_CHUNK - 1) // SEL_CHUNK
        init = (jnp.full((Q_PER_KV, Q_BLOCK, 1), NEG_INF, F32), jnp.zeros((Q_PER_KV, Q_BLOCK, 1), F32),
                jnp.zeros((Q_PER_KV, Q_BLOCK, HEAD_DIM), F32))
        _, l_s, acc_s = lax.fori_loop(0, n_steps, sel_step, init)
        o_s = (acc_s / l_s).reshape(Q_PER_KV * Q_BLOCK, HEAD_DIM)

        w0 = pl.multiple_of(jnp.maximum(q0 - WINDOW, 0), Q_BLOCK)
        kpos = w0 + lax.broadcasted_iota(jnp.int32, (1, WIN_SPAN), 1)
        dist = tpos - kpos
        w_mask = ((dist >= 0) & (dist <= WINDOW))[None]
        s_w = (_dot_nt(qs, win_ref[pl.ds(w0, WIN_SPAN), kcol]) * SCALE).reshape(Q_PER_KV, Q_BLOCK, WIN_SPAN)
        p_w = _masked_softmax_rows(s_w, w_mask)
        o_w = jnp.dot(p_w.reshape(Q_PER_KV * Q_BLOCK, WIN_SPAN).astype(BF16), win_ref[pl.ds(w0, WIN_SPAN), vcol],
                      preferred_element_type=F32)

        for h in range(Q_PER_KV):
            rows = slice(h * Q_BLOCK, (h + 1) * Q_BLOCK)
            c = g * LANES + 3 * h
            slabs.append(gates[:, c:c + 1] * o_c[rows] + gates[:, c + 1:c + 2] * o_s[rows]
                         + gates[:, c + 2:c + 3] * o_w[rows])
    o = jnp.concatenate(slabs, axis=1)
    o_ref[...] = _rms(o, gain_ref[...]).astype(o_ref.dtype)


def _attn_prompt(q_b, k_cmp, v_cmp, sel_b, win_b, gates, gain, n_batch, seq_len):
    n_qb = seq_len // Q_BLOCK
    nc = seq_len // CMP_BLOCK
    qrow = lambda width: pl.BlockSpec((Q_BLOCK, width), lambda b, i: (b * n_qb + i, 0))
    cmp_spec = pl.BlockSpec((1, nc, KV_WIDTH), lambda b, i: (b, 0, 0))
    seq_spec = pl.BlockSpec((seq_len, ROW_WIDTH), lambda b, i: (b, 0))
    return pl.pallas_call(
        functools.partial(_attn_prompt_kernel, seq_len=seq_len),
        grid=(n_batch, n_qb),
        in_specs=[qrow(D_ATTN), cmp_spec, cmp_spec, seq_spec, seq_spec, qrow(GATE_WIDTH), _full((1, D_ATTN))],
        out_specs=qrow(D_ATTN),
        out_shape=jax.ShapeDtypeStruct((n_batch * seq_len, D_ATTN), BF16),
        compiler_params=_params(("parallel", "arbitrary")),
        name="attn_prompt",
    )(q_b, k_cmp, v_cmp, sel_b, win_b, gates, gain.reshape(1, -1))


def _own_group(x, row):
    return jnp.where(row < Q_PER_KV, x[:, :HEAD_DIM], x[:, HEAD_DIM:])


def _attn_sample_kernel(pt_ref, *refs, n_pages, page, win_buf, past_len):
    del pt_ref
    q_ref, kc_ref, vc_ref, snew_ref, win_ref, wnew_ref, gate_ref, gain_ref = refs[:8]
    pages = refs[8:8 + n_pages]
    o_ref = refs[8 + n_pages]
    ns_past = past_len // SEL_BLOCK
    nc_past = past_len // CMP_BLOCK

    q = q_ref[0]
    row = lax.broadcasted_iota(jnp.int32, (N_HEADS, 1), 0)
    zero = jnp.zeros_like(q)
    q2 = jnp.concatenate([jnp.where(row < Q_PER_KV, q, zero), jnp.where(row < Q_PER_KV, zero, q)], axis=1)
    qf = q.astype(F32)

    s_c = _dot_nt(q2, kc_ref[0]) * SCALE
    m_c = jnp.max(s_c, axis=-1, keepdims=True)
    e_c = jnp.exp(s_c - m_c)
    p_c = e_c / jnp.sum(e_c, axis=-1, keepdims=True)
    o_c = _own_group(jnp.dot(p_c.astype(BF16), vc_ref[0], preferred_element_type=F32), row)
    p_grp = jnp.where(row < Q_PER_KV, jnp.sum(p_c[:Q_PER_KV], axis=0, keepdims=True),
                      jnp.sum(p_c[Q_PER_KV:], axis=0, keepdims=True))
    imp = p_grp[:, :ns_past] + p_grp[:, ns_past:]

    sblk = lax.broadcasted_iota(jnp.int32, (N_HEADS, ns_past), 1)
    score = jnp.where(sblk == 0, 1e30, imp)
    sel = _topk_mask(score, min(TOP_N, ns_past + 1) - 1).astype(BF16)

    kpos = lax.broadcasted_iota(jnp.int32, (1, past_len), 1)
    expand = (kpos // SEL_BLOCK == lax.broadcasted_iota(jnp.int32, (ns_past, past_len), 0)).astype(BF16)
    chosen = jnp.dot(sel, expand, preferred_element_type=F32) > 0.5

    k_sel = jnp.concatenate([p[0, :, :KV_WIDTH] for p in pages], axis=0).astype(BF16)
    v_sel = jnp.concatenate([p[0, :, KV_WIDTH:] for p in pages], axis=0).astype(BF16)
    s_s = _dot_nt(q2, k_sel) * SCALE
    k_new = _own_group(snew_ref[0, :, :KV_WIDTH], row)
    v_new = _own_group(snew_ref[0, :, KV_WIDTH:], row)
    s_new = jnp.sum(qf * k_new, axis=-1, keepdims=True) * SCALE
    sm = jnp.where(chosen, s_s, NEG_INF)
    m_s = jnp.maximum(jnp.max(sm, axis=-1, keepdims=True), s_new)
    p_s = jnp.where(chosen, jnp.exp(sm - m_s), 0.0)
    p_new = jnp.exp(s_new - m_s)
    l_s = jnp.sum(p_s, axis=-1, keepdims=True) + p_new
    o_s = (_own_group(jnp.dot(p_s.astype(BF16), v_sel, preferred_element_type=F32), row) + p_new * v_new) / l_s

    k_win = win_ref[0, :, :KV_WIDTH].astype(BF16)
    v_win = win_ref[0, :, KV_WIDTH:].astype(BF16)
    wpos = past_len - win_buf + lax.broadcasted_iota(jnp.int32, (1, win_buf), 1)
    w_mask = ((past_len - wpos) <= WINDOW) & (wpos >= 0)
    s_w = jnp.where(w_mask, _dot_nt(q2, k_win) * SCALE, NEG_INF)
    kw_new = _own_group(wnew_ref[0, :, :KV_WIDTH], row)
    vw_new = _own_group(wnew_ref[0, :, KV_WIDTH:], row)
    sw_new = jnp.sum(qf * kw_new, axis=-1, keepdims=True) * SCALE
    m_w = jnp.maximum(jnp.max(s_w, axis=-1, keepdims=True), sw_new)
    p_w = jnp.where(w_mask, jnp.exp(s_w - m_w), 0.0)
    pw_new = jnp.exp(sw_new - m_w)
    l_w = jnp.sum(p_w, axis=-1, keepdims=True) + pw_new
    o_w = (_own_group(jnp.dot(p_w.astype(BF16), v_win, preferred_element_type=F32), row) + pw_new * vw_new) / l_w

    g_own = _own_group(gate_ref[0], row)
    lane = lax.broadcasted_iota(jnp.int32, (N_HEADS, LANES), 1)
    first = 3 * (row % Q_PER_KV)
    pick = lambda j: jnp.sum(jnp.where(lane == first + j, g_own, 0.0), axis=-1, keepdims=True)
    o = pick(0) * o_c + pick(1) * o_s + pick(2) * o_w
    ms = jnp.sum(jnp.sum(o * o, axis=-1, keepdims=True), axis=0, keepdims=True) / D_ATTN
    o_ref[0] = (o * lax.rsqrt(ms + EPS) * gain_ref[...]).astype(o_ref.dtype)


def _attn_sample(q_b, k_cmp, v_cmp, sel_new, cache_sel, state_win, win_new, gates, gain, layer, page_table):
    depth, n_phys, page = cache_sel.shape[:3]
    n_batch, n_pages = page_table.shape
    win_buf = state_win.shape[2]
    past_len = n_pages * page
    nc_past = past_len // CMP_BLOCK
    pages = cache_sel.reshape(depth * n_phys, page, ROW_WIDTH)
    wins = state_win.reshape(depth * n_batch, win_buf, ROW_WIDTH)
    base = layer * n_phys
    per_b = lambda shape: pl.BlockSpec((1,) + shape, lambda b, pt: (b, 0, 0))
    page_specs = [pl.BlockSpec((1, page, ROW_WIDTH), lambda b, pt, p=p: (base + pt[b, p], 0, 0))
                  for p in range(n_pages)]
    in_specs = [per_b((N_HEADS, HEAD_DIM)), per_b((nc_past, KV_WIDTH)), per_b((nc_past, KV_WIDTH)),
                per_b((1, ROW_WIDTH)),
                pl.BlockSpec((1, win_buf, ROW_WIDTH), lambda b, pt: (layer * n_batch + b, 0, 0)),
                per_b((1, ROW_WIDTH)), per_b((1, GATE_WIDTH)),
                pl.BlockSpec((N_HEADS, HEAD_DIM), lambda b, pt: (0, 0))] + page_specs
    out = pl.pallas_call(
        functools.partial(_attn_sample_kernel, n_pages=n_pages, page=page, win_buf=win_buf, past_len=past_len),
        grid_spec=pltpu.PrefetchScalarGridSpec(
            num_scalar_prefetch=1,
            grid=(n_batch,),
            in_specs=in_specs,
            out_specs=per_b((N_HEADS, HEAD_DIM)),
        ),
        out_shape=jax.ShapeDtypeStruct((n_batch, N_HEADS, HEAD_DIM), BF16),
        compiler_params=_params(("arbitrary",)),
        name="attn_sample",
    )(page_table, q_b.reshape(n_batch, N_HEADS, HEAD_DIM), k_cmp, v_cmp,
      sel_new.reshape(n_batch, 1, ROW_WIDTH), wins, win_new.reshape(n_batch, 1, ROW_WIDTH),
      gates.reshape(n_batch, 1, GATE_WIDTH), gain.reshape(N_HEADS, HEAD_DIM), *([pages] * n_pages))
    return out.reshape(n_batch, D_ATTN)


def _lru_gates(xc, wa_ref, wx_ref, ba, bx, lam):
    width = xc.shape[1]
    bd = width // LRU_BLOCKS
    xb = xc.astype(BF16)
    ra = jnp.concatenate([jnp.dot(xb[:, n * bd:(n + 1) * bd], wa_ref[n], preferred_element_type=F32)
                          for n in range(LRU_BLOCKS)], axis=1)
    rx = jnp.concatenate([jnp.dot(xb[:, n * bd:(n + 1) * bd], wx_ref[n], preferred_element_type=F32)
                          for n in range(LRU_BLOCKS)], axis=1)
    r = jax.nn.sigmoid(ra + ba)
    i = jax.nn.sigmoid(rx + bx)
    log_a = -LRU_C * r * _softplus(-lam)
    a = jnp.exp(log_a)
    b = jnp.sqrt(-jnp.tanh(log_a) * (a * a + 1.0)) * (i * xc)
    return a, b


def _scan_rows(a8, b8, h):
    row = lax.broadcasted_iota(jnp.int32, a8.shape, 0)
    s = 1
    while s < SUBLANES:
        a_prev = jnp.where(row >= s, pltpu.roll(a8, s, 0), 1.0)
        b_prev = jnp.where(row >= s, pltpu.roll(b8, s, 0), 0.0)
        b8 = a8 * b_prev + b8
        a8 = a8 * a_prev
        s *= 2
    return a8 * h + b8


def _lru_prompt_kernel(ux_ref, uy_ref, cw_ref, cb_ref, wa_ref, wx_ref, ba_ref, bx_ref, lam_ref, gain_ref,
                       r_ref, h_ref, x_s, h_s, a_s, b_s):
    tt = ux_ref.shape[0]
    i = pl.program_id(1)

    @pl.when(i == 0)
    def _():
        x_s[0:SUBLANES, :] = jnp.zeros((SUBLANES, x_s.shape[1]), F32)
        h_s[...] = jnp.zeros_like(h_s)

    @pl.when(i > 0)
    def _():
        x_s[0:SUBLANES, :] = x_s[tt:tt + SUBLANES, :]

    x_s[SUBLANES:, :] = ux_ref[...]
    xc = cb_ref[...]
    for j in range(CONV_W):
        k = CONV_W - 1 - j
        xc = xc + x_s[SUBLANES - k:SUBLANES - k + tt, :] * cw_ref[j:j + 1, :]
    a, b = _lru_gates(xc, wa_ref, wx_ref, ba_ref[...], bx_ref[...], lam_ref[...])
    a_s[...] = a
    b_s[...] = b

    def group(k, h):
        r0 = pl.multiple_of(k * SUBLANES, SUBLANES)
        hs = _scan_rows(a_s[pl.ds(r0, SUBLANES), :], b_s[pl.ds(r0, SUBLANES), :], h)
        a_s[pl.ds(r0, SUBLANES), :] = hs
        return hs[SUBLANES - 1:SUBLANES, :]

    h_last = lax.fori_loop(0, tt // SUBLANES, group, h_s[...])
    h_s[...] = h_last
    h_ref[0] = h_last
    y = a_s[...] * jax.nn.gelu(uy_ref[...])
    r_ref[...] = _rms(y, gain_ref[...]).astype(r_ref.dtype)


def _lru_prompt(proj, n_batch, seq_len, weights):
    width = D_ATTN
    tt = _row_tile(seq_len, 512, SUBLANES)
    n_t = seq_len // tt
    rows = lambda col: pl.BlockSpec((tt, width), lambda b, i: (b * n_t + i, col))
    vec = _full((1, width))
    blk = _full((LRU_BLOCKS, width // LRU_BLOCKS, width // LRU_BLOCKS))
    return pl.pallas_call(
        _lru_prompt_kernel,
        grid=(n_batch, n_t),
        in_specs=[rows(COL_UX // width), rows(COL_UY // width), _full((CONV_W, width)), vec, blk, blk,
                  vec, vec, vec, vec],
        out_specs=[pl.BlockSpec((tt, width), lambda b, i: (b * n_t + i, 0)),
                   pl.BlockSpec((1, 1, width), lambda b, i: (b, 0, 0))],
        out_shape=[jax.ShapeDtypeStruct((n_batch * seq_len, width), BF16),
                   jax.ShapeDtypeStruct((n_batch, 1, width), F32)],
        scratch_shapes=[pltpu.VMEM((tt + SUBLANES, width), F32), pltpu.VMEM((1, width), F32),
                        pltpu.VMEM((tt, width), F32), pltpu.VMEM((tt, width), F32)],
        compiler_params=_params(("parallel", "arbitrary")),
        name="lru_prompt",
    )(proj, proj, *weights)


def _lru_sample_kernel(ux_ref, uy_ref, c0_ref, c1_ref, c2_ref, h0_ref, cw_ref, cb_ref, wa_ref, wx_ref,
                       ba_ref, bx_ref, lam_ref, gain_ref, r_ref, h_ref):
    xc = (cb_ref[...] + c0_ref[...] * cw_ref[0:1, :] + c1_ref[...] * cw_ref[1:2, :]
          + c2_ref[...] * cw_ref[2:3, :] + ux_ref[...] * cw_ref[3:4, :])
    a, b = _lru_gates(xc, wa_ref, wx_ref, ba_ref[...], bx_ref[...], lam_ref[...])
    h = a * h0_ref[...] + b
    h_ref[...] = h
    r_ref[...] = _rms(h * jax.nn.gelu(uy_ref[...]), gain_ref[...]).astype(r_ref.dtype)


def _lru_sample(ux, uy, conv_state, h0, weights):
    n, width = ux.shape
    mat = _full((n, width))
    vec = _full((1, width))
    blk = _full((LRU_BLOCKS, width // LRU_BLOCKS, width // LRU_BLOCKS))
    return pl.pallas_call(
        _lru_sample_kernel,
        grid=(1,),
        in_specs=[mat] * 6 + [_full((CONV_W, width)), vec, blk, blk, vec, vec, vec, vec],
        out_specs=[mat, mat],
        out_shape=[jax.ShapeDtypeStruct((n, width), BF16), jax.ShapeDtypeStruct((n, width), F32)],
        compiler_params=_params(("arbitrary",)),
        name="lru_sample",
    )(ux, uy, conv_state[:, 0], conv_state[:, 1], conv_state[:, 2], h0, *weights)


def _router_kernel(x_ref, w_ref, o_ref, *, n_experts):
    logits = jnp.dot(x_ref[...], w_ref[...], preferred_element_type=F32)
    lane = lax.broadcasted_iota(jnp.int32, logits.shape, 1).astype(F32)
    logits = jnp.where(lane < n_experts, logits, -jnp.inf)
    m1 = jnp.max(logits, axis=-1, keepdims=True)
    i1 = jnp.min(jnp.where(logits == m1, lane, float(LANES)), axis=-1, keepdims=True)
    rest = jnp.where(lane == i1, -jnp.inf, logits)
    m2 = jnp.max(rest, axis=-1, keepdims=True)
    i2 = jnp.min(jnp.where(rest == m2, lane, float(LANES)), axis=-1, keepdims=True)
    e2 = jnp.exp(m2 - m1)
    den = 1.0 + e2
    o_ref[...] = jnp.where(lane == i1, 1.0 / den, 0.0) + jnp.where(lane == i2, e2 / den, 0.0)


def _router(hn, w_router):
    n, d = hn.shape
    n_experts = w_router.shape[1]
    tm = _row_tile(n, 832)
    w = jnp.concatenate([w_router, jnp.zeros((d, LANES - n_experts), w_router.dtype)], axis=1).astype(BF16)
    return pl.pallas_call(
        functools.partial(_router_kernel, n_experts=n_experts),
        grid=(n // tm,),
        in_specs=[pl.BlockSpec((tm, d), lambda i: (i, 0)), _full((d, LANES))],
        out_specs=pl.BlockSpec((tm, LANES), lambda i: (i, 0)),
        out_shape=jax.ShapeDtypeStruct((n, LANES), F32),
        compiler_params=_params(("parallel",)),
        name="router",
    )(hn, w)


def _ffn_kernel(x_ref, gate_ref, wg_ref, wu_ref, wd_ref, res_ref, o_ref, acc_s, *, gated):
    e, f = pl.program_id(1), pl.program_id(2)
    n_f = pl.num_programs(2)

    @pl.when((e == 0) & (f == 0))
    def _():
        o_ref[...] = res_ref[...]

    @pl.when(f == 0)
    def _():
        acc_s[...] = jnp.zeros_like(acc_s)

    x = x_ref[...]
    g = jnp.dot(x, wg_ref[0].astype(BF16), preferred_element_type=F32)
    u = jnp.dot(x, wu_ref[0].astype(BF16), preferred_element_type=F32)
    hid = (jax.nn.silu(g) * u).astype(BF16)
    acc_s[...] += jnp.dot(hid, wd_ref[0].astype(BF16), preferred_element_type=F32)

    @pl.when(f == n_f - 1)
    def _():
        if gated:
            lane = lax.broadcasted_iota(jnp.int32, gate_ref.shape, 1)
            w = jnp.sum(jnp.where(lane == e, gate_ref[...], 0.0), axis=-1, keepdims=True)
            o_ref[...] += w * acc_s[...]
        else:
            o_ref[...] += acc_s[...]


def _ffn(hn, gate, wg, wu, wd, res, *, gated):
    n, d = hn.shape
    n_e, _, d_ff = wg.shape
    tm = _row_tile(n, 640)
    tf = _row_tile(d_ff, 256, LANES)
    row = lambda width: pl.BlockSpec((tm, width), lambda i, e, f: (i, 0))
    return pl.pallas_call(
        functools.partial(_ffn_kernel, gated=gated),
        grid=(n // tm, n_e, d_ff // tf),
        in_specs=[row(d), row(LANES),
                  pl.BlockSpec((1, d, tf), lambda i, e, f: (e, 0, f)),
                  pl.BlockSpec((1, d, tf), lambda i, e, f: (e, 0, f)),
                  pl.BlockSpec((1, tf, d), lambda i, e, f: (e, f, 0)),
                  row(d)],
        out_specs=row(d),
        out_shape=jax.ShapeDtypeStruct((n, d), F32),
        scratch_shapes=[pltpu.VMEM((tm, d), F32)],
        compiler_params=_params(("parallel", "arbitrary", "arbitrary")),
        name="ffn",
    )(hn, gate, wg, wu, wd, res)


def kernel(x_prompt, x_sample, cache_cmp, cache_sel, state_win, state_lru, state_conv, page_table, ln_mix, w_in, q_norm, k_norm, cmp_pe, cmp_w1, cmp_w2, conv_w, conv_b, lru_wa, lru_ba, lru_wx, lru_bx, lru_lambda, out_norm_attn, out_norm_lru, w_out, ln_ffn, ffn_w_gate, ffn_w_up, ffn_w_down, moe_router, moe_w_gate, moe_w_up, moe_w_down):
    n_batch, seq_len, d_model = x_prompt.shape
    dec_batch, dec_seq, _ = x_sample.shape
    depth = ln_mix.shape[0]
    page = cache_cmp.shape[2]
    n_pages = page_table.shape[1]
    past_len = n_pages * page
    win_buf = state_win.shape[2]
    lru_width = d_model - D_ATTN
    assert dec_seq == 1 and lru_width == D_ATTN
    assert seq_len % Q_BLOCK == 0 and seq_len >= WIN_SPAN and seq_len % SEL_CHUNK == 0 and seq_len >= win_buf
    assert past_len % Q_BLOCK == 0 and win_buf <= past_len and page % CMP_BLOCK == 0
    n_p = n_batch * seq_len
    nc = seq_len // CMP_BLOCK
    nc_past = past_len // CMP_BLOCK
    kv_row = (2, N_KV, HEAD_DIM)

    pos = jnp.concatenate([jnp.tile(jnp.arange(seq_len), n_batch), jnp.full((dec_batch,), past_len)])
    tables = _rope_tables(pos)
    end_p = _rope_tables((jnp.arange(nc) + 1) * CMP_BLOCK - 1)
    end_s = _rope_tables((jnp.arange(nc_past) + 1) * CMP_BLOCK - 1)

    x = jnp.concatenate([x_prompt.reshape(n_p, d_model), x_sample.reshape(dec_batch, d_model)], axis=0)
    ones_gate = jnp.ones((x.shape[0], LANES), F32)
    outs = [[] for _ in range(10)]
    for l in range(depth):
        hn = _rmsnorm(x, ln_mix[l])
        proj = _matmul([hn], [_reorder_w_in(w_in[l])], tm_target=832, tn_target=1280)
        q_b, sel_f, sel_b, win_f, win_b, gates = _postproj(proj, tables, q_norm[l], k_norm[l, 1], k_norm[l, 2])
        c_rows = proj[:, COL_CMP:COL_CMP + ROW_WIDTH]

        cmp_w = (cmp_pe[l], cmp_w1[l].reshape(2, CMP_BLOCK, HEAD_DIM, HEAD_DIM).astype(BF16),
                 cmp_w2[l].astype(BF16), k_norm[l, 0].reshape(1, HEAD_DIM))
        lru_w = (conv_w[l], conv_b[l].reshape(1, -1), lru_wa[l].astype(BF16), lru_wx[l].astype(BF16),
                 lru_ba[l].reshape(1, -1), lru_bx[l].reshape(1, -1), lru_lambda[l].reshape(1, -1),
                 out_norm_lru[l].reshape(1, -1))

        kc_p, vc_p = _compress_prompt(c_rows[:n_p].reshape(n_batch * nc, CMP_ROW), n_batch, nc, cmp_w, end_p)
        o_p = _attn_prompt(q_b, _deinterleave_blocks(kc_p, n_batch, nc), _deinterleave_blocks(vc_p, n_batch, nc),
                           sel_b, win_b, gates, out_norm_attn[l], n_batch, seq_len)
        r_p, h_p = _lru_prompt(proj, n_batch, seq_len, lru_w)

        kc_s, vc_s = _compress_sample(cache_cmp, l, page_table, cmp_w, end_s, nb=2 if dec_batch % 2 == 0 else 1)
        o_s = _attn_sample(q_b[n_p:], _deinterleave_blocks(kc_s, dec_batch, nc_past),
                           _deinterleave_blocks(vc_s, dec_batch, nc_past), sel_f[n_p:], cache_sel, state_win,
                           win_f[n_p:], gates[n_p:], out_norm_attn[l], l, page_table)
        ux_s = proj[n_p:, COL_UX:COL_UX + lru_width]
        r_s, h_s = _lru_sample(ux_s, proj[n_p:, COL_UY:COL_UY + lru_width], state_conv[l], state_lru[l], lru_w)

        w_o = w_out[l].astype(BF16)
        x = _matmul([jnp.concatenate([o_p, o_s], axis=0), jnp.concatenate([r_p, r_s], axis=0)],
                    [w_o[:D_ATTN], w_o[D_ATTN:]], res=x, tm_target=832, tn_target=1024)

        hn = _rmsnorm(x, ln_ffn[l])
        if l % 2 == 0:
            k = l // 2
            x = _ffn(hn, ones_gate, ffn_w_gate[k:k + 1], ffn_w_up[k:k + 1], ffn_w_down[k:k + 1], x, gated=False)
        else:
            k = l // 2
            x = _ffn(hn, _router(hn, moe_router[k]), moe_w_gate[k], moe_w_up[k], moe_w_down[k], x, gated=True)

        ux_p = proj[:n_p, COL_UX:COL_UX + lru_width].reshape(n_batch, seq_len, lru_width)
        new = (c_rows[:n_p].reshape((n_batch, seq_len) + kv_row), c_rows[n_p:].reshape((dec_batch, 1) + kv_row),
               sel_f[:n_p].reshape((n_batch, seq_len) + kv_row), sel_f[n_p:].reshape((dec_batch, 1) + kv_row),
               win_f[:n_p].reshape((n_batch, seq_len) + kv_row)[:, seq_len - win_buf:],
               jnp.concatenate([state_win[l][:, 1:], win_f[n_p:].reshape((dec_batch, 1) + kv_row)], axis=1),
               h_p.reshape(n_batch, lru_width), h_s,
               ux_p[:, seq_len - (CONV_W - 1):],
               jnp.concatenate([state_conv[l][:, 1:], ux_s[:, None, :]], axis=1))
        for acc, val in zip(outs, new):
            acc.append(val)

    return (x[:n_p].reshape(n_batch, seq_len, d_model), x[n_p:].reshape(dec_batch, 1, d_model),
            *[jnp.stack(v) for v in outs])
```

```python
import functools

import jax
import jax.numpy as jnp
from jax import lax
from jax.experimental import pallas as pl
from jax.experimental.pallas import tpu as pltpu

N_HEADS = 8
HEAD_DIM = 128
N_KV = 2
Q_PER_KV = N_HEADS // N_KV
D_ATTN = N_HEADS * HEAD_DIM
KV_WIDTH = N_KV * HEAD_DIM
ROW_WIDTH = 2 * KV_WIDTH
ROT_DIM = HEAD_DIM // 4
ROPE_THETA = 500000.0
CMP_BLOCK = 32
SEL_BLOCK = 64
TOP_N = 16
WINDOW = 512
Q_BLOCK = 128
LRU_BLOCKS = 8
CONV_W = 4
LRU_C = 8.0
TOP_K = 2
EPS = 1e-6
SCALE = HEAD_DIM ** -0.5
NEG_INF = -1e30

LANES = 128
SUBLANES = 8
VMEM_LIMIT = 56 * 1024 * 1024

F32 = jnp.float32
BF16 = jnp.bfloat16


def _row_tile(n, target, mult=16):
    best = None
    for t in range(mult, min(n, target) + 1, mult):
        if n % t == 0:
            best = t
    return best if best is not None else n


def _params(semantics):
    return pltpu.CompilerParams(dimension_semantics=semantics, vmem_limit_bytes=VMEM_LIMIT)


def _full(shape):
    zeros = (0,) * len(shape)
    return pl.BlockSpec(shape, lambda *_: zeros)


def _rms(x, gain):
    return x * lax.rsqrt(jnp.mean(x * x, axis=-1, keepdims=True) + EPS) * gain


def _rope_apply(y, cos_t, sin_lo, sin_hi):
    half = ROT_DIM // 2
    return y * cos_t + pltpu.roll(y, HEAD_DIM - half, 1) * sin_lo + pltpu.roll(y, half, 1) * sin_hi


def _softplus(x):
    return jnp.maximum(x, 0.0) + jnp.log1p(jnp.exp(-jnp.abs(x)))


def _topk_mask(score, n_pick):
    n_lanes = score.shape[-1]
    lane = lax.broadcasted_iota(jnp.int32, score.shape, 1).astype(F32)
    sel = jnp.zeros(score.shape, F32)
    for _ in range(n_pick):
        m = jnp.max(score, axis=-1, keepdims=True)
        idx = jnp.min(jnp.where(score == m, lane, float(n_lanes)), axis=-1, keepdims=True)
        pick = lane == idx
        sel = jnp.where(pick, 1.0, sel)
        score = jnp.where(pick, -2.0, score)
    return sel


def _masked_softmax_rows(s, mask):
    sm = jnp.where(mask, s, NEG_INF)
    m = jnp.max(sm, axis=-1, keepdims=True)
    e = jnp.where(mask, jnp.exp(sm - m), 0.0)
    l = jnp.sum(e, axis=-1, keepdims=True)
    return e / jnp.where(l > 0.0, l, 1.0)


def _dot_nt(a, b):
    return lax.dot_general(a, b, (((1,), (1,)), ((), ())), preferred_element_type=F32)


def _rmsnorm_kernel(x_ref, g_ref, o_ref):
    o_ref[...] = _rms(x_ref[...], g_ref[...]).astype(o_ref.dtype)


def _rmsnorm(x, gain):
    n, d = x.shape
    tm = _row_tile(n, 832)
    return pl.pallas_call(
        _rmsnorm_kernel,
        grid=(n // tm,),
        in_specs=[pl.BlockSpec((tm, d), lambda i: (i, 0)), _full((1, d))],
        out_specs=pl.BlockSpec((tm, d), lambda i: (i, 0)),
        out_shape=jax.ShapeDtypeStruct((n, d), BF16),
        compiler_params=_params(("parallel",)),
        name="rmsnorm",
    )(x, gain.reshape(1, d))


def _matmul_kernel(*refs, n_lhs, has_res):
    o_ref = refs[-1]
    acc = jnp.dot(refs[0][...], refs[n_lhs][...], preferred_element_type=F32)
    for k in range(1, n_lhs):
        acc = acc + jnp.dot(refs[k][...], refs[n_lhs + k][...], preferred_element_type=F32)
    if has_res:
        acc = acc + refs[2 * n_lhs][...]
    o_ref[...] = acc


def _matmul(xs, ws, res=None, *, tm_target, tn_target):
    n = xs[0].shape[0]
    d_out = ws[0].shape[1]
    tm = _row_tile(n, tm_target)
    tn = _row_tile(d_out, tn_target, LANES)
    in_specs = [pl.BlockSpec((tm, x.shape[1]), lambda i, j: (i, 0)) for x in xs]
    in_specs += [pl.BlockSpec((w.shape[0], tn), lambda i, j: (0, j)) for w in ws]
    args = list(xs) + list(ws)
    if res is not None:
        in_specs.append(pl.BlockSpec((tm, tn), lambda i, j: (i, j)))
        args.append(res)
    return pl.pallas_call(
        functools.partial(_matmul_kernel, n_lhs=len(xs), has_res=res is not None),
        grid=(n // tm, d_out // tn),
        in_specs=in_specs,
        out_specs=pl.BlockSpec((tm, tn), lambda i, j: (i, j)),
        out_shape=jax.ShapeDtypeStruct((n, d_out), F32),
        compiler_params=_params(("parallel", "parallel")),
        name="matmul",
    )(*args)


PROJ_WIDTH = 5120
COL_UX, COL_UY, COL_CMP, COL_SEL, COL_WIN, COL_GATE = 1024, 2048, 3072, 3584, 4096, 4608
GATE_WIDTH = 2 * LANES


def _reorder_w_in(w):
    lru = D_ATTN
    c_kv = D_ATTN
    c_g = c_kv + 3 * ROW_WIDTH
    c_ux = c_g + 3 * N_HEADS
    c_uy = c_ux + lru
    n_g = 3 * Q_PER_KV
    zpad = lambda k: jnp.zeros((w.shape[0], k), w.dtype)
    parts = [w[:, :D_ATTN], w[:, c_ux:c_uy], w[:, c_uy:c_uy + lru], w[:, c_kv:c_g],
             w[:, c_g:c_g + n_g], zpad(LANES - n_g), w[:, c_g + n_g:c_ux], zpad(LANES - n_g)]
    width = sum(p.shape[1] for p in parts)
    parts.append(zpad(PROJ_WIDTH - width))
    return jnp.concatenate(parts, axis=1).astype(BF16)


def _postproj_kernel(q_ref, sel_ref, win_ref, gate_ref, cos_ref, slo_ref, shi_ref,
                     qg_ref, ksg_ref, kwg_ref,
                     qo_ref, sel_f_ref, sel_b_ref, win_f_ref, win_b_ref, gate_o_ref):
    cos_t, slo, shi = cos_ref[...], slo_ref[...], shi_ref[...]
    for h in range(N_HEADS):
        cols = slice(h * HEAD_DIM, (h + 1) * HEAD_DIM)
        y = _rope_apply(_rms(q_ref[:, cols], qg_ref[...]), cos_t, slo, shi)
        qo_ref[:, cols] = y.astype(qo_ref.dtype)
    for src, gain_ref, out_f, out_b in ((sel_ref, ksg_ref, sel_f_ref, sel_b_ref),
                                        (win_ref, kwg_ref, win_f_ref, win_b_ref)):
        for g in range(N_KV):
            cols = slice(g * HEAD_DIM, (g + 1) * HEAD_DIM)
            y = _rope_apply(_rms(src[:, cols], gain_ref[...]), cos_t, slo, shi)
            out_f[:, cols] = y
            out_b[:, cols] = y.astype(out_b.dtype)
        v = src[:, KV_WIDTH:]
        out_f[:, KV_WIDTH:] = v
        out_b[:, KV_WIDTH:] = v.astype(out_b.dtype)
    gate_o_ref[...] = jax.nn.sigmoid(gate_ref[...])


def _postproj(proj, tables, q_gain, ks_gain, kw_gain):
    n = proj.shape[0]
    tm = _row_tile(n, 416)
    rows = lambda width, col: pl.BlockSpec((tm, width), lambda i: (i, col))
    tab = pl.BlockSpec((tm, HEAD_DIM), lambda i: (i, 0))
    gain = _full((1, HEAD_DIM))
    out = lambda width: pl.BlockSpec((tm, width), lambda i: (i, 0))
    sds = jax.ShapeDtypeStruct
    return pl.pallas_call(
        _postproj_kernel,
        grid=(n // tm,),
        in_specs=[rows(D_ATTN, 0), rows(ROW_WIDTH, COL_SEL // ROW_WIDTH), rows(ROW_WIDTH, COL_WIN // ROW_WIDTH),
                  rows(GATE_WIDTH, COL_GATE // GATE_WIDTH), tab, tab, tab, gain, gain, gain],
        out_specs=[out(D_ATTN), out(ROW_WIDTH), out(ROW_WIDTH), out(ROW_WIDTH), out(ROW_WIDTH), out(GATE_WIDTH)],
        out_shape=[sds((n, D_ATTN), BF16), sds((n, ROW_WIDTH), F32), sds((n, ROW_WIDTH), BF16),
                   sds((n, ROW_WIDTH), F32), sds((n, ROW_WIDTH), BF16), sds((n, GATE_WIDTH), F32)],
        compiler_params=_params(("parallel",)),
        name="postproj",
    )(proj, proj, proj, proj, *tables, q_gain.reshape(1, -1), ks_gain.reshape(1, -1), kw_gain.reshape(1, -1))


def _rope_tables(pos):
    half = ROT_DIM // 2
    inv_freq = ROPE_THETA ** (-2.0 * jnp.arange(half, dtype=F32) / ROT_DIM)
    ang = pos.astype(F32)[:, None] * inv_freq
    cos, sin = jnp.cos(ang), jnp.sin(ang)
    n = pos.shape[0]
    ones = jnp.ones((n, HEAD_DIM - ROT_DIM), F32)
    zeros = jnp.zeros((n, HEAD_DIM - half), F32)
    cos_t = jnp.concatenate([cos, cos, ones], axis=1)
    sin_lo = jnp.concatenate([-sin, zeros], axis=1)
    sin_hi = jnp.concatenate([jnp.zeros((n, half), F32), sin, jnp.zeros((n, HEAD_DIM - ROT_DIM), F32)], axis=1)
    return cos_t, sin_lo, sin_hi


CMP_ROW = CMP_BLOCK * ROW_WIDTH


def _compress_rows(load_cols, n_rows, pe_ref, w1_ref, w2_ref, gain_ref, cos_ref, slo_ref, shi_ref,
                   k_ref, v_ref):
    for n in range(2):
        acc = jnp.zeros((N_KV * n_rows, HEAD_DIM), F32)
        for t in range(CMP_BLOCK):
            base = t * ROW_WIDTH + n * KV_WIDTH
            pe = pe_ref[n, t:t + 1, :]
            lhs = jnp.concatenate([(load_cols(base + g * HEAD_DIM) + pe).astype(BF16) for g in range(N_KV)], axis=0)
            acc = acc + jnp.dot(lhs, w1_ref[n, t], preferred_element_type=F32)
        hid = jax.nn.gelu(acc).astype(BF16)
        comp = jnp.dot(hid, w2_ref[n], preferred_element_type=F32)
        for g in range(N_KV):
            part = comp[g * n_rows:(g + 1) * n_rows]
            cols = slice(g * HEAD_DIM, (g + 1) * HEAD_DIM)
            if n == 0:
                part = _rope_apply(_rms(part, gain_ref[...]), cos_ref[...], slo_ref[...], shi_ref[...])
                k_ref[:, cols] = part.astype(k_ref.dtype)
            else:
                v_ref[:, cols] = part.astype(v_ref.dtype)


def _compress_prompt_kernel(x_ref, *rest):
    n_rows = x_ref.shape[0]
    _compress_rows(lambda c0: x_ref[:, c0:c0 + HEAD_DIM], n_rows, *rest)


def _compress_sample_kernel(pt_ref, *refs, n_page_refs, rows_per_page):
    del pt_ref
    pages = refs[:n_page_refs]
    rest = refs[n_page_refs:-1]
    x_s = refs[-1]
    for k, page in enumerate(pages):
        x_s[k * rows_per_page:(k + 1) * rows_per_page, :] = page[0]
    n_rows = x_s.shape[0]
    _compress_rows(lambda c0: x_s[:, c0:c0 + HEAD_DIM], n_rows, *rest)


def _compress_weight_specs():
    return [_full((2, CMP_BLOCK, HEAD_DIM)), _full((2, CMP_BLOCK, HEAD_DIM, HEAD_DIM)),
            _full((2, HEAD_DIM, HEAD_DIM)), _full((1, HEAD_DIM))]


def _compress_prompt(c_rows, n_batch, nc, weights, end_tables):
    tr = _row_tile(nc, 128)
    per_b = nc // tr
    tab = pl.BlockSpec((tr, HEAD_DIM), lambda i: (i % per_b, 0))
    out = pl.BlockSpec((tr, KV_WIDTH), lambda i: (i, 0))
    sds = jax.ShapeDtypeStruct((n_batch * nc, KV_WIDTH), BF16)
    return pl.pallas_call(
        _compress_prompt_kernel,
        grid=(n_batch * per_b,),
        in_specs=[pl.BlockSpec((tr, CMP_ROW), lambda i: (i, 0))] + _compress_weight_specs() + [tab, tab, tab],
        out_specs=[out, out],
        out_shape=[sds, sds],
        compiler_params=_params(("parallel",)),
        name="compress_prompt",
    )(c_rows, *weights, *end_tables)


def _compress_sample(cache, layer, page_table, weights, end_tables, nb):
    depth, n_phys, page = cache.shape[:3]
    n_batch, n_pages = page_table.shape
    rows_per_page = page // CMP_BLOCK
    nc = n_pages * rows_per_page
    pages = cache.reshape(depth * n_phys, rows_per_page, CMP_ROW)
    base = layer * n_phys
    page_specs = [pl.BlockSpec((1, rows_per_page, CMP_ROW),
                               lambda i, pt, bb=bb, p=p: (base + pt[i * nb + bb, p], 0, 0))
                  for bb in range(nb) for p in range(n_pages)]
    n_rows = nb * nc
    tab = pl.BlockSpec((n_rows, HEAD_DIM), lambda i, pt: (0, 0))
    wspecs = [pl.BlockSpec(s.block_shape, lambda i, pt, z=(0,) * len(s.block_shape): z) for s in _compress_weight_specs()]
    out = pl.BlockSpec((n_rows, KV_WIDTH), lambda i, pt: (i, 0))
    sds = jax.ShapeDtypeStruct((n_batch * nc, KV_WIDTH), BF16)
    tables = [jnp.tile(t, (nb, 1)) for t in end_tables]
    return pl.pallas_call(
        functools.partial(_compress_sample_kernel, n_page_refs=nb * n_pages, rows_per_page=rows_per_page),
        grid_spec=pltpu.PrefetchScalarGridSpec(
            num_scalar_prefetch=1,
            grid=(n_batch // nb,),
            in_specs=page_specs + wspecs + [tab, tab, tab],
            out_specs=[out, out],
            scratch_shapes=[pltpu.VMEM((n_rows, CMP_ROW), F32)],
        ),
        out_shape=[sds, sds],
        compiler_params=_params(("arbitrary",)),
        name="compress_sample",
    )(page_table, *([pages] * (nb * n_pages)), *weights, *tables)


def _deinterleave_blocks(x, n_batch, nc):
    x = x.reshape(n_batch, nc, -1)
    return jnp.concatenate([x[:, 0::2], x[:, 1::2]], axis=1)


SEL_CHUNK = 256
WIN_SPAN = WINDOW + Q_BLOCK


def _attn_prompt_kernel(q_ref, kc_ref, vc_ref, sel_ref, win_ref, gate_ref, gain_ref, o_ref, *, seq_len):
    nc = seq_len // CMP_BLOCK
    ns = seq_len // SEL_BLOCK
    i = pl.program_id(1)
    q0 = i * Q_BLOCK
    tpos = q0 + lax.broadcasted_iota(jnp.int32, (Q_BLOCK, 1), 0)
    gates = gate_ref[...]
    slabs = []
    for g in range(N_KV):
        kcol = slice(g * HEAD_DIM, (g + 1) * HEAD_DIM)
        vcol = slice(KV_WIDTH + g * HEAD_DIM, KV_WIDTH + (g + 1) * HEAD_DIM)
        qs = jnp.concatenate([q_ref[:, (g * Q_PER_KV + h) * HEAD_DIM:(g * Q_PER_KV + h + 1) * HEAD_DIM]
                              for h in range(Q_PER_KV)], axis=0)

        col = lax.broadcasted_iota(jnp.int32, (Q_BLOCK, nc), 1)
        blk = jnp.where(col < ns, 2 * col, 2 * (col - ns) + 1)
        c_mask = ((blk + 1) * CMP_BLOCK - 1) <= tpos
        s_c = (_dot_nt(qs, kc_ref[0, :, kcol]) * SCALE).reshape(Q_PER_KV, Q_BLOCK, nc)
        p_c = _masked_softmax_rows(s_c, c_mask[None])
        o_c = jnp.dot(p_c.reshape(Q_PER_KV * Q_BLOCK, nc).astype(BF16), vc_ref[0, :, kcol],
                      preferred_element_type=F32)
        p_tok = p_c[0] + p_c[1] + p_c[2] + p_c[3]
        imp = p_tok[:, :ns] + p_tok[:, ns:]

        sblk = lax.broadcasted_iota(jnp.int32, (Q_BLOCK, ns), 1)
        allowed = sblk * SEL_BLOCK <= tpos
        forced = (sblk == 0) | (sblk == tpos // SEL_BLOCK)
        score = jnp.where(forced, 1e30, jnp.where(allowed, imp, -1.0))
        sel = _topk_mask(score, min(TOP_N, ns)).astype(BF16)

        def sel_step(j, carry):
            m, l, acc = carry
            k0 = pl.multiple_of(j * SEL_CHUNK, SEL_CHUNK)
            kpos = k0 + lax.broadcasted_iota(jnp.int32, (1, SEL_CHUNK), 1)
            expand = (kpos // SEL_BLOCK == lax.broadcasted_iota(jnp.int32, (ns, SEL_CHUNK), 0)).astype(BF16)
            chosen = jnp.dot(sel, expand, preferred_element_type=F32) > 0.5
            mask = (chosen & (kpos <= tpos))[None]
            s = (_dot_nt(qs, sel_ref[pl.ds(k0, SEL_CHUNK), kcol]) * SCALE).reshape(Q_PER_KV, Q_BLOCK, SEL_CHUNK)
            sm = jnp.where(mask, s, NEG_INF)
            m_new = jnp.maximum(m, jnp.max(sm, axis=-1, keepdims=True))
            p = jnp.where(mask, jnp.exp(sm - m_new), 0.0)
            alpha = jnp.exp(m - m_new)
            l = alpha * l + jnp.sum(p, axis=-1, keepdims=True)
            pv = jnp.dot(p.reshape(Q_PER_KV * Q_BLOCK, SEL_CHUNK).astype(BF16),
                         sel_ref[pl.ds(k0, SEL_CHUNK), vcol], preferred_element_type=F32)
            acc = alpha * acc + pv.reshape(Q_PER_KV, Q_BLOCK, HEAD_DIM)
            return m_new, l, acc

        n_steps = (q0 + Q_BLOCK + SEL_CHUNK - 1) // SEL_CHUNK
        init = (jnp.full((Q_PER_KV, Q_BLOCK, 1), NEG_INF, F32), jnp.zeros((Q_PER_KV, Q_BLOCK, 1), F32),
                jnp.zeros((Q_PER_KV, Q_BLOCK, HEAD_DIM), F32))
        _, l_s, acc_s = lax.fori_loop(0, n_steps, sel_step, init)
        o_s = (acc_s / l_s).reshape(Q_PER_KV * Q_BLOCK, HEAD_DIM)

        w0 = pl.multiple_of(jnp.maximum(q0 - WINDOW, 0), Q_BLOCK)
        kpos = w0 + lax.broadcasted_iota(jnp.int32, (1, WIN_SPAN), 1)
        dist = tpos - kpos
        w_mask = ((dist >= 0) & (dist <= WINDOW))[None]
        s_w = (_dot_nt(qs, win_ref[pl.ds(w0, WIN_SPAN), kcol]) * SCALE).reshape(Q_PER_KV, Q_BLOCK, WIN_SPAN)
        p_w = _masked_softmax_rows(s_w, w_mask)
        o_w = jnp.dot(p_w.reshape(Q_PER_KV * Q_BLOCK, WIN_SPAN).astype(BF16), win_ref[pl.ds(w0, WIN_SPAN), vcol],
                      preferred_element_type=F32)

        for h in range(Q_PER_KV):
            rows = slice(h * Q_BLOCK, (h + 1) * Q_BLOCK)
            c = g * LANES + 3 * h
            slabs.append(gates[:, c:c + 1] * o_c[rows] + gates[:, c + 1:c + 2] * o_s[rows]
                         + gates[:, c + 2:c + 3] * o_w[rows])
    o = jnp.concatenate(slabs, axis=1)
    o_ref[...] = _rms(o, gain_ref[...]).astype(o_ref.dtype)


def _attn_prompt(q_b, k_cmp, v_cmp, sel_b, win_b, gates, gain, n_batch, seq_len):
    n_qb = seq_len // Q_BLOCK
    nc = seq_len // CMP_BLOCK
    qrow = lambda width: pl.BlockSpec((Q_BLOCK, width), lambda b, i: (b * n_qb + i, 0))
    cmp_spec = pl.BlockSpec((1, nc, KV_WIDTH), lambda b, i: (b, 0, 0))
    seq_spec = pl.BlockSpec((seq_len, ROW_WIDTH), lambda b, i: (b, 0))
    return pl.pallas_call(
        functools.partial(_attn_prompt_kernel, seq_len=seq_len),
        grid=(n_batch, n_qb),
        in_specs=[qrow(D_ATTN), cmp_spec, cmp_spec, seq_spec, seq_spec, qrow(GATE_WIDTH), _full((1, D_ATTN))],
        out_specs=qrow(D_ATTN),
        out_shape=jax.ShapeDtypeStruct((n_batch * seq_len, D_ATTN), BF16),
        compiler_params=_params(("parallel", "arbitrary")),
        name="attn_prompt",
    )(q_b, k_cmp, v_cmp, sel_b, win_b, gates, gain.reshape(1, -1))


UNITS = 2 * N_KV


def _unit_rows(refs, n_tokens, first_unit):
    cols = [jnp.concatenate([r[pl.ds(first_unit + g, n_tokens, stride=UNITS), :] for r in refs], axis=0)
            for g in range(N_KV)]
    return jnp.concatenate(cols, axis=1).astype(BF16)


def _own_group(x, row):
    return jnp.where(row < Q_PER_KV, x[:, :HEAD_DIM], x[:, HEAD_DIM:])


def _attn_sample_kernel(pt_ref, *refs, n_pages, page, win_buf, past_len):
    del pt_ref
    q_ref, kc_ref, vc_ref, snew_ref, win_ref, wnew_ref, gate_ref, gain_ref = refs[:8]
    pages = refs[8:8 + n_pages]
    o_ref = refs[8 + n_pages]
    ns_past = past_len // SEL_BLOCK
    nc_past = past_len // CMP_BLOCK

    q = q_ref[0]
    row = lax.broadcasted_iota(jnp.int32, (N_HEADS, 1), 0)
    zero = jnp.zeros_like(q)
    q2 = jnp.concatenate([jnp.where(row < Q_PER_KV, q, zero), jnp.where(row < Q_PER_KV, zero, q)], axis=1)
    qf = q.astype(F32)

    s_c = _dot_nt(q2, kc_ref[0]) * SCALE
    m_c = jnp.max(s_c, axis=-1, keepdims=True)
    e_c = jnp.exp(s_c - m_c)
    p_c = e_c / jnp.sum(e_c, axis=-1, keepdims=True)
    o_c = _own_group(jnp.dot(p_c.astype(BF16), vc_ref[0], preferred_element_type=F32), row)
    p_grp = jnp.where(row < Q_PER_KV, jnp.sum(p_c[:Q_PER_KV], axis=0, keepdims=True),
                      jnp.sum(p_c[Q_PER_KV:], axis=0, keepdims=True))
    imp = p_grp[:, :ns_past] + p_grp[:, ns_past:]

    sblk = lax.broadcasted_iota(jnp.int32, (N_HEADS, ns_past), 1)
    score = jnp.where(sblk == 0, 1e30, imp)
    sel = _topk_mask(score, min(TOP_N, ns_past + 1) - 1).astype(BF16)

    kpos = lax.broadcasted_iota(jnp.int32, (1, past_len), 1)
    expand = (kpos // SEL_BLOCK == lax.broadcasted_iota(jnp.int32, (ns_past, past_len), 0)).astype(BF16)
    chosen = jnp.dot(sel, expand, preferred_element_type=F32) > 0.5

    k_sel = _unit_rows(pages, page, 0)
    v_sel = _unit_rows(pages, page, N_KV)
    s_s = _dot_nt(q2, k_sel) * SCALE
    k_new = _own_group(snew_ref[0, :, :KV_WIDTH], row)
    v_new = _own_group(snew_ref[0, :, KV_WIDTH:], row)
    s_new = jnp.sum(qf * k_new, axis=-1, keepdims=True) * SCALE
    sm = jnp.where(chosen, s_s, NEG_INF)
    m_s = jnp.maximum(jnp.max(sm, axis=-1, keepdims=True), s_new)
    p_s = jnp.where(chosen, jnp.exp(sm - m_s), 0.0)
    p_new = jnp.exp(s_new - m_s)
    l_s = jnp.sum(p_s, axis=-1, keepdims=True) + p_new
    o_s = (_own_group(jnp.dot(p_s.astype(BF16), v_sel, preferred_element_type=F32), row) + p_new * v_new) / l_s

    k_win = _unit_rows([win_ref], win_buf, 0)
    v_win = _unit_rows([win_ref], win_buf, N_KV)
    wpos = past_len - win_buf + lax.broadcasted_iota(jnp.int32, (1, win_buf), 1)
    w_mask = ((past_len - wpos) <= WINDOW) & (wpos >= 0)
    s_w = jnp.where(w_mask, _dot_nt(q2, k_win) * SCALE, NEG_INF)
    kw_new = _own_group(wnew_ref[0, :, :KV_WIDTH], row)
    vw_new = _own_group(wnew_ref[0, :, KV_WIDTH:], row)
    sw_new = jnp.sum(qf * kw_new, axis=-1, keepdims=True) * SCALE
    m_w = jnp.maximum(jnp.max(s_w, axis=-1, keepdims=True), sw_new)
    p_w = jnp.where(w_mask, jnp.exp(s_w - m_w), 0.0)
    pw_new = jnp.exp(sw_new - m_w)
    l_w = jnp.sum(p_w, axis=-1, keepdims=True) + pw_new
    o_w = (_own_group(jnp.dot(p_w.astype(BF16), v_win, preferred_element_type=F32), row) + pw_new * vw_new) / l_w

    g_own = _own_group(gate_ref[0], row)
    lane = lax.broadcasted_iota(jnp.int32, (N_HEADS, LANES), 1)
    first = 3 * (row % Q_PER_KV)
    pick = lambda j: jnp.sum(jnp.where(lane == first + j, g_own, 0.0), axis=-1, keepdims=True)
    o = pick(0) * o_c + pick(1) * o_s + pick(2) * o_w
    ms = jnp.sum(jnp.sum(o * o, axis=-1, keepdims=True), axis=0, keepdims=True) / D_ATTN
    o_ref[0] = (o * lax.rsqrt(ms + EPS) * gain_ref[...]).astype(o_ref.dtype)


def _attn_sample(q_b, k_cmp, v_cmp, sel_new, cache_sel, state_win, win_new, gates, gain, layer, page_table):
    depth, n_phys, page = cache_sel.shape[:3]
    n_batch, n_pages = page_table.shape
    win_buf = state_win.shape[2]
    past_len = n_pages * page
    nc_past = past_len // CMP_BLOCK
    pages = cache_sel.reshape(depth * n_phys * page * UNITS, HEAD_DIM)
    wins = state_win.reshape(depth * n_batch * win_buf * UNITS, HEAD_DIM)
    base = layer * n_phys
    per_b = lambda shape: pl.BlockSpec((1,) + shape, lambda b, pt: (b, 0, 0))
    page_specs = [pl.BlockSpec((page * UNITS, HEAD_DIM), lambda b, pt, p=p: (base + pt[b, p], 0))
                  for p in range(n_pages)]
    in_specs = [per_b((N_HEADS, HEAD_DIM)), per_b((nc_past, KV_WIDTH)), per_b((nc_past, KV_WIDTH)),
                per_b((1, ROW_WIDTH)),
                pl.BlockSpec((win_buf * UNITS, HEAD_DIM), lambda b, pt: (layer * n_batch + b, 0)),
                per_b((1, ROW_WIDTH)), per_b((1, GATE_WIDTH)),
                pl.BlockSpec((N_HEADS, HEAD_DIM), lambda b, pt: (0, 0))] + page_specs
    out = pl.pallas_call(
        functools.partial(_attn_sample_kernel, n_pages=n_pages, page=page, win_buf=win_buf, past_len=past_len),
        grid_spec=pltpu.PrefetchScalarGridSpec(
            num_scalar_prefetch=1,
            grid=(n_batch,),
            in_specs=in_specs,
            out_specs=per_b((N_HEADS, HEAD_DIM)),
        ),
        out_shape=jax.ShapeDtypeStruct((n_batch, N_HEADS, HEAD_DIM), BF16),
        compiler_params=_params(("arbitrary",)),
        name="attn_sample",
    )(page_table, q_b.reshape(n_batch, N_HEADS, HEAD_DIM), k_cmp, v_cmp,
      sel_new.reshape(n_batch, 1, ROW_WIDTH), wins, win_new.reshape(n_batch, 1, ROW_WIDTH),
      gates.reshape(n_batch, 1, GATE_WIDTH), gain.reshape(N_HEADS, HEAD_DIM), *([pages] * n_pages))
    return out.reshape(n_batch, D_ATTN)


def _lru_gates(xc, wa_ref, wx_ref, ba, bx, lam):
    width = xc.shape[1]
    bd = width // LRU_BLOCKS
    xb = xc.astype(BF16)
    ra = jnp.concatenate([jnp.dot(xb[:, n * bd:(n + 1) * bd], wa_ref[n], preferred_element_type=F32)
                          for n in range(LRU_BLOCKS)], axis=1)
    rx = jnp.concatenate([jnp.dot(xb[:, n * bd:(n + 1) * bd], wx_ref[n], preferred_element_type=F32)
                          for n in range(LRU_BLOCKS)], axis=1)
    r = jax.nn.sigmoid(ra + ba)
    i = jax.nn.sigmoid(rx + bx)
    log_a = -LRU_C * r * _softplus(-lam)
    a = jnp.exp(log_a)
    b = jnp.sqrt(-jnp.tanh(log_a) * (a * a + 1.0)) * (i * xc)
    return a, b


def _scan_rows(a8, b8, h):
    row = lax.broadcasted_iota(jnp.int32, a8.shape, 0)
    s = 1
    while s < SUBLANES:
        a_prev = jnp.where(row >= s, pltpu.roll(a8, s, 0), 1.0)
        b_prev = jnp.where(row >= s, pltpu.roll(b8, s, 0), 0.0)
        b8 = a8 * b_prev + b8
        a8 = a8 * a_prev
        s *= 2
    return a8 * h + b8


def _lru_prompt_kernel(ux_ref, uy_ref, cw_ref, cb_ref, wa_ref, wx_ref, ba_ref, bx_ref, lam_ref, gain_ref,
                       r_ref, h_ref, x_s, h_s, a_s, b_s):
    tt = ux_ref.shape[0]
    i = pl.program_id(1)

    @pl.when(i == 0)
    def _():
        x_s[0:SUBLANES, :] = jnp.zeros((SUBLANES, x_s.shape[1]), F32)
        h_s[...] = jnp.zeros_like(h_s)

    @pl.when(i > 0)
    def _():
        x_s[0:SUBLANES, :] = x_s[tt:tt + SUBLANES, :]

    x_s[SUBLANES:, :] = ux_ref[...]
    xc = cb_ref[...]
    for j in range(CONV_W):
        k = CONV_W - 1 - j
        xc = xc + x_s[SUBLANES - k:SUBLANES - k + tt, :] * cw_ref[j:j + 1, :]
    a, b = _lru_gates(xc, wa_ref, wx_ref, ba_ref[...], bx_ref[...], lam_ref[...])
    a_s[...] = a
    b_s[...] = b

    def group(k, h):
        r0 = pl.multiple_of(k * SUBLANES, SUBLANES)
        hs = _scan_rows(a_s[pl.ds(r0, SUBLANES), :], b_s[pl.ds(r0, SUBLANES), :], h)
        a_s[pl.ds(r0, SUBLANES), :] = hs
        return hs[SUBLANES - 1:SUBLANES, :]

    h_last = lax.fori_loop(0, tt // SUBLANES, group, h_s[...])
    h_s[...] = h_last
    h_ref[0] = h_last
    y = a_s[...] * jax.nn.gelu(uy_ref[...])
    r_ref[...] = _rms(y, gain_ref[...]).astype(r_ref.dtype)


def _lru_prompt(proj, n_batch, seq_len, weights):
    width = D_ATTN
    tt = _row_tile(seq_len, 512, SUBLANES)
    n_t = seq_len // tt
    rows = lambda col: pl.BlockSpec((tt, width), lambda b, i: (b * n_t + i, col))
    vec = _full((1, width))
    blk = _full((LRU_BLOCKS, width // LRU_BLOCKS, width // LRU_BLOCKS))
    return pl.pallas_call(
        _lru_prompt_kernel,
        grid=(n_batch, n_t),
        in_specs=[rows(COL_UX // width), rows(COL_UY // width), _full((CONV_W, width)), vec, blk, blk,
                  vec, vec, vec, vec],
        out_specs=[pl.BlockSpec((tt, width), lambda b, i: (b * n_t + i, 0)),
                   pl.BlockSpec((1, 1, width), lambda b, i: (b, 0, 0))],
        out_shape=[jax.ShapeDtypeStruct((n_batch * seq_len, width), BF16),
                   jax.ShapeDtypeStruct((n_batch, 1, width), F32)],
        scratch_shapes=[pltpu.VMEM((tt + SUBLANES, width), F32), pltpu.VMEM((1, width), F32),
                        pltpu.VMEM((tt, width), F32), pltpu.VMEM((tt, width), F32)],
        compiler_params=_params(("parallel", "arbitrary")),
        name="lru_prompt",
    )(proj, proj, *weights)


def _lru_sample_kernel(ux_ref, uy_ref, c0_ref, c1_ref, c2_ref, h0_ref, cw_ref, cb_ref, wa_ref, wx_ref,
                       ba_ref, bx_ref, lam_ref, gain_ref, r_ref, h_ref):
    xc = (cb_ref[...] + c0_ref[...] * cw_ref[0:1, :] + c1_ref[...] * cw_ref[1:2, :]
          + c2_ref[...] * cw_ref[2:3, :] + ux_ref[...] * cw_ref[3:4, :])
    a, b = _lru_gates(xc, wa_ref, wx_ref, ba_ref[...], bx_ref[...], lam_ref[...])
    h = a * h0_ref[...] + b
    h_ref[...] = h
    r_ref[...] = _rms(h * jax.nn.gelu(uy_ref[...]), gain_ref[...]).astype(r_ref.dtype)


def _lru_sample(ux, uy, conv_state, h0, weights):
    n, width = ux.shape
    mat = _full((n, width))
    vec = _full((1, width))
    blk = _full((LRU_BLOCKS, width // LRU_BLOCKS, width // LRU_BLOCKS))
    return pl.pallas_call(
        _lru_sample_kernel,
        grid=(1,),
        in_specs=[mat] * 6 + [_full((CONV_W, width)), vec, blk, blk, vec, vec, vec, vec],
        out_specs=[mat, mat],
        out_shape=[jax.ShapeDtypeStruct((n, width), BF16), jax.ShapeDtypeStruct((n, width), F32)],
        compiler_params=_params(("arbitrary",)),
        name="lru_sample",
    )(ux, uy, conv_state[:, 0], conv_state[:, 1], conv_state[:, 2], h0, *weights)


def _swiglu_step(x, wg_ref, wu_ref, wd_ref):
    g = jnp.dot(x, wg_ref[...].astype(BF16), preferred_element_type=F32)
    u = jnp.dot(x, wu_ref[...].astype(BF16), preferred_element_type=F32)
    hid = (jax.nn.silu(g) * u).astype(BF16)
    return jnp.dot(hid, wd_ref[...].astype(BF16), preferred_element_type=F32)


def _ffn_kernel(x_ref, wg_ref, wu_ref, wd_ref, res_ref, o_ref):
    @pl.when(pl.program_id(1) == 0)
    def _():
        o_ref[...] = res_ref[...]

    o_ref[...] += _swiglu_step(x_ref[...], wg_ref, wu_ref, wd_ref)


def _ffn(hn, wg, wu, wd, res):
    n, d = hn.shape
    d_ff = wg.shape[1]
    tm = _row_tile(n, 832)
    tf = _row_tile(d_ff, 256, LANES)
    row = pl.BlockSpec((tm, d), lambda i, f: (i, 0))
    return pl.pallas_call(
        _ffn_kernel,
        grid=(n // tm, d_ff // tf),
        in_specs=[row, pl.BlockSpec((d, tf), lambda i, f: (0, f)), pl.BlockSpec((d, tf), lambda i, f: (0, f)),
                  pl.BlockSpec((tf, d), lambda i, f: (f, 0)), row],
        out_specs=row,
        out_shape=jax.ShapeDtypeStruct((n, d), F32),
        compiler_params=_params(("parallel", "arbitrary")),
        name="ffn",
    )(hn, wg, wu, wd, res)


MOE_BLOCK = 1024
MOE_TILE = 256
COMBINE_TILE = 320


def _router_kernel(x_ref, g_ref, w_ref, o_ref, *, n_experts):
    hn = _rms(x_ref[...], g_ref[...]).astype(BF16)
    logits = jnp.dot(hn, w_ref[...], preferred_element_type=F32)
    lane = lax.broadcasted_iota(jnp.int32, logits.shape, 1).astype(F32)
    logits = jnp.where(lane < n_experts, logits, -jnp.inf)
    m1 = jnp.max(logits, axis=-1, keepdims=True)
    i1 = jnp.min(jnp.where(logits == m1, lane, float(LANES)), axis=-1, keepdims=True)
    rest = jnp.where(lane == i1, -jnp.inf, logits)
    m2 = jnp.max(rest, axis=-1, keepdims=True)
    i2 = jnp.min(jnp.where(rest == m2, lane, float(LANES)), axis=-1, keepdims=True)
    e2 = jnp.exp(m2 - m1)
    den = 1.0 + e2
    o_ref[...] = (jnp.where(lane == 0.0, 1.0 / den, 0.0) + jnp.where(lane == 1.0, e2 / den, 0.0)
                  + jnp.where(lane == 2.0, i1, 0.0) + jnp.where(lane == 3.0, i2, 0.0))


def _router(x, gain, w_router):
    n, d = x.shape
    n_experts = w_router.shape[1]
    tm = _row_tile(n, 832)
    w = jnp.concatenate([w_router, jnp.zeros((d, LANES - n_experts), w_router.dtype)], axis=1).astype(BF16)
    return pl.pallas_call(
        functools.partial(_router_kernel, n_experts=n_experts),
        grid=(n // tm,),
        in_specs=[pl.BlockSpec((tm, d), lambda i: (i, 0)), _full((1, d)), _full((d, LANES))],
        out_specs=pl.BlockSpec((tm, LANES), lambda i: (i, 0)),
        out_shape=jax.ShapeDtypeStruct((n, LANES), F32),
        compiler_params=_params(("parallel",)),
        name="router",
    )(x, gain.reshape(1, d), w)


def _start_row_gather(ids_ref, first, n_rows, src_hbm, dst, sem):
    def body(r, carry):
        pltpu.make_async_copy(src_hbm.at[pl.ds(ids_ref[first + r], 1)], dst.at[pl.ds(r, 1)], sem).start()
        return carry

    lax.fori_loop(0, n_rows, body, 0, unroll=8)


def _dispatch_kernel(ids_ref, on_ref, x_hbm, g_ref, o_ref, buf, sem):
    i = pl.program_id(0)
    slot = i % 2
    start = lambda tile, s: _start_row_gather(ids_ref, tile * MOE_TILE, MOE_TILE, x_hbm, buf.at[s], sem.at[s])

    @pl.when((i == 0) & (on_ref[0] > 0))
    def _():
        start(0, 0)

    @pl.when(i + 1 < pl.num_programs(0))
    def _():
        @pl.when(on_ref[i + 1] > 0)
        def _():
            start(i + 1, 1 - slot)

    @pl.when(on_ref[i] > 0)
    def _():
        pltpu.make_async_copy(buf.at[slot], buf.at[slot], sem.at[slot]).wait()
        o_ref[...] = _rms(buf[slot], g_ref[...]).astype(o_ref.dtype)

    @pl.when(on_ref[i] == 0)
    def _():
        o_ref[...] = jnp.zeros_like(o_ref)


def _dispatch(x, gain, row_ids, tile_on):
    n, d = x.shape
    n_slots = row_ids.shape[0]
    return pl.pallas_call(
        _dispatch_kernel,
        grid_spec=pltpu.PrefetchScalarGridSpec(
            num_scalar_prefetch=2,
            grid=(n_slots // MOE_TILE,),
            in_specs=[pl.BlockSpec(memory_space=pl.ANY), pl.BlockSpec((1, d), lambda i, ids, on: (0, 0))],
            out_specs=pl.BlockSpec((MOE_TILE, d), lambda i, ids, on: (i, 0)),
            scratch_shapes=[pltpu.VMEM((2, MOE_TILE, d), F32), pltpu.SemaphoreType.DMA((2,))],
        ),
        out_shape=jax.ShapeDtypeStruct((n_slots, d), BF16),
        compiler_params=_params(("arbitrary",)),
        name="moe_dispatch",
    )(row_ids, tile_on, x, gain.reshape(1, d))


def _experts_kernel(be_ref, br_ref, x_ref, w_ref, wg_ref, wu_ref, wd_ref, o_ref):
    b, f = pl.program_id(0), pl.program_id(1)
    rows = br_ref[b]

    @pl.when(f == 0)
    def _():
        o_ref[...] = jnp.zeros_like(o_ref)

    @pl.when(rows == MOE_BLOCK)
    def _():
        o_ref[...] += _swiglu_step(x_ref[...], wg_ref.at[0], wu_ref.at[0], wd_ref.at[0])

    @pl.when((rows > 0) & (rows < MOE_BLOCK))
    def _():
        for s in range(MOE_BLOCK // MOE_TILE):
            @pl.when(s * MOE_TILE < rows)
            def _():
                sub = pl.ds(s * MOE_TILE, MOE_TILE)
                o_ref[sub, :] += _swiglu_step(x_ref[sub, :], wg_ref.at[0], wu_ref.at[0], wd_ref.at[0])

    @pl.when((f == pl.num_programs(1) - 1) & (rows > 0))
    def _():
        o_ref[...] = o_ref[...] * w_ref[...]


def _experts(xs, slot_w, blk_expert, blk_rows, wg, wu, wd):
    n_slots, d = xs.shape
    d_ff = wg.shape[2]
    tf = _row_tile(d_ff, 256, LANES)
    n_f = d_ff // tf
    fcol = lambda b, f, br: jnp.where(br[b] > 0, f, n_f - 1)
    return pl.pallas_call(
        _experts_kernel,
        grid_spec=pltpu.PrefetchScalarGridSpec(
            num_scalar_prefetch=2,
            grid=(n_slots // MOE_BLOCK, n_f),
            in_specs=[pl.BlockSpec((MOE_BLOCK, d), lambda b, f, be, br: (b, 0)),
                      pl.BlockSpec((MOE_BLOCK, 1), lambda b, f, be, br: (b, 0)),
                      pl.BlockSpec((1, d, tf), lambda b, f, be, br: (be[b], 0, fcol(b, f, br))),
                      pl.BlockSpec((1, d, tf), lambda b, f, be, br: (be[b], 0, fcol(b, f, br))),
                      pl.BlockSpec((1, tf, d), lambda b, f, be, br: (be[b], fcol(b, f, br), 0))],
            out_specs=pl.BlockSpec((MOE_BLOCK, d), lambda b, f, be, br: (b, 0)),
        ),
        out_shape=jax.ShapeDtypeStruct((n_slots, d), F32),
        compiler_params=_params(("arbitrary", "arbitrary")),
        name="moe_experts",
    )(blk_expert, blk_rows, xs, slot_w, wg, wu, wd)


def _combine_kernel(s0_ref, s1_ref, y_hbm, x_ref, o_ref, buf, sem):
    tc = x_ref.shape[0]
    i = pl.program_id(0)
    slot = i % 2

    def start(tile, s):
        _start_row_gather(s0_ref, tile * tc, tc, y_hbm, buf.at[s, 0], sem.at[s])
        _start_row_gather(s1_ref, tile * tc, tc, y_hbm, buf.at[s, 1], sem.at[s])

    @pl.when(i == 0)
    def _():
        start(0, 0)

    @pl.when(i + 1 < pl.num_programs(0))
    def _():
        start(i + 1, 1 - slot)

    pltpu.make_async_copy(buf.at[slot], buf.at[slot], sem.at[slot]).wait()
    o_ref[...] = x_ref[...] + buf[slot, 0] + buf[slot, 1]


def _combine(x, y, slot0, slot1):
    n, d = x.shape
    tc = _row_tile(n, COMBINE_TILE, SUBLANES)
    row = pl.BlockSpec((tc, d), lambda i, s0, s1: (i, 0))
    return pl.pallas_call(
        _combine_kernel,
        grid_spec=pltpu.PrefetchScalarGridSpec(
            num_scalar_prefetch=2,
            grid=(n // tc,),
            in_specs=[pl.BlockSpec(memory_space=pl.ANY), row],
            out_specs=row,
            scratch_shapes=[pltpu.VMEM((2, 2, tc, d), F32), pltpu.SemaphoreType.DMA((2,))],
        ),
        out_shape=jax.ShapeDtypeStruct((n, d), F32),
        compiler_params=_params(("arbitrary",)),
        name="moe_combine",
    )(slot0, slot1, y, x)


def _routing_tables(route, n_experts):
    n = route.shape[0]
    expert = route[:, 2:4].astype(jnp.int32).T.reshape(-1)
    weight = route[:, 0:2].T.reshape(-1)
    token = jnp.tile(jnp.arange(n, dtype=jnp.int32), 2)
    onehot = (expert[:, None] == jnp.arange(n_experts, dtype=jnp.int32)).astype(jnp.int32)
    rank = jnp.sum((jnp.cumsum(onehot, axis=0) - 1) * onehot, axis=1)
    count = jnp.sum(onehot, axis=0)
    padded = (count + MOE_BLOCK - 1) // MOE_BLOCK * MOE_BLOCK
    ends = jnp.cumsum(padded)
    starts = ends - padded
    slot = starts[expert] + rank
    n_slots = (2 * n + n_experts * (MOE_BLOCK - 1) + MOE_BLOCK - 1) // MOE_BLOCK * MOE_BLOCK
    row_ids = jnp.zeros((n_slots,), jnp.int32).at[slot].set(token)
    slot_w = jnp.zeros((n_slots,), F32).at[slot].set(weight).reshape(n_slots, 1)
    blk_first = jnp.arange(n_slots // MOE_BLOCK, dtype=jnp.int32) * MOE_BLOCK
    blk_expert = jnp.minimum(jnp.searchsorted(ends, blk_first, side="right").astype(jnp.int32), n_experts - 1)
    used = (count + MOE_TILE - 1) // MOE_TILE * MOE_TILE
    blk_rows = jnp.clip(starts[blk_expert] + used[blk_expert] - blk_first, 0, MOE_BLOCK)
    blk_rows = jnp.where(blk_first < ends[-1], blk_rows, 0).astype(jnp.int32)
    tile_first = jnp.arange(n_slots // MOE_TILE, dtype=jnp.int32) * MOE_TILE
    tile_on = (tile_first % MOE_BLOCK < jnp.repeat(blk_rows, MOE_BLOCK // MOE_TILE)).astype(jnp.int32)
    return row_ids, slot_w, blk_expert, blk_rows, tile_on, slot[:n], slot[n:]


def _moe(x, gain, w_router, wg, wu, wd):
    route = _router(x, gain, w_router)
    row_ids, slot_w, blk_expert, blk_rows, tile_on, slot0, slot1 = _routing_tables(route, w_router.shape[1])
    xs = _dispatch(x, gain, row_ids, tile_on)
    y = _experts(xs, slot_w, blk_expert, blk_rows, wg, wu, wd)
    return _combine(x, y, slot0, slot1)


def kernel(x_prompt, x_sample, cache_cmp, cache_sel, state_win, state_lru, state_conv, page_table, ln_mix, w_in, q_norm, k_norm, cmp_pe, cmp_w1, cmp_w2, conv_w, conv_b, lru_wa, lru_ba, lru_wx, lru_bx, lru_lambda, out_norm_attn, out_norm_lru, w_out, ln_ffn, ffn_w_gate, ffn_w_up, ffn_w_down, moe_router, moe_w_gate, moe_w_up, moe_w_down):
    n_batch, seq_len, d_model = x_prompt.shape
    dec_batch, dec_seq, _ = x_sample.shape
    depth = ln_mix.shape[0]
    page = cache_cmp.shape[2]
    n_pages = page_table.shape[1]
    past_len = n_pages * page
    win_buf = state_win.shape[2]
    lru_width = d_model - D_ATTN
    assert dec_seq == 1 and lru_width == D_ATTN
    assert seq_len % Q_BLOCK == 0 and seq_len >= WIN_SPAN and seq_len % SEL_CHUNK == 0 and seq_len >= win_buf
    assert past_len % Q_BLOCK == 0 and win_buf <= past_len and page % CMP_BLOCK == 0
    n_p = n_batch * seq_len
    nc = seq_len // CMP_BLOCK
    nc_past = past_len // CMP_BLOCK
    kv_row = (2, N_KV, HEAD_DIM)

    pos = jnp.concatenate([jnp.tile(jnp.arange(seq_len), n_batch), jnp.full((dec_batch,), past_len)])
    tables = _rope_tables(pos)
    end_p = _rope_tables((jnp.arange(nc) + 1) * CMP_BLOCK - 1)
    end_s = _rope_tables((jnp.arange(nc_past) + 1) * CMP_BLOCK - 1)

    x = jnp.concatenate([x_prompt.reshape(n_p, d_model), x_sample.reshape(dec_batch, d_model)], axis=0)
    outs = [[] for _ in range(10)]
    for l in range(depth):
        hn = _rmsnorm(x, ln_mix[l])
        proj = _matmul([hn], [_reorder_w_in(w_in[l])], tm_target=832, tn_target=1280)
        q_b, sel_f, sel_b, win_f, win_b, gates = _postproj(proj, tables, q_norm[l], k_norm[l, 1], k_norm[l, 2])
        c_rows = proj[:, COL_CMP:COL_CMP + ROW_WIDTH]

        cmp_w = (cmp_pe[l], cmp_w1[l].reshape(2, CMP_BLOCK, HEAD_DIM, HEAD_DIM).astype(BF16),
                 cmp_w2[l].astype(BF16), k_norm[l, 0].reshape(1, HEAD_DIM))
        lru_w = (conv_w[l], conv_b[l].reshape(1, -1), lru_wa[l].astype(BF16), lru_wx[l].astype(BF16),
                 lru_ba[l].reshape(1, -1), lru_bx[l].reshape(1, -1), lru_lambda[l].reshape(1, -1),
                 out_norm_lru[l].reshape(1, -1))

        kc_p, vc_p = _compress_prompt(c_rows[:n_p].reshape(n_batch * nc, CMP_ROW), n_batch, nc, cmp_w, end_p)
        o_p = _attn_prompt(q_b, _deinterleave_blocks(kc_p, n_batch, nc), _deinterleave_blocks(vc_p, n_batch, nc),
                           sel_b, win_b, gates, out_norm_attn[l], n_batch, seq_len)
        r_p, h_p = _lru_prompt(proj, n_batch, seq_len, lru_w)

        kc_s, vc_s = _compress_sample(cache_cmp, l, page_table, cmp_w, end_s, nb=2 if dec_batch % 2 == 0 else 1)
        o_s = _attn_sample(q_b[n_p:], _deinterleave_blocks(kc_s, dec_batch, nc_past),
                           _deinterleave_blocks(vc_s, dec_batch, nc_past), sel_f[n_p:], cache_sel, state_win,
                           win_f[n_p:], gates[n_p:], out_norm_attn[l], l, page_table)
        ux_s = proj[n_p:, COL_UX:COL_UX + lru_width]
        r_s, h_s = _lru_sample(ux_s, proj[n_p:, COL_UY:COL_UY + lru_width], state_conv[l], state_lru[l], lru_w)

        w_o = w_out[l].astype(BF16)
        x = _matmul([jnp.concatenate([o_p, o_s], axis=0), jnp.concatenate([r_p, r_s], axis=0)],
                    [w_o[:D_ATTN], w_o[D_ATTN:]], res=x, tm_target=832, tn_target=1024)

        k = l // 2
        if l % 2 == 0:
            x = _ffn(_rmsnorm(x, ln_ffn[l]), ffn_w_gate[k], ffn_w_up[k], ffn_w_down[k], x)
        else:
            x = _moe(x, ln_ffn[l], moe_router[k], moe_w_gate[k], moe_w_up[k], moe_w_down[k])

        ux_p = proj[:n_p, COL_UX:COL_UX + lru_width].reshape(n_batch, seq_len, lru_width)
        new = (c_rows[:n_p].reshape((n_batch, seq_len) + kv_row), c_rows[n_p:].reshape((dec_batch, 1) + kv_row),
               sel_f[:n_p].reshape((n_batch, seq_len) + kv_row), sel_f[n_p:].reshape((dec_batch, 1) + kv_row),
               win_f[:n_p].reshape((n_batch, seq_len) + kv_row)[:, seq_len - win_buf:],
               jnp.concatenate([state_win[l][:, 1:], win_f[n_p:].reshape((dec_batch, 1) + kv_row)], axis=1),
               h_p.reshape(n_batch, lru_width), h_s,
               ux_p[:, seq_len - (CONV_W - 1):],
               jnp.concatenate([state_conv[l][:, 1:], ux_s[:, None, :]], axis=1))
        for acc, val in zip(outs, new):
            acc.append(val)

    return (x[:n_p].reshape(n_batch, seq_len, d_model), x[n_p:].reshape(dec_batch, 1, d_model),
            *[jnp.stack(v) for v in outs])
```

```python
import functools

import jax
import jax.numpy as jnp
from jax import lax
from jax.experimental import pallas as pl
from jax.experimental.pallas import tpu as pltpu

N_HEADS = 8
HEAD_DIM = 128
N_KV = 2
Q_PER_KV = N_HEADS // N_KV
D_ATTN = N_HEADS * HEAD_DIM
KV_WIDTH = N_KV * HEAD_DIM
ROW_WIDTH = 2 * KV_WIDTH
UNITS = 2 * N_KV
ROT_DIM = HEAD_DIM // 4
ROPE_THETA = 500000.0
CMP_BLOCK = 32
SEL_BLOCK = 64
TOP_N = 16
WINDOW = 512
Q_BLOCK = 128
LRU_BLOCKS = 8
CONV_W = 4
LRU_C = 8.0
TOP_K = 2
EPS = 1e-6
SCALE = HEAD_DIM ** -0.5
NEG_INF = -1e30

LANES = 128
SUBLANES = 8
VMEM_LIMIT = 56 * 1024 * 1024

F32 = jnp.float32
BF16 = jnp.bfloat16


def _row_tile(n, target, mult=16):
    best = None
    for t in range(mult, min(n, target) + 1, mult):
        if n % t == 0:
            best = t
    return best if best is not None else n


def _params(semantics):
    return pltpu.CompilerParams(dimension_semantics=semantics, vmem_limit_bytes=VMEM_LIMIT)


def _full(shape):
    zeros = (0,) * len(shape)
    return pl.BlockSpec(shape, lambda *_: zeros)


def _rms(x, gain):
    return x * lax.rsqrt(jnp.mean(x * x, axis=-1, keepdims=True) + EPS) * gain


def _rope_apply(y, cos_t, sin_lo, sin_hi):
    half = ROT_DIM // 2
    return y * cos_t + pltpu.roll(y, HEAD_DIM - half, 1) * sin_lo + pltpu.roll(y, half, 1) * sin_hi


def _softplus(x):
    return jnp.maximum(x, 0.0) + jnp.log1p(jnp.exp(-jnp.abs(x)))


def _topk_mask(score, n_pick):
    n_lanes = score.shape[-1]
    lane = lax.broadcasted_iota(jnp.int32, score.shape, 1).astype(F32)
    sel = jnp.zeros(score.shape, F32)
    for _ in range(n_pick):
        m = jnp.max(score, axis=-1, keepdims=True)
        idx = jnp.min(jnp.where(score == m, lane, float(n_lanes)), axis=-1, keepdims=True)
        pick = lane == idx
        sel = jnp.where(pick, 1.0, sel)
        score = jnp.where(pick, -2.0, score)
    return sel


def _masked_softmax_rows(s, mask):
    sm = jnp.where(mask, s, NEG_INF)
    m = jnp.max(sm, axis=-1, keepdims=True)
    e = jnp.where(mask, jnp.exp(sm - m), 0.0)
    l = jnp.sum(e, axis=-1, keepdims=True)
    return e / jnp.where(l > 0.0, l, 1.0)


def _dot_nt(a, b):
    return lax.dot_general(a, b, (((1,), (1,)), ((), ())), preferred_element_type=F32)


def _rmsnorm_kernel(x_ref, g_ref, o_ref):
    o_ref[...] = _rms(x_ref[...], g_ref[...]).astype(o_ref.dtype)


def _rmsnorm(x, gain):
    n, d = x.shape
    tm = _row_tile(n, 832)
    return pl.pallas_call(
        _rmsnorm_kernel,
        grid=(n // tm,),
        in_specs=[pl.BlockSpec((tm, d), lambda i: (i, 0)), _full((1, d))],
        out_specs=pl.BlockSpec((tm, d), lambda i: (i, 0)),
        out_shape=jax.ShapeDtypeStruct((n, d), BF16),
        compiler_params=_params(("parallel",)),
        name="rmsnorm",
    )(x, gain.reshape(1, d))


def _matmul_kernel(*refs, n_lhs, has_res):
    o_ref = refs[-1]
    acc = jnp.dot(refs[0][...], refs[n_lhs][...], preferred_element_type=F32)
    for k in range(1, n_lhs):
        acc = acc + jnp.dot(refs[k][...], refs[n_lhs + k][...], preferred_element_type=F32)
    if has_res:
        acc = acc + refs[2 * n_lhs][...]
    o_ref[...] = acc


def _matmul(xs, ws, res=None, *, tm_target, tn_target):
    n = xs[0].shape[0]
    d_out = ws[0].shape[1]
    tm = _row_tile(n, tm_target)
    tn = _row_tile(d_out, tn_target, LANES)
    in_specs = [pl.BlockSpec((tm, x.shape[1]), lambda i, j: (i, 0)) for x in xs]
    in_specs += [pl.BlockSpec((w.shape[0], tn), lambda i, j: (0, j)) for w in ws]
    args = list(xs) + list(ws)
    if res is not None:
        in_specs.append(pl.BlockSpec((tm, tn), lambda i, j: (i, j)))
        args.append(res)
    return pl.pallas_call(
        functools.partial(_matmul_kernel, n_lhs=len(xs), has_res=res is not None),
        grid=(n // tm, d_out // tn),
        in_specs=in_specs,
        out_specs=pl.BlockSpec((tm, tn), lambda i, j: (i, j)),
        out_shape=jax.ShapeDtypeStruct((n, d_out), F32),
        compiler_params=_params(("parallel", "parallel")),
        name="matmul",
    )(*args)


PROJ_WIDTH = 5120
COL_UX, COL_UY, COL_CMP, COL_SEL, COL_WIN, COL_GATE = 1024, 2048, 3072, 3584, 4096, 4608
GATE_WIDTH = 2 * LANES


def _reorder_w_in(w):
    lru = D_ATTN
    c_kv = D_ATTN
    c_g = c_kv + 3 * ROW_WIDTH
    c_ux = c_g + 3 * N_HEADS
    c_uy = c_ux + lru
    n_g = 3 * Q_PER_KV
    zpad = lambda k: jnp.zeros((w.shape[0], k), w.dtype)
    parts = [w[:, :D_ATTN], w[:, c_ux:c_uy], w[:, c_uy:c_uy + lru], w[:, c_kv:c_g],
             w[:, c_g:c_g + n_g], zpad(LANES - n_g), w[:, c_g + n_g:c_ux], zpad(LANES - n_g)]
    width = sum(p.shape[1] for p in parts)
    parts.append(zpad(PROJ_WIDTH - width))
    return jnp.concatenate(parts, axis=1).astype(BF16)


def _postproj_kernel(q_ref, sel_ref, win_ref, gate_ref, cos_ref, slo_ref, shi_ref,
                     qg_ref, ksg_ref, kwg_ref,
                     qo_ref, sel_f_ref, sel_b_ref, win_f_ref, win_b_ref, gate_o_ref):
    cos_t, slo, shi = cos_ref[...], slo_ref[...], shi_ref[...]
    for h in range(N_HEADS):
        cols = slice(h * HEAD_DIM, (h + 1) * HEAD_DIM)
        y = _rope_apply(_rms(q_ref[:, cols], qg_ref[...]), cos_t, slo, shi)
        qo_ref[:, cols] = y.astype(qo_ref.dtype)
    for src, gain_ref, out_f, out_b in ((sel_ref, ksg_ref, sel_f_ref, sel_b_ref),
                                        (win_ref, kwg_ref, win_f_ref, win_b_ref)):
        for g in range(N_KV):
            cols = slice(g * HEAD_DIM, (g + 1) * HEAD_DIM)
            y = _rope_apply(_rms(src[:, cols], gain_ref[...]), cos_t, slo, shi)
            out_f[:, cols] = y
            out_b[:, cols] = y.astype(out_b.dtype)
        v = src[:, KV_WIDTH:]
        out_f[:, KV_WIDTH:] = v
        out_b[:, KV_WIDTH:] = v.astype(out_b.dtype)
    gate_o_ref[...] = jax.nn.sigmoid(gate_ref[...])


def _postproj(proj, tables, q_gain, ks_gain, kw_gain):
    n = proj.shape[0]
    tm = _row_tile(n, 416)
    rows = lambda width, col: pl.BlockSpec((tm, width), lambda i: (i, col))
    tab = pl.BlockSpec((tm, HEAD_DIM), lambda i: (i, 0))
    gain = _full((1, HEAD_DIM))
    out = lambda width: pl.BlockSpec((tm, width), lambda i: (i, 0))
    sds = jax.ShapeDtypeStruct
    return pl.pallas_call(
        _postproj_kernel,
        grid=(n // tm,),
        in_specs=[rows(D_ATTN, 0), rows(ROW_WIDTH, COL_SEL // ROW_WIDTH), rows(ROW_WIDTH, COL_WIN // ROW_WIDTH),
                  rows(GATE_WIDTH, COL_GATE // GATE_WIDTH), tab, tab, tab, gain, gain, gain],
        out_specs=[out(D_ATTN), out(ROW_WIDTH), out(ROW_WIDTH), out(ROW_WIDTH), out(ROW_WIDTH), out(GATE_WIDTH)],
        out_shape=[sds((n, D_ATTN), BF16), sds((n, ROW_WIDTH), F32), sds((n, ROW_WIDTH), BF16),
                   sds((n, ROW_WIDTH), F32), sds((n, ROW_WIDTH), BF16), sds((n, GATE_WIDTH), F32)],
        compiler_params=_params(("parallel",)),
        name="postproj",
    )(proj, proj, proj, proj, *tables, q_gain.reshape(1, -1), ks_gain.reshape(1, -1), kw_gain.reshape(1, -1))


def _rope_tables(pos):
    half = ROT_DIM // 2
    inv_freq = ROPE_THETA ** (-2.0 * jnp.arange(half, dtype=F32) / ROT_DIM)
    ang = pos.astype(F32)[:, None] * inv_freq
    cos, sin = jnp.cos(ang), jnp.sin(ang)
    n = pos.shape[0]
    ones = jnp.ones((n, HEAD_DIM - ROT_DIM), F32)
    zeros = jnp.zeros((n, HEAD_DIM - half), F32)
    cos_t = jnp.concatenate([cos, cos, ones], axis=1)
    sin_lo = jnp.concatenate([-sin, zeros], axis=1)
    sin_hi = jnp.concatenate([jnp.zeros((n, half), F32), sin, jnp.zeros((n, HEAD_DIM - ROT_DIM), F32)], axis=1)
    return cos_t, sin_lo, sin_hi


CMP_ROW = CMP_BLOCK * ROW_WIDTH


def _compress_rows(piece, n_rows, w1_ref, w2_ref, gain_ref, cos_ref, slo_ref, shi_ref, k_ref, v_ref):
    for n in range(2):
        acc = jnp.zeros((N_KV * n_rows, HEAD_DIM), F32)
        for t in range(CMP_BLOCK):
            lhs = jnp.concatenate([piece(n, t, g) for g in range(N_KV)], axis=0)
            acc = acc + jnp.dot(lhs, w1_ref[n, t], preferred_element_type=F32)
        hid = jax.nn.gelu(acc).astype(BF16)
        comp = jnp.dot(hid, w2_ref[n], preferred_element_type=F32)
        for g in range(N_KV):
            part = comp[g * n_rows:(g + 1) * n_rows]
            cols = slice(g * HEAD_DIM, (g + 1) * HEAD_DIM)
            if n == 0:
                part = _rope_apply(_rms(part, gain_ref[...]), cos_ref[...], slo_ref[...], shi_ref[...])
                k_ref[:, cols] = part.astype(k_ref.dtype)
            else:
                v_ref[:, cols] = part.astype(v_ref.dtype)


def _compress_prompt_kernel(x_ref, pe_ref, *rest):
    def piece(n, t, g):
        c0 = t * ROW_WIDTH + n * KV_WIDTH + g * HEAD_DIM
        return (x_ref[:, c0:c0 + HEAD_DIM] + pe_ref[n, t:t + 1, :]).astype(BF16)

    _compress_rows(piece, x_ref.shape[0], *rest)


def _compress_sample_kernel(pt_ref, *refs, n_page_refs, page):
    del pt_ref
    pages = refs[:n_page_refs]
    pe_ref = refs[n_page_refs]
    rest = refs[n_page_refs + 1:-1]
    y_s = refs[-1]
    n_tok = 2 * page
    out_row = lax.broadcasted_iota(jnp.int32, (n_tok, n_tok), 0)
    src_tok = lax.broadcasted_iota(jnp.int32, (n_tok, n_tok), 1)
    regroup = ((out_row % SUBLANES) * CMP_BLOCK + out_row // SUBLANES == src_tok).astype(BF16)
    pe_tok = [jnp.concatenate([pe_ref[n]] * (n_tok // CMP_BLOCK), axis=0) for n in range(2)]
    for pair in range(n_page_refs // 2):
        for s in range(UNITS):
            tok = jnp.concatenate([pages[2 * pair + k][pl.ds(s, page, stride=UNITS), :] for k in range(2)], axis=0)
            y = jnp.dot(regroup, (tok + pe_tok[s // N_KV]).astype(BF16), preferred_element_type=F32)
            y_s[s, :, pair * SUBLANES:(pair + 1) * SUBLANES, :] = y.reshape(CMP_BLOCK, SUBLANES, HEAD_DIM)

    _compress_rows(lambda n, t, g: y_s[n * N_KV + g, t].astype(BF16), y_s.shape[2], *rest)


def _compress_weight_specs():
    return [_full((2, CMP_BLOCK, HEAD_DIM)), _full((2, CMP_BLOCK, HEAD_DIM, HEAD_DIM)),
            _full((2, HEAD_DIM, HEAD_DIM)), _full((1, HEAD_DIM))]


def _compress_prompt(c_rows, n_batch, nc, weights, end_tables):
    tr = _row_tile(nc, 128)
    per_b = nc // tr
    tab = pl.BlockSpec((tr, HEAD_DIM), lambda i: (i % per_b, 0))
    out = pl.BlockSpec((tr, KV_WIDTH), lambda i: (i, 0))
    sds = jax.ShapeDtypeStruct((n_batch * nc, KV_WIDTH), BF16)
    return pl.pallas_call(
        _compress_prompt_kernel,
        grid=(n_batch * per_b,),
        in_specs=[pl.BlockSpec((tr, CMP_ROW), lambda i: (i, 0))] + _compress_weight_specs() + [tab, tab, tab],
        out_specs=[out, out],
        out_shape=[sds, sds],
        compiler_params=_params(("parallel",)),
        name="compress_prompt",
    )(c_rows, *weights, *end_tables)


def _compress_sample(cache, layer, page_table, weights, end_tables, nb):
    depth, n_phys, page = cache.shape[:3]
    n_batch, n_pages = page_table.shape
    assert n_pages % 2 == 0 and 2 * page == SUBLANES * CMP_BLOCK
    nc = n_pages * page // CMP_BLOCK
    pages = cache.reshape(depth * n_phys * page * UNITS, HEAD_DIM)
    base = layer * n_phys
    page_specs = [pl.BlockSpec((page * UNITS, HEAD_DIM), lambda i, pt, bb=bb, p=p: (base + pt[i * nb + bb, p], 0))
                  for bb in range(nb) for p in range(n_pages)]
    n_rows = nb * nc
    tab = pl.BlockSpec((n_rows, HEAD_DIM), lambda i, pt: (0, 0))
    wspecs = [pl.BlockSpec(s.block_shape, lambda i, pt, z=(0,) * len(s.block_shape): z) for s in _compress_weight_specs()]
    out = pl.BlockSpec((n_rows, KV_WIDTH), lambda i, pt: (i, 0))
    sds = jax.ShapeDtypeStruct((n_batch * nc, KV_WIDTH), BF16)
    tables = [jnp.tile(t, (nb, 1)) for t in end_tables]
    return pl.pallas_call(
        functools.partial(_compress_sample_kernel, n_page_refs=nb * n_pages, page=page),
        grid_spec=pltpu.PrefetchScalarGridSpec(
            num_scalar_prefetch=1,
            grid=(n_batch // nb,),
            in_specs=page_specs + wspecs + [tab, tab, tab],
            out_specs=[out, out],
            scratch_shapes=[pltpu.VMEM((UNITS, CMP_BLOCK, n_rows, HEAD_DIM), F32)],
        ),
        out_shape=[sds, sds],
        compiler_params=_params(("arbitrary",)),
        name="compress_sample",
    )(page_table, *([pages] * (nb * n_pages)), *weights, *tables)


def _deinterleave_blocks(x, n_batch, nc):
    x = x.reshape(n_batch, nc, -1)
    return jnp.concatenate([x[:, 0::2], x[:, 1::2]], axis=1)


SEL_CHUNK = 256
WIN_SPAN = WINDOW + Q_BLOCK


def _attn_prompt_kernel(q_ref, kc_ref, vc_ref, sel_ref, win_ref, gate_ref, gain_ref, o_ref, *, seq_len):
    nc = seq_len // CMP_BLOCK
    ns = seq_len // SEL_BLOCK
    i = pl.program_id(1)
    q0 = i * Q_BLOCK
    tpos = q0 + lax.broadcasted_iota(jnp.int32, (Q_BLOCK, 1), 0)
    gates = gate_ref[...]
    slabs = []
    for g in range(N_KV):
        kcol = slice(g * HEAD_DIM, (g + 1) * HEAD_DIM)
        vcol = slice(KV_WIDTH + g * HEAD_DIM, KV_WIDTH + (g + 1) * HEAD_DIM)
        qs = jnp.concatenate([q_ref[:, (g * Q_PER_KV + h) * HEAD_DIM:(g * Q_PER_KV + h + 1) * HEAD_DIM]
                              for h in range(Q_PER_KV)], axis=0)

        col = lax.broadcasted_iota(jnp.int32, (Q_BLOCK, nc), 1)
        blk = jnp.where(col < ns, 2 * col, 2 * (col - ns) + 1)
        c_mask = ((blk + 1) * CMP_BLOCK - 1) <= tpos
        s_c = (_dot_nt(qs, kc_ref[0, :, kcol]) * SCALE).reshape(Q_PER_KV, Q_BLOCK, nc)
        p_c = _masked_softmax_rows(s_c, c_mask[None])
        o_c = jnp.dot(p_c.reshape(Q_PER_KV * Q_BLOCK, nc).astype(BF16), vc_ref[0, :, kcol],
                      preferred_element_type=F32)
        p_tok = p_c[0] + p_c[1] + p_c[2] + p_c[3]
        imp = p_tok[:, :ns] + p_tok[:, ns:]

        sblk = lax.broadcasted_iota(jnp.int32, (Q_BLOCK, ns), 1)
        allowed = sblk * SEL_BLOCK <= tpos
        forced = (sblk == 0) | (sblk == tpos // SEL_BLOCK)
        score = jnp.where(forced, 1e30, jnp.where(allowed, imp, -1.0))
        sel = _topk_mask(score, min(TOP_N, ns)).astype(BF16)

        def sel_step(j, carry):
            m, l, acc = carry
            k0 = pl.multiple_of(j * SEL_CHUNK, SEL_CHUNK)
            kpos = k0 + lax.broadcasted_iota(jnp.int32, (1, SEL_CHUNK), 1)
            expand = (kpos // SEL_BLOCK == lax.broadcasted_iota(jnp.int32, (ns, SEL_CHUNK), 0)).astype(BF16)
            chosen = jnp.dot(sel, expand, preferred_element_type=F32) > 0.5
            mask = (chosen & (kpos <= tpos))[None]
            s = (_dot_nt(qs, sel_ref[pl.ds(k0, SEL_CHUNK), kcol]) * SCALE).reshape(Q_PER_KV, Q_BLOCK, SEL_CHUNK)
            sm = jnp.where(mask, s, NEG_INF)
            m_new = jnp.maximum(m, jnp.max(sm, axis=-1, keepdims=True))
            p = jnp.where(mask, jnp.exp(sm - m_new), 0.0)
            alpha = jnp.exp(m - m_new)
            l = alpha * l + jnp.sum(p, axis=-1, keepdims=True)
            pv = jnp.dot(p.reshape(Q_PER_KV * Q_BLOCK, SEL_CHUNK).astype(BF16),
                         sel_ref[pl.ds(k0, SEL_CHUNK), vcol], preferred_element_type=F32)
            acc = alpha * acc + pv.reshape(Q_PER_KV, Q_BLOCK, HEAD_DIM)
            return m_new, l, acc

        n_steps = (q0 + Q_BLOCK + SEL_CHUNK - 1) // SEL_CHUNK
        init = (jnp.full((Q_PER_KV, Q_BLOCK, 1), NEG_INF, F32), jnp.zeros((Q_PER_KV, Q_BLOCK, 1), F32),
                jnp.zeros((Q_PER_KV, Q_BLOCK, HEAD_DIM), F32))
        _, l_s, acc_s = lax.fori_loop(0, n_steps, sel_step, init)
        o_s = (acc_s / l_s).reshape(Q_PER_KV * Q_BLOCK, HEAD_DIM)

        w0 = pl.multiple_of(jnp.maximum(q0 - WINDOW, 0), Q_BLOCK)
        kpos = w0 + lax.broadcasted_iota(jnp.int32, (1, WIN_SPAN), 1)
        dist = tpos - kpos
        w_mask = ((dist >= 0) & (dist <= WINDOW))[None]
        s_w = (_dot_nt(qs, win_ref[pl.ds(w0, WIN_SPAN), kcol]) * SCALE).reshape(Q_PER_KV, Q_BLOCK, WIN_SPAN)
        p_w = _masked_softmax_rows(s_w, w_mask)
        o_w = jnp.dot(p_w.reshape(Q_PER_KV * Q_BLOCK, WIN_SPAN).astype(BF16), win_ref[pl.ds(w0, WIN_SPAN), vcol],
                      preferred_element_type=F32)

        for h in range(Q_PER_KV):
            rows = slice(h * Q_BLOCK, (h + 1) * Q_BLOCK)
            c = g * LANES + 3 * h
            slabs.append(gates[:, c:c + 1] * o_c[rows] + gates[:, c + 1:c + 2] * o_s[rows]
                         + gates[:, c + 2:c + 3] * o_w[rows])
    o = jnp.concatenate(slabs, axis=1)
    o_ref[...] = _rms(o, gain_ref[...]).astype(o_ref.dtype)


def _attn_prompt(q_b, k_cmp, v_cmp, sel_b, win_b, gates, gain, n_batch, seq_len):
    n_qb = seq_len // Q_BLOCK
    nc = seq_len // CMP_BLOCK
    qrow = lambda width: pl.BlockSpec((Q_BLOCK, width), lambda b, i: (b * n_qb + i, 0))
    cmp_spec = pl.BlockSpec((1, nc, KV_WIDTH), lambda b, i: (b, 0, 0))
    seq_spec = pl.BlockSpec((seq_len, ROW_WIDTH), lambda b, i: (b, 0))
    return pl.pallas_call(
        functools.partial(_attn_prompt_kernel, seq_len=seq_len),
        grid=(n_batch, n_qb),
        in_specs=[qrow(D_ATTN), cmp_spec, cmp_spec, seq_spec, seq_spec, qrow(GATE_WIDTH), _full((1, D_ATTN))],
        out_specs=qrow(D_ATTN),
        out_shape=jax.ShapeDtypeStruct((n_batch * seq_len, D_ATTN), BF16),
        compiler_params=_params(("parallel", "arbitrary")),
        name="attn_prompt",
    )(q_b, k_cmp, v_cmp, sel_b, win_b, gates, gain.reshape(1, -1))


def _unit_rows(refs, n_tokens, first_unit):
    cols = [jnp.concatenate([r[pl.ds(first_unit + g, n_tokens, stride=UNITS), :] for r in refs], axis=0)
            for g in range(N_KV)]
    return jnp.concatenate(cols, axis=1).astype(BF16)


def _own_group(x, row):
    return jnp.where(row < Q_PER_KV, x[:, :HEAD_DIM], x[:, HEAD_DIM:])


def _split_groups(q):
    row = lax.broadcasted_iota(jnp.int32, q.shape[:-1] + (1,), q.ndim - 2)
    zero = jnp.zeros_like(q)
    return jnp.concatenate([jnp.where(row < Q_PER_KV, q, zero), jnp.where(row < Q_PER_KV, zero, q)], axis=-1)


def _select_sample_kernel(q_ref, kc_ref, vc_ref, oc_ref, sel_ref, *, ns_past):
    q2 = _split_groups(q_ref[...])
    bt = q2.shape[0]
    s_c = jnp.einsum("bhk,bck->bhc", q2, kc_ref[...], preferred_element_type=F32) * SCALE
    m_c = jnp.max(s_c, axis=-1, keepdims=True)
    e_c = jnp.exp(s_c - m_c)
    p_c = e_c / jnp.sum(e_c, axis=-1, keepdims=True)
    o2 = jnp.einsum("bhc,bck->bhk", p_c.astype(BF16), vc_ref[...], preferred_element_type=F32)
    row = lax.broadcasted_iota(jnp.int32, (bt, N_HEADS, 1), 1)
    oc_ref[...] = jnp.where(row < Q_PER_KV, o2[..., :HEAD_DIM], o2[..., HEAD_DIM:])

    pair = p_c + jnp.where(row % 2 == 0, pltpu.roll(p_c, N_HEADS - 1, 1), pltpu.roll(p_c, 1, 1))
    p_grp = pair + jnp.where(row % Q_PER_KV < 2, pltpu.roll(pair, N_HEADS - 2, 1), pltpu.roll(pair, 2, 1))
    imp = (p_grp[..., :ns_past] + p_grp[..., ns_past:]).reshape(bt * N_HEADS, ns_past)

    sblk = lax.broadcasted_iota(jnp.int32, imp.shape, 1)
    score = jnp.where(sblk == 0, 1e30, imp)
    sel = _topk_mask(score, min(TOP_N, ns_past + 1) - 1).reshape(bt, N_HEADS, ns_past)
    sel_ref[...] = jnp.concatenate([sel, jnp.zeros((bt, N_HEADS, LANES - ns_past), F32)], axis=-1)


def _select_sample(q3, k_cmp, v_cmp):
    n_batch, nc_past, _ = k_cmp.shape
    ns_past = nc_past // (SEL_BLOCK // CMP_BLOCK)
    assert ns_past <= LANES
    bt = _row_tile(n_batch, 32, 1)
    per_b = lambda rows, width: pl.BlockSpec((bt, rows, width), lambda i: (i, 0, 0))
    return pl.pallas_call(
        functools.partial(_select_sample_kernel, ns_past=ns_past),
        grid=(n_batch // bt,),
        in_specs=[per_b(N_HEADS, HEAD_DIM), per_b(nc_past, KV_WIDTH), per_b(nc_past, KV_WIDTH)],
        out_specs=[per_b(N_HEADS, HEAD_DIM), per_b(N_HEADS, LANES)],
        out_shape=[jax.ShapeDtypeStruct((n_batch, N_HEADS, HEAD_DIM), F32),
                   jax.ShapeDtypeStruct((n_batch, N_HEADS, LANES), F32)],
        compiler_params=_params(("parallel",)),
        name="select_sample",
    )(q3, k_cmp, v_cmp)


def _attn_sample_kernel(pt_ref, *refs, n_pages, page, win_buf, past_len):
    del pt_ref
    q_ref, oc_ref, sel_ref, snew_ref, win_ref, wnew_ref, gate_ref, gain_ref = refs[:8]
    pages = refs[8:8 + n_pages]
    o_ref, nwin_ref = refs[8 + n_pages:]

    q = q_ref[0]
    row = lax.broadcasted_iota(jnp.int32, (N_HEADS, 1), 0)
    q2 = _split_groups(q)
    qf = q.astype(F32)
    o_c = oc_ref[0]

    kpos = lax.broadcasted_iota(jnp.int32, (1, past_len), 1)
    expand = (kpos // SEL_BLOCK == lax.broadcasted_iota(jnp.int32, (LANES, past_len), 0)).astype(BF16)
    chosen = jnp.dot(sel_ref[0].astype(BF16), expand, preferred_element_type=F32) > 0.5

    k_sel = _unit_rows(pages, page, 0)
    v_sel = _unit_rows(pages, page, N_KV)
    s_s = _dot_nt(q2, k_sel) * SCALE
    k_new = _own_group(snew_ref[0, :, :KV_WIDTH], row)
    v_new = _own_group(snew_ref[0, :, KV_WIDTH:], row)
    s_new = jnp.sum(qf * k_new, axis=-1, keepdims=True) * SCALE
    sm = jnp.where(chosen, s_s, NEG_INF)
    m_s = jnp.maximum(jnp.max(sm, axis=-1, keepdims=True), s_new)
    p_s = jnp.where(chosen, jnp.exp(sm - m_s), 0.0)
    p_new = jnp.exp(s_new - m_s)
    l_s = jnp.sum(p_s, axis=-1, keepdims=True) + p_new
    o_s = (_own_group(jnp.dot(p_s.astype(BF16), v_sel, preferred_element_type=F32), row) + p_new * v_new) / l_s

    k_win = _unit_rows([win_ref], win_buf, 0)
    v_win = _unit_rows([win_ref], win_buf, N_KV)
    wpos = past_len - win_buf + lax.broadcasted_iota(jnp.int32, (1, win_buf), 1)
    w_mask = ((past_len - wpos) <= WINDOW) & (wpos >= 0)
    s_w = jnp.where(w_mask, _dot_nt(q2, k_win) * SCALE, NEG_INF)
    kw_new = _own_group(wnew_ref[0, :, :KV_WIDTH], row)
    vw_new = _own_group(wnew_ref[0, :, KV_WIDTH:], row)
    sw_new = jnp.sum(qf * kw_new, axis=-1, keepdims=True) * SCALE
    m_w = jnp.maximum(jnp.max(s_w, axis=-1, keepdims=True), sw_new)
    p_w = jnp.where(w_mask, jnp.exp(s_w - m_w), 0.0)
    pw_new = jnp.exp(sw_new - m_w)
    l_w = jnp.sum(p_w, axis=-1, keepdims=True) + pw_new
    o_w = (_own_group(jnp.dot(p_w.astype(BF16), v_win, preferred_element_type=F32), row) + pw_new * vw_new) / l_w

    g_own = _own_group(gate_ref[0], row)
    lane = lax.broadcasted_iota(jnp.int32, (N_HEADS, LANES), 1)
    first = 3 * (row % Q_PER_KV)
    pick = lambda j: jnp.sum(jnp.where(lane == first + j, g_own, 0.0), axis=-1, keepdims=True)
    o = pick(0) * o_c + pick(1) * o_s + pick(2) * o_w
    ms = jnp.sum(jnp.sum(o * o, axis=-1, keepdims=True), axis=0, keepdims=True) / D_ATTN
    o_ref[0] = (o * lax.rsqrt(ms + EPS) * gain_ref[...]).astype(o_ref.dtype)

    kept = (win_buf - 1) * UNITS
    nwin_ref[0:kept, :] = win_ref[UNITS:win_buf * UNITS, :]
    for u in range(UNITS):
        nwin_ref[kept + u:kept + u + 1, :] = wnew_ref[0, :, u * HEAD_DIM:(u + 1) * HEAD_DIM]


def _attn_sample(q_b, k_cmp, v_cmp, sel_new, cache_sel, state_win, win_new, gates, gain, layer, page_table):
    depth, n_phys, page = cache_sel.shape[:3]
    n_batch, n_pages = page_table.shape
    win_buf = state_win.shape[2]
    past_len = n_pages * page
    q3 = q_b.reshape(n_batch, N_HEADS, HEAD_DIM)
    o_c, sel = _select_sample(q3, k_cmp, v_cmp)
    pages = cache_sel.reshape(depth * n_phys * page * UNITS, HEAD_DIM)
    wins = state_win.reshape(depth * n_batch * win_buf * UNITS, HEAD_DIM)
    base = layer * n_phys
    per_b = lambda shape: pl.BlockSpec((1,) + shape, lambda b, pt: (b, 0, 0))
    page_specs = [pl.BlockSpec((page * UNITS, HEAD_DIM), lambda b, pt, p=p: (base + pt[b, p], 0))
                  for p in range(n_pages)]
    in_specs = [per_b((N_HEADS, HEAD_DIM)), per_b((N_HEADS, HEAD_DIM)), per_b((N_HEADS, LANES)),
                per_b((1, ROW_WIDTH)),
                pl.BlockSpec((win_buf * UNITS, HEAD_DIM), lambda b, pt: (layer * n_batch + b, 0)),
                per_b((1, ROW_WIDTH)), per_b((1, GATE_WIDTH)),
                pl.BlockSpec((N_HEADS, HEAD_DIM), lambda b, pt: (0, 0))] + page_specs
    out, new_win = pl.pallas_call(
        functools.partial(_attn_sample_kernel, n_pages=n_pages, page=page, win_buf=win_buf, past_len=past_len),
        grid_spec=pltpu.PrefetchScalarGridSpec(
            num_scalar_prefetch=1,
            grid=(n_batch,),
            in_specs=in_specs,
            out_specs=[per_b((N_HEADS, HEAD_DIM)), pl.BlockSpec((win_buf * UNITS, HEAD_DIM), lambda b, pt: (b, 0))],
        ),
        out_shape=[jax.ShapeDtypeStruct((n_batch, N_HEADS, HEAD_DIM), BF16),
                   jax.ShapeDtypeStruct((n_batch * win_buf * UNITS, HEAD_DIM), F32)],
        compiler_params=_params(("arbitrary",)),
        name="attn_sample",
    )(page_table, q3, o_c, sel,
      sel_new.reshape(n_batch, 1, ROW_WIDTH), wins, win_new.reshape(n_batch, 1, ROW_WIDTH),
      gates.reshape(n_batch, 1, GATE_WIDTH), gain.reshape(N_HEADS, HEAD_DIM), *([pages] * n_pages))
    return out.reshape(n_batch, D_ATTN), new_win.reshape(n_batch, win_buf, 2, N_KV, HEAD_DIM)


def _lru_gates(xc, wa_ref, wx_ref, ba, bx, lam):
    width = xc.shape[1]
    bd = width // LRU_BLOCKS
    xb = xc.astype(BF16)
    ra = jnp.concatenate([jnp.dot(xb[:, n * bd:(n + 1) * bd], wa_ref[n], preferred_element_type=F32)
                          for n in range(LRU_BLOCKS)], axis=1)
    rx = jnp.concatenate([jnp.dot(xb[:, n * bd:(n + 1) * bd], wx_ref[n], preferred_element_type=F32)
                          for n in range(LRU_BLOCKS)], axis=1)
    r = jax.nn.sigmoid(ra + ba)
    i = jax.nn.sigmoid(rx + bx)
    log_a = -LRU_C * r * _softplus(-lam)
    a = jnp.exp(log_a)
    b = jnp.sqrt(-jnp.tanh(log_a) * (a * a + 1.0)) * (i * xc)
    return a, b


def _scan_rows(a8, b8, h):
    row = lax.broadcasted_iota(jnp.int32, a8.shape, 0)
    s = 1
    while s < SUBLANES:
        a_prev = jnp.where(row >= s, pltpu.roll(a8, s, 0), 1.0)
        b_prev = jnp.where(row >= s, pltpu.roll(b8, s, 0), 0.0)
        b8 = a8 * b_prev + b8
        a8 = a8 * a_prev
        s *= 2
    return a8 * h + b8


def _lru_prompt_kernel(ux_ref, uy_ref, cw_ref, cb_ref, wa_ref, wx_ref, ba_ref, bx_ref, lam_ref, gain_ref,
                       r_ref, h_ref, x_s, h_s, a_s, b_s):
    tt = ux_ref.shape[0]
    i = pl.program_id(1)

    @pl.when(i == 0)
    def _():
        x_s[0:SUBLANES, :] = jnp.zeros((SUBLANES, x_s.shape[1]), F32)
        h_s[...] = jnp.zeros_like(h_s)

    @pl.when(i > 0)
    def _():
        x_s[0:SUBLANES, :] = x_s[tt:tt + SUBLANES, :]

    x_s[SUBLANES:, :] = ux_ref[...]
    xc = cb_ref[...]
    for j in range(CONV_W):
        k = CONV_W - 1 - j
        xc = xc + x_s[SUBLANES - k:SUBLANES - k + tt, :] * cw_ref[j:j + 1, :]
    a, b = _lru_gates(xc, wa_ref, wx_ref, ba_ref[...], bx_ref[...], lam_ref[...])
    a_s[...] = a
    b_s[...] = b

    def group(k, h):
        r0 = pl.multiple_of(k * SUBLANES, SUBLANES)
        hs = _scan_rows(a_s[pl.ds(r0, SUBLANES), :], b_s[pl.ds(r0, SUBLANES), :], h)
        a_s[pl.ds(r0, SUBLANES), :] = hs
        return hs[SUBLANES - 1:SUBLANES, :]

    h_last = lax.fori_loop(0, tt // SUBLANES, group, h_s[...])
    h_s[...] = h_last
    h_ref[0] = h_last
    y = a_s[...] * jax.nn.gelu(uy_ref[...])
    r_ref[...] = _rms(y, gain_ref[...]).astype(r_ref.dtype)


def _lru_prompt(proj, n_batch, seq_len, weights):
    width = D_ATTN
    tt = _row_tile(seq_len, 512, SUBLANES)
    n_t = seq_len // tt
    rows = lambda col: pl.BlockSpec((tt, width), lambda b, i: (b * n_t + i, col))
    vec = _full((1, width))
    blk = _full((LRU_BLOCKS, width // LRU_BLOCKS, width // LRU_BLOCKS))
    return pl.pallas_call(
        _lru_prompt_kernel,
        grid=(n_batch, n_t),
        in_specs=[rows(COL_UX // width), rows(COL_UY // width), _full((CONV_W, width)), vec, blk, blk,
                  vec, vec, vec, vec],
        out_specs=[pl.BlockSpec((tt, width), lambda b, i: (b * n_t + i, 0)),
                   pl.BlockSpec((1, 1, width), lambda b, i: (b, 0, 0))],
        out_shape=[jax.ShapeDtypeStruct((n_batch * seq_len, width), BF16),
                   jax.ShapeDtypeStruct((n_batch, 1, width), F32)],
        scratch_shapes=[pltpu.VMEM((tt + SUBLANES, width), F32), pltpu.VMEM((1, width), F32),
                        pltpu.VMEM((tt, width), F32), pltpu.VMEM((tt, width), F32)],
        compiler_params=_params(("parallel", "arbitrary")),
        name="lru_prompt",
    )(proj, proj, *weights)


def _lru_sample_kernel(ux_ref, uy_ref, c0_ref, c1_ref, c2_ref, h0_ref, cw_ref, cb_ref, wa_ref, wx_ref,
                       ba_ref, bx_ref, lam_ref, gain_ref, r_ref, h_ref):
    xc = (cb_ref[...] + c0_ref[...] * cw_ref[0:1, :] + c1_ref[...] * cw_ref[1:2, :]
          + c2_ref[...] * cw_ref[2:3, :] + ux_ref[...] * cw_ref[3:4, :])
    a, b = _lru_gates(xc, wa_ref, wx_ref, ba_ref[...], bx_ref[...], lam_ref[...])
    h = a * h0_ref[...] + b
    h_ref[...] = h
    r_ref[...] = _rms(h * jax.nn.gelu(uy_ref[...]), gain_ref[...]).astype(r_ref.dtype)


def _lru_sample(ux, uy, conv_state, h0, weights):
    n, width = ux.shape
    mat = _full((n, width))
    vec = _full((1, width))
    blk = _full((LRU_BLOCKS, width // LRU_BLOCKS, width // LRU_BLOCKS))
    return pl.pallas_call(
        _lru_sample_kernel,
        grid=(1,),
        in_specs=[mat] * 6 + [_full((CONV_W, width)), vec, blk, blk, vec, vec, vec, vec],
        out_specs=[mat, mat],
        out_shape=[jax.ShapeDtypeStruct((n, width), BF16), jax.ShapeDtypeStruct((n, width), F32)],
        compiler_params=_params(("arbitrary",)),
        name="lru_sample",
    )(ux, uy, conv_state[:, 0], conv_state[:, 1], conv_state[:, 2], h0, *weights)


def _swiglu_step(x, wg_ref, wu_ref, wd_ref):
    g = jnp.dot(x, wg_ref[...].astype(BF16), preferred_element_type=F32)
    u = jnp.dot(x, wu_ref[...].astype(BF16), preferred_element_type=F32)
    hid = (jax.nn.silu(g) * u).astype(BF16)
    return jnp.dot(hid, wd_ref[...].astype(BF16), preferred_element_type=F32)


def _ffn_kernel(x_ref, wg_ref, wu_ref, wd_ref, res_ref, o_ref):
    @pl.when(pl.program_id(1) == 0)
    def _():
        o_ref[...] = res_ref[...]

    o_ref[...] += _swiglu_step(x_ref[...], wg_ref, wu_ref, wd_ref)


def _ffn(hn, wg, wu, wd, res):
    n, d = hn.shape
    d_ff = wg.shape[1]
    tm = _row_tile(n, 832)
    tf = _row_tile(d_ff, 256, LANES)
    row = pl.BlockSpec((tm, d), lambda i, f: (i, 0))
    return pl.pallas_call(
        _ffn_kernel,
        grid=(n // tm, d_ff // tf),
        in_specs=[row, pl.BlockSpec((d, tf), lambda i, f: (0, f)), pl.BlockSpec((d, tf), lambda i, f: (0, f)),
                  pl.BlockSpec((tf, d), lambda i, f: (f, 0)), row],
        out_specs=row,
        out_shape=jax.ShapeDtypeStruct((n, d), F32),
        compiler_params=_params(("parallel", "arbitrary")),
        name="ffn",
    )(hn, wg, wu, wd, res)


MOE_BLOCK = 1024
MOE_TILE = 256
COMBINE_TILE = 320


def _router_kernel(x_ref, g_ref, w_ref, o_ref, *, n_experts):
    hn = _rms(x_ref[...], g_ref[...]).astype(BF16)
    logits = jnp.dot(hn, w_ref[...], preferred_element_type=F32)
    lane = lax.broadcasted_iota(jnp.int32, logits.shape, 1).astype(F32)
    logits = jnp.where(lane < n_experts, logits, -jnp.inf)
    m1 = jnp.max(logits, axis=-1, keepdims=True)
    i1 = jnp.min(jnp.where(logits == m1, lane, float(LANES)), axis=-1, keepdims=True)
    rest = jnp.where(lane == i1, -jnp.inf, logits)
    m2 = jnp.max(rest, axis=-1, keepdims=True)
    i2 = jnp.min(jnp.where(rest == m2, lane, float(LANES)), axis=-1, keepdims=True)
    e2 = jnp.exp(m2 - m1)
    den = 1.0 + e2
    o_ref[...] = (jnp.where(lane == 0.0, 1.0 / den, 0.0) + jnp.where(lane == 1.0, e2 / den, 0.0)
                  + jnp.where(lane == 2.0, i1, 0.0) + jnp.where(lane == 3.0, i2, 0.0))


def _router(x, gain, w_router):
    n, d = x.shape
    n_experts = w_router.shape[1]
    tm = _row_tile(n, 832)
    w = jnp.concatenate([w_router, jnp.zeros((d, LANES - n_experts), w_router.dtype)], axis=1).astype(BF16)
    return pl.pallas_call(
        functools.partial(_router_kernel, n_experts=n_experts),
        grid=(n // tm,),
        in_specs=[pl.BlockSpec((tm, d), lambda i: (i, 0)), _full((1, d)), _full((d, LANES))],
        out_specs=pl.BlockSpec((tm, LANES), lambda i: (i, 0)),
        out_shape=jax.ShapeDtypeStruct((n, LANES), F32),
        compiler_params=_params(("parallel",)),
        name="router",
    )(x, gain.reshape(1, d), w)


def _start_row_gather(ids_ref, first, n_rows, src_hbm, dst, sem):
    def body(r, carry):
        pltpu.make_async_copy(src_hbm.at[pl.ds(ids_ref[first + r], 1)], dst.at[pl.ds(r, 1)], sem).start()
        return carry

    lax.fori_loop(0, n_rows, body, 0, unroll=8)


def _dispatch_kernel(ids_ref, on_ref, x_hbm, g_ref, o_ref, buf, sem):
    i = pl.program_id(0)
    slot = i % 2
    start = lambda tile, s: _start_row_gather(ids_ref, tile * MOE_TILE, MOE_TILE, x_hbm, buf.at[s], sem.at[s])

    @pl.when((i == 0) & (on_ref[0] > 0))
    def _():
        start(0, 0)

    @pl.when(i + 1 < pl.num_programs(0))
    def _():
        @pl.when(on_ref[i + 1] > 0)
        def _():
            start(i + 1, 1 - slot)

    @pl.when(on_ref[i] > 0)
    def _():
        pltpu.make_async_copy(buf.at[slot], buf.at[slot], sem.at[slot]).wait()
        o_ref[...] = _rms(buf[slot], g_ref[...]).astype(o_ref.dtype)

    @pl.when(on_ref[i] == 0)
    def _():
        o_ref[...] = jnp.zeros_like(o_ref)


def _dispatch(x, gain, row_ids, tile_on):
    n, d = x.shape
    n_slots = row_ids.shape[0]
    return pl.pallas_call(
        _dispatch_kernel,
        grid_spec=pltpu.PrefetchScalarGridSpec(
            num_scalar_prefetch=2,
            grid=(n_slots // MOE_TILE,),
            in_specs=[pl.BlockSpec(memory_space=pl.ANY), pl.BlockSpec((1, d), lambda i, ids, on: (0, 0))],
            out_specs=pl.BlockSpec((MOE_TILE, d), lambda i, ids, on: (i, 0)),
            scratch_shapes=[pltpu.VMEM((2, MOE_TILE, d), F32), pltpu.SemaphoreType.DMA((2,))],
        ),
        out_shape=jax.ShapeDtypeStruct((n_slots, d), BF16),
        compiler_params=_params(("arbitrary",)),
        name="moe_dispatch",
    )(row_ids, tile_on, x, gain.reshape(1, d))


def _experts_kernel(be_ref, br_ref, x_ref, w_ref, wg_ref, wu_ref, wd_ref, o_ref):
    b, f = pl.program_id(0), pl.program_id(1)
    rows = br_ref[b]

    @pl.when(f == 0)
    def _():
        o_ref[...] = jnp.zeros_like(o_ref)

    @pl.when(rows == MOE_BLOCK)
    def _():
        o_ref[...] += _swiglu_step(x_ref[...], wg_ref.at[0], wu_ref.at[0], wd_ref.at[0])

    @pl.when((rows > 0) & (rows < MOE_BLOCK))
    def _():
        for s in range(MOE_BLOCK // MOE_TILE):
            @pl.when(s * MOE_TILE < rows)
            def _():
                sub = pl.ds(s * MOE_TILE, MOE_TILE)
                o_ref[sub, :] += _swiglu_step(x_ref[sub, :], wg_ref.at[0], wu_ref.at[0], wd_ref.at[0])

    @pl.when((f == pl.num_programs(1) - 1) & (rows > 0))
    def _():
        o_ref[...] = o_ref[...] * w_ref[...]


def _experts(xs, slot_w, blk_expert, blk_rows, wg, wu, wd):
    n_slots, d = xs.shape
    d_ff = wg.shape[2]
    tf = _row_tile(d_ff, 256, LANES)
    n_f = d_ff // tf
    fcol = lambda b, f, br: jnp.where(br[b] > 0, f, n_f - 1)
    return pl.pallas_call(
        _experts_kernel,
        grid_spec=pltpu.PrefetchScalarGridSpec(
            num_scalar_prefetch=2,
            grid=(n_slots // MOE_BLOCK, n_f),
            in_specs=[pl.BlockSpec((MOE_BLOCK, d), lambda b, f, be, br: (b, 0)),
                      pl.BlockSpec((MOE_BLOCK, 1), lambda b, f, be, br: (b, 0)),
                      pl.BlockSpec((1, d, tf), lambda b, f, be, br: (be[b], 0, fcol(b, f, br))),
                      pl.BlockSpec((1, d, tf), lambda b, f, be, br: (be[b], 0, fcol(b, f, br))),
                      pl.BlockSpec((1, tf, d), lambda b, f, be, br: (be[b], fcol(b, f, br), 0))],
            out_specs=pl.BlockSpec((MOE_BLOCK, d), lambda b, f, be, br: (b, 0)),
        ),
        out_shape=jax.ShapeDtypeStruct((n_slots, d), F32),
        compiler_params=_params(("arbitrary", "arbitrary")),
        name="moe_experts",
    )(blk_expert, blk_rows, xs, slot_w, wg, wu, wd)


def _combine_kernel(s0_ref, s1_ref, y_hbm, x_ref, o_ref, buf, sem):
    tc = x_ref.shape[0]
    i = pl.program_id(0)
    slot = i % 2

    def start(tile, s):
        _start_row_gather(s0_ref, tile * tc, tc, y_hbm, buf.at[s, 0], sem.at[s])
        _start_row_gather(s1_ref, tile * tc, tc, y_hbm, buf.at[s, 1], sem.at[s])

    @pl.when(i == 0)
    def _():
        start(0, 0)

    @pl.when(i + 1 < pl.num_programs(0))
    def _():
        start(i + 1, 1 - slot)

    pltpu.make_async_copy(buf.at[slot], buf.at[slot], sem.at[slot]).wait()
    o_ref[...] = x_ref[...] + buf[slot, 0] + buf[slot, 1]


def _combine(x, y, slot0, slot1):
    n, d = x.shape
    tc = _row_tile(n, COMBINE_TILE, SUBLANES)
    row = pl.BlockSpec((tc, d), lambda i, s0, s1: (i, 0))
    return pl.pallas_call(
        _combine_kernel,
        grid_spec=pltpu.PrefetchScalarGridSpec(
            num_scalar_prefetch=2,
            grid=(n // tc,),
            in_specs=[pl.BlockSpec(memory_space=pl.ANY), row],
            out_specs=row,
            scratch_shapes=[pltpu.VMEM((2, 2, tc, d), F32), pltpu.SemaphoreType.DMA((2,))],
        ),
        out_shape=jax.ShapeDtypeStruct((n, d), F32),
        compiler_params=_params(("arbitrary",)),
        name="moe_combine",
    )(slot0, slot1, y, x)


def _routing_tables(route, n_experts):
    n = route.shape[0]
    expert = route[:, 2:4].astype(jnp.int32).T.reshape(-1)
    weight = route[:, 0:2].T.reshape(-1)
    token = jnp.tile(jnp.arange(n, dtype=jnp.int32), 2)
    onehot = (expert[:, None] == jnp.arange(n_experts, dtype=jnp.int32)).astype(jnp.int32)
    rank = jnp.sum((jnp.cumsum(onehot, axis=0) - 1) * onehot, axis=1)
    count = jnp.sum(onehot, axis=0)
    padded = (count + MOE_BLOCK - 1) // MOE_BLOCK * MOE_BLOCK
    ends = jnp.cumsum(padded)
    starts = ends - padded
    slot = starts[expert] + rank
    n_slots = (2 * n + n_experts * (MOE_BLOCK - 1) + MOE_BLOCK - 1) // MOE_BLOCK * MOE_BLOCK
    row_ids = jnp.zeros((n_slots,), jnp.int32).at[slot].set(token)
    slot_w = jnp.zeros((n_slots,), F32).at[slot].set(weight).reshape(n_slots, 1)
    blk_first = jnp.arange(n_slots // MOE_BLOCK, dtype=jnp.int32) * MOE_BLOCK
    blk_expert = jnp.minimum(jnp.sum((ends[None, :] <= blk_first[:, None]).astype(jnp.int32), axis=1), n_experts - 1)
    used = (count + MOE_TILE - 1) // MOE_TILE * MOE_TILE
    blk_rows = jnp.clip(starts[blk_expert] + used[blk_expert] - blk_first, 0, MOE_BLOCK)
    blk_rows = jnp.where(blk_first < ends[-1], blk_rows, 0).astype(jnp.int32)
    tile_first = jnp.arange(n_slots // MOE_TILE, dtype=jnp.int32) * MOE_TILE
    tile_on = (tile_first % MOE_BLOCK < jnp.repeat(blk_rows, MOE_BLOCK // MOE_TILE)).astype(jnp.int32)
    return row_ids, slot_w, blk_expert, blk_rows, tile_on, slot[:n], slot[n:]


def _moe(x, gain, w_router, wg, wu, wd):
    route = _router(x, gain, w_router)
    row_ids, slot_w, blk_expert, blk_rows, tile_on, slot0, slot1 = _routing_tables(route, w_router.shape[1])
    xs = _dispatch(x, gain, row_ids, tile_on)
    y = _experts(xs, slot_w, blk_expert, blk_rows, wg, wu, wd)
    return _combine(x, y, slot0, slot1)


def kernel(x_prompt, x_sample, cache_cmp, cache_sel, state_win, state_lru, state_conv, page_table, ln_mix, w_in, q_norm, k_norm, cmp_pe, cmp_w1, cmp_w2, conv_w, conv_b, lru_wa, lru_ba, lru_wx, lru_bx, lru_lambda, out_norm_attn, out_norm_lru, w_out, ln_ffn, ffn_w_gate, ffn_w_up, ffn_w_down, moe_router, moe_w_gate, moe_w_up, moe_w_down):
    n_batch, seq_len, d_model = x_prompt.shape
    dec_batch, dec_seq, _ = x_sample.shape
    depth = ln_mix.shape[0]
    page = cache_cmp.shape[2]
    n_pages = page_table.shape[1]
    past_len = n_pages * page
    win_buf = state_win.shape[2]
    lru_width = d_model - D_ATTN
    assert dec_seq == 1 and lru_width == D_ATTN
    assert seq_len % Q_BLOCK == 0 and seq_len >= WIN_SPAN and seq_len % SEL_CHUNK == 0 and seq_len >= win_buf
    assert past_len % Q_BLOCK == 0 and win_buf <= past_len and page % CMP_BLOCK == 0
    n_p = n_batch * seq_len
    nc = seq_len // CMP_BLOCK
    nc_past = past_len // CMP_BLOCK
    kv_row = (2, N_KV, HEAD_DIM)

    pos = jnp.concatenate([jnp.tile(jnp.arange(seq_len), n_batch), jnp.full((dec_batch,), past_len)])
    tables = _rope_tables(pos)
    end_p = _rope_tables((jnp.arange(nc) + 1) * CMP_BLOCK - 1)
    end_s = _rope_tables((jnp.arange(nc_past) + 1) * CMP_BLOCK - 1)

    x = jnp.concatenate([x_prompt.reshape(n_p, d_model), x_sample.reshape(dec_batch, d_model)], axis=0)
    outs = [[] for _ in range(10)]
    for l in range(depth):
        hn = _rmsnorm(x, ln_mix[l])
        proj = _matmul([hn], [_reorder_w_in(w_in[l])], tm_target=832, tn_target=1280)
        q_b, sel_f, sel_b, win_f, win_b, gates = _postproj(proj, tables, q_norm[l], k_norm[l, 1], k_norm[l, 2])
        c_rows = proj[:, COL_CMP:COL_CMP + ROW_WIDTH]

        cmp_w = (cmp_pe[l], cmp_w1[l].reshape(2, CMP_BLOCK, HEAD_DIM, HEAD_DIM).astype(BF16),
                 cmp_w2[l].astype(BF16), k_norm[l, 0].reshape(1, HEAD_DIM))
        lru_w = (conv_w[l], conv_b[l].reshape(1, -1), lru_wa[l].astype(BF16), lru_wx[l].astype(BF16),
                 lru_ba[l].reshape(1, -1), lru_bx[l].reshape(1, -1), lru_lambda[l].reshape(1, -1),
                 out_norm_lru[l].reshape(1, -1))

        kc_p, vc_p = _compress_prompt(c_rows[:n_p].reshape(n_batch * nc, CMP_ROW), n_batch, nc, cmp_w, end_p)
        o_p = _attn_prompt(q_b, _deinterleave_blocks(kc_p, n_batch, nc), _deinterleave_blocks(vc_p, n_batch, nc),
                           sel_b, win_b, gates, out_norm_attn[l], n_batch, seq_len)
        r_p, h_p = _lru_prompt(proj, n_batch, seq_len, lru_w)

        kc_s, vc_s = _compress_sample(cache_cmp, l, page_table, cmp_w, end_s, nb=2 if dec_batch % 2 == 0 else 1)
        o_s, new_win_s = _attn_sample(q_b[n_p:], _deinterleave_blocks(kc_s, dec_batch, nc_past),
                                      _deinterleave_blocks(vc_s, dec_batch, nc_past), sel_f[n_p:], cache_sel,
                                      state_win, win_f[n_p:], gates[n_p:], out_norm_attn[l], l, page_table)
        ux_s = proj[n_p:, COL_UX:COL_UX + lru_width]
        r_s, h_s = _lru_sample(ux_s, proj[n_p:, COL_UY:COL_UY + lru_width], state_conv[l], state_lru[l], lru_w)

        w_o = w_out[l].astype(BF16)
        x = _matmul([jnp.concatenate([o_p, o_s], axis=0), jnp.concatenate([r_p, r_s], axis=0)],
                    [w_o[:D_ATTN], w_o[D_ATTN:]], res=x, tm_target=832, tn_target=1024)

        k = l // 2
        if l % 2 == 0:
            x = _ffn(_rmsnorm(x, ln_ffn[l]), ffn_w_gate[k], ffn_w_up[k], ffn_w_down[k], x)
        else:
            x = _moe(x, ln_ffn[l], moe_router[k], moe_w_gate[k], moe_w_up[k], moe_w_down[k])

        ux_p = proj[:n_p, COL_UX:COL_UX + lru_width].reshape(n_batch, seq_len, lru_width)
        new = (c_rows[:n_p].reshape((n_batch, seq_len) + kv_row), c_rows[n_p:].reshape((dec_batch, 1) + kv_row),
               sel_f[:n_p].reshape((n_batch, seq_len) + kv_row), sel_f[n_p:].reshape((dec_batch, 1) + kv_row),
               win_f[:n_p].reshape((n_batch, seq_len) + kv_row)[:, seq_len - win_buf:],
               new_win_s,
               h_p.reshape(n_batch, lru_width), h_s,
               ux_p[:, seq_len - (CONV_W - 1):],
               jnp.concatenate([state_conv[l][:, 1:], ux_s[:, None, :]], axis=1))
        for acc, val in zip(outs, new):
            acc.append(val)

    return (x[:n_p].reshape(n_batch, seq_len, d_model), x[n_p:].reshape(dec_batch, 1, d_model),
            *[jnp.stack(v) for v in outs])
```

```python
import functools

import jax
import jax.numpy as jnp
from jax import lax
from jax.experimental import pallas as pl
from jax.experimental.pallas import tpu as pltpu

N_HEADS = 8
HEAD_DIM = 128
N_KV = 2
Q_PER_KV = N_HEADS // N_KV
D_ATTN = N_HEADS * HEAD_DIM
KV_WIDTH = N_KV * HEAD_DIM
ROW_WIDTH = 2 * KV_WIDTH
UNITS = 2 * N_KV
ROT_DIM = HEAD_DIM // 4
ROPE_THETA = 500000.0
CMP_BLOCK = 32
SEL_BLOCK = 64
TOP_N = 16
WINDOW = 512
Q_BLOCK = 128
LRU_BLOCKS = 8
CONV_W = 4
LRU_C = 8.0
TOP_K = 2
EPS = 1e-6
SCALE = HEAD_DIM ** -0.5
NEG_INF = -1e30

LANES = 128
SUBLANES = 8
VMEM_LIMIT = 56 * 1024 * 1024

F32 = jnp.float32
BF16 = jnp.bfloat16


def _row_tile(n, target, mult=16):
    best = None
    for t in range(mult, min(n, target) + 1, mult):
        if n % t == 0:
            best = t
    return best if best is not None else n


def _params(semantics):
    return pltpu.CompilerParams(dimension_semantics=semantics, vmem_limit_bytes=VMEM_LIMIT)


def _full(shape):
    zeros = (0,) * len(shape)
    return pl.BlockSpec(shape, lambda *_: zeros)


def _rms(x, gain):
    return x * lax.rsqrt(jnp.mean(x * x, axis=-1, keepdims=True) + EPS) * gain


def _rope_apply(y, cos_t, sin_lo, sin_hi):
    half = ROT_DIM // 2
    return y * cos_t + pltpu.roll(y, HEAD_DIM - half, 1) * sin_lo + pltpu.roll(y, half, 1) * sin_hi


def _softplus(x):
    return jnp.maximum(x, 0.0) + jnp.log1p(jnp.exp(-jnp.abs(x)))


def _topk_mask(score, n_pick):
    n_lanes = score.shape[-1]
    lane = lax.broadcasted_iota(jnp.int32, score.shape, 1).astype(F32)
    sel = jnp.zeros(score.shape, F32)
    for _ in range(n_pick):
        m = jnp.max(score, axis=-1, keepdims=True)
        idx = jnp.min(jnp.where(score == m, lane, float(n_lanes)), axis=-1, keepdims=True)
        pick = lane == idx
        sel = jnp.where(pick, 1.0, sel)
        score = jnp.where(pick, -2.0, score)
    return sel


def _topk_mask_cols(score, n_pick):
    n_rows = score.shape[0]
    row = lax.broadcasted_iota(jnp.int32, score.shape, 0).astype(F32)
    sel = jnp.zeros(score.shape, F32)
    for _ in range(n_pick):
        m = jnp.max(score, axis=0, keepdims=True)
        idx = jnp.min(jnp.where(score == m, row, float(n_rows)), axis=0, keepdims=True)
        pick = row == idx
        sel = jnp.where(pick, 1.0, sel)
        score = jnp.where(pick, -2.0, score)
    return sel


def _masked_softmax_rows(s, mask):
    sm = jnp.where(mask, s, NEG_INF)
    m = jnp.max(sm, axis=-1, keepdims=True)
    e = jnp.where(mask, jnp.exp(sm - m), 0.0)
    l = jnp.sum(e, axis=-1, keepdims=True)
    return e / jnp.where(l > 0.0, l, 1.0)


def _dot_nt(a, b):
    return lax.dot_general(a, b, (((1,), (1,)), ((), ())), preferred_element_type=F32)


def _rmsnorm_kernel(x_ref, g_ref, o_ref):
    o_ref[...] = _rms(x_ref[...], g_ref[...]).astype(o_ref.dtype)


def _rmsnorm(x, gain):
    n, d = x.shape
    tm = _row_tile(n, 832)
    return pl.pallas_call(
        _rmsnorm_kernel,
        grid=(n // tm,),
        in_specs=[pl.BlockSpec((tm, d), lambda i: (i, 0)), _full((1, d))],
        out_specs=pl.BlockSpec((tm, d), lambda i: (i, 0)),
        out_shape=jax.ShapeDtypeStruct((n, d), BF16),
        compiler_params=_params(("parallel",)),
        name="rmsnorm",
    )(x, gain.reshape(1, d))


def _matmul_kernel(*refs, n_lhs, has_res):
    o_ref = refs[-1]
    acc = jnp.dot(refs[0][...], refs[n_lhs][...], preferred_element_type=F32)
    for k in range(1, n_lhs):
        acc = acc + jnp.dot(refs[k][...], refs[n_lhs + k][...], preferred_element_type=F32)
    if has_res:
        acc = acc + refs[2 * n_lhs][...]
    o_ref[...] = acc


def _matmul(xs, ws, res=None, *, tm_target, tn_target):
    n = xs[0].shape[0]
    d_out = ws[0].shape[1]
    tm = _row_tile(n, tm_target)
    tn = _row_tile(d_out, tn_target, LANES)
    in_specs = [pl.BlockSpec((tm, x.shape[1]), lambda i, j: (i, 0)) for x in xs]
    in_specs += [pl.BlockSpec((w.shape[0], tn), lambda i, j: (0, j)) for w in ws]
    args = list(xs) + list(ws)
    if res is not None:
        in_specs.append(pl.BlockSpec((tm, tn), lambda i, j: (i, j)))
        args.append(res)
    return pl.pallas_call(
        functools.partial(_matmul_kernel, n_lhs=len(xs), has_res=res is not None),
        grid=(n // tm, d_out // tn),
        in_specs=in_specs,
        out_specs=pl.BlockSpec((tm, tn), lambda i, j: (i, j)),
        out_shape=jax.ShapeDtypeStruct((n, d_out), F32),
        compiler_params=_params(("parallel", "parallel")),
        name="matmul",
    )(*args)


PROJ_WIDTH = 5120
COL_UX, COL_UY, COL_CMP, COL_SEL, COL_WIN, COL_GATE = 1024, 2048, 3072, 3584, 4096, 4608
GATE_WIDTH = 2 * LANES


def _reorder_w_in(w):
    lru = D_ATTN
    c_kv = D_ATTN
    c_g = c_kv + 3 * ROW_WIDTH
    c_ux = c_g + 3 * N_HEADS
    c_uy = c_ux + lru
    n_g = 3 * Q_PER_KV
    zpad = lambda k: jnp.zeros((w.shape[0], k), w.dtype)
    parts = [w[:, :D_ATTN], w[:, c_ux:c_uy], w[:, c_uy:c_uy + lru], w[:, c_kv:c_g],
             w[:, c_g:c_g + n_g], zpad(LANES - n_g), w[:, c_g + n_g:c_ux], zpad(LANES - n_g)]
    width = sum(p.shape[1] for p in parts)
    parts.append(zpad(PROJ_WIDTH - width))
    return jnp.concatenate(parts, axis=1).astype(BF16)


def _postproj_kernel(q_ref, sel_ref, win_ref, gate_ref, cos_ref, slo_ref, shi_ref,
                     qg_ref, ksg_ref, kwg_ref,
                     qo_ref, sel_f_ref, sel_b_ref, win_f_ref, win_b_ref, gate_o_ref):
    cos_t, slo, shi = cos_ref[...], slo_ref[...], shi_ref[...]
    for h in range(N_HEADS):
        cols = slice(h * HEAD_DIM, (h + 1) * HEAD_DIM)
        y = _rope_apply(_rms(q_ref[:, cols], qg_ref[...]), cos_t, slo, shi)
        qo_ref[:, cols] = y.astype(qo_ref.dtype)
    for src, gain_ref, out_f, out_b in ((sel_ref, ksg_ref, sel_f_ref, sel_b_ref),
                                        (win_ref, kwg_ref, win_f_ref, win_b_ref)):
        for g in range(N_KV):
            cols = slice(g * HEAD_DIM, (g + 1) * HEAD_DIM)
            y = _rope_apply(_rms(src[:, cols], gain_ref[...]), cos_t, slo, shi)
            out_f[:, cols] = y
            out_b[:, cols] = y.astype(out_b.dtype)
        v = src[:, KV_WIDTH:]
        out_f[:, KV_WIDTH:] = v
        out_b[:, KV_WIDTH:] = v.astype(out_b.dtype)
    gate_o_ref[...] = jax.nn.sigmoid(gate_ref[...])


def _postproj(proj, tables, q_gain, ks_gain, kw_gain):
    n = proj.shape[0]
    tm = _row_tile(n, 416)
    rows = lambda width, col: pl.BlockSpec((tm, width), lambda i: (i, col))
    tab = pl.BlockSpec((tm, HEAD_DIM), lambda i: (i, 0))
    gain = _full((1, HEAD_DIM))
    out = lambda width: pl.BlockSpec((tm, width), lambda i: (i, 0))
    sds = jax.ShapeDtypeStruct
    return pl.pallas_call(
        _postproj_kernel,
        grid=(n // tm,),
        in_specs=[rows(D_ATTN, 0), rows(ROW_WIDTH, COL_SEL // ROW_WIDTH), rows(ROW_WIDTH, COL_WIN // ROW_WIDTH),
                  rows(GATE_WIDTH, COL_GATE // GATE_WIDTH), tab, tab, tab, gain, gain, gain],
        out_specs=[out(D_ATTN), out(ROW_WIDTH), out(ROW_WIDTH), out(ROW_WIDTH), out(ROW_WIDTH), out(GATE_WIDTH)],
        out_shape=[sds((n, D_ATTN), BF16), sds((n, ROW_WIDTH), F32), sds((n, ROW_WIDTH), BF16),
                   sds((n, ROW_WIDTH), F32), sds((n, ROW_WIDTH), BF16), sds((n, GATE_WIDTH), F32)],
        compiler_params=_params(("parallel",)),
        name="postproj",
    )(proj, proj, proj, proj, *tables, q_gain.reshape(1, -1), ks_gain.reshape(1, -1), kw_gain.reshape(1, -1))


def _rope_tables(pos):
    half = ROT_DIM // 2
    inv_freq = ROPE_THETA ** (-2.0 * jnp.arange(half, dtype=F32) / ROT_DIM)
    ang = pos.astype(F32)[:, None] * inv_freq
    cos, sin = jnp.cos(ang), jnp.sin(ang)
    n = pos.shape[0]
    ones = jnp.ones((n, HEAD_DIM - ROT_DIM), F32)
    zeros = jnp.zeros((n, HEAD_DIM - half), F32)
    cos_t = jnp.concatenate([cos, cos, ones], axis=1)
    sin_lo = jnp.concatenate([-sin, zeros], axis=1)
    sin_hi = jnp.concatenate([jnp.zeros((n, half), F32), sin, jnp.zeros((n, HEAD_DIM - ROT_DIM), F32)], axis=1)
    return cos_t, sin_lo, sin_hi


CMP_ROW = CMP_BLOCK * ROW_WIDTH


def _compress_rows(piece, n_rows, w1_ref, w2_ref, gain_ref, cos_ref, slo_ref, shi_ref, k_ref, v_ref):
    for n in range(2):
        acc = jnp.zeros((N_KV * n_rows, HEAD_DIM), F32)
        for t in range(CMP_BLOCK):
            lhs = jnp.concatenate([piece(n, t, g) for g in range(N_KV)], axis=0)
            acc = acc + jnp.dot(lhs, w1_ref[n, t], preferred_element_type=F32)
        hid = jax.nn.gelu(acc).astype(BF16)
        comp = jnp.dot(hid, w2_ref[n], preferred_element_type=F32)
        for g in range(N_KV):
            part = comp[g * n_rows:(g + 1) * n_rows]
            cols = slice(g * HEAD_DIM, (g + 1) * HEAD_DIM)
            if n == 0:
                part = _rope_apply(_rms(part, gain_ref[...]), cos_ref[...], slo_ref[...], shi_ref[...])
                k_ref[:, cols] = part.astype(k_ref.dtype)
            else:
                v_ref[:, cols] = part.astype(v_ref.dtype)


def _compress_prompt_kernel(x_ref, pe_ref, *rest):
    def piece(n, t, g):
        c0 = t * ROW_WIDTH + n * KV_WIDTH + g * HEAD_DIM
        return (x_ref[:, c0:c0 + HEAD_DIM] + pe_ref[n, t:t + 1, :]).astype(BF16)

    _compress_rows(piece, x_ref.shape[0], *rest)


def _compress_sample_kernel(pt_ref, *refs, n_page_refs, page):
    del pt_ref
    pages = refs[:n_page_refs]
    pe_ref = refs[n_page_refs]
    rest = refs[n_page_refs + 1:-1]
    y_s = refs[-1]
    n_tok = 2 * page
    out_row = lax.broadcasted_iota(jnp.int32, (n_tok, n_tok), 0)
    src_tok = lax.broadcasted_iota(jnp.int32, (n_tok, n_tok), 1)
    regroup = ((out_row % SUBLANES) * CMP_BLOCK + out_row // SUBLANES == src_tok).astype(BF16)
    pe_tok = [jnp.concatenate([pe_ref[n]] * (n_tok // CMP_BLOCK), axis=0) for n in range(2)]
    for pair in range(n_page_refs // 2):
        for s in range(UNITS):
            tok = jnp.concatenate([pages[2 * pair + k][pl.ds(s, page, stride=UNITS), :] for k in range(2)], axis=0)
            y = jnp.dot(regroup, (tok + pe_tok[s // N_KV]).astype(BF16), preferred_element_type=F32)
            y_s[s, :, pair * SUBLANES:(pair + 1) * SUBLANES, :] = y.reshape(CMP_BLOCK, SUBLANES, HEAD_DIM)

    _compress_rows(lambda n, t, g: y_s[n * N_KV + g, t].astype(BF16), y_s.shape[2], *rest)


def _compress_weight_specs():
    return [_full((2, CMP_BLOCK, HEAD_DIM)), _full((2, CMP_BLOCK, HEAD_DIM, HEAD_DIM)),
            _full((2, HEAD_DIM, HEAD_DIM)), _full((1, HEAD_DIM))]


def _compress_prompt(c_rows, n_batch, nc, weights, end_tables):
    tr = _row_tile(nc, 128)
    per_b = nc // tr
    tab = pl.BlockSpec((tr, HEAD_DIM), lambda i: (i % per_b, 0))
    out = pl.BlockSpec((tr, KV_WIDTH), lambda i: (i, 0))
    sds = jax.ShapeDtypeStruct((n_batch * nc, KV_WIDTH), BF16)
    return pl.pallas_call(
        _compress_prompt_kernel,
        grid=(n_batch * per_b,),
        in_specs=[pl.BlockSpec((tr, CMP_ROW), lambda i: (i, 0))] + _compress_weight_specs() + [tab, tab, tab],
        out_specs=[out, out],
        out_shape=[sds, sds],
        compiler_params=_params(("parallel",)),
        name="compress_prompt",
    )(c_rows, *weights, *end_tables)


def _compress_sample(cache, layer, page_table, weights, end_tables, nb):
    depth, n_phys, page = cache.shape[:3]
    n_batch, n_pages = page_table.shape
    assert n_pages % 2 == 0 and 2 * page == SUBLANES * CMP_BLOCK
    nc = n_pages * page // CMP_BLOCK
    pages = cache.reshape(depth * n_phys * page * UNITS, HEAD_DIM)
    base = layer * n_phys
    page_specs = [pl.BlockSpec((page * UNITS, HEAD_DIM), lambda i, pt, bb=bb, p=p: (base + pt[i * nb + bb, p], 0))
                  for bb in range(nb) for p in range(n_pages)]
    n_rows = nb * nc
    tab = pl.BlockSpec((n_rows, HEAD_DIM), lambda i, pt: (0, 0))
    wspecs = [pl.BlockSpec(s.block_shape, lambda i, pt, z=(0,) * len(s.block_shape): z) for s in _compress_weight_specs()]
    out = pl.BlockSpec((n_rows, KV_WIDTH), lambda i, pt: (i, 0))
    sds = jax.ShapeDtypeStruct((n_batch * nc, KV_WIDTH), BF16)
    tables = [jnp.tile(t, (nb, 1)) for t in end_tables]
    return pl.pallas_call(
        functools.partial(_compress_sample_kernel, n_page_refs=nb * n_pages, page=page),
        grid_spec=pltpu.PrefetchScalarGridSpec(
            num_scalar_prefetch=1,
            grid=(n_batch // nb,),
            in_specs=page_specs + wspecs + [tab, tab, tab],
            out_specs=[out, out],
            scratch_shapes=[pltpu.VMEM((UNITS, CMP_BLOCK, n_rows, HEAD_DIM), F32)],
        ),
        out_shape=[sds, sds],
        compiler_params=_params(("arbitrary",)),
        name="compress_sample",
    )(page_table, *([pages] * (nb * n_pages)), *weights, *tables)


def _deinterleave_blocks(x, n_batch, nc):
    x = x.reshape(n_batch, nc, -1)
    return jnp.concatenate([x[:, 0::2], x[:, 1::2]], axis=1)


SEL_CHUNK = 256
WIN_SPAN = WINDOW + Q_BLOCK


def _attn_prompt_kernel(q_ref, kc_ref, vc_ref, sel_ref, win_ref, gate_ref, gain_ref, o_ref, *, seq_len):
    nc = seq_len // CMP_BLOCK
    ns = seq_len // SEL_BLOCK
    i = pl.program_id(1)
    q0 = i * Q_BLOCK
    tpos = q0 + lax.broadcasted_iota(jnp.int32, (Q_BLOCK, 1), 0)
    tlane = q0 + lax.broadcasted_iota(jnp.int32, (1, Q_BLOCK), 1)
    kcol = lambda g: slice(g * HEAD_DIM, (g + 1) * HEAD_DIM)
    vcol = lambda g: slice(KV_WIDTH + g * HEAD_DIM, KV_WIDTH + (g + 1) * HEAD_DIM)
    rows = lambda h: slice(h * Q_BLOCK, (h + 1) * Q_BLOCK)
    n_rows = Q_PER_KV * Q_BLOCK

    qs, o_c, o_w, sel = [], [], [], []
    for g in range(N_KV):
        q_g = jnp.concatenate([q_ref[:, (g * Q_PER_KV + h) * HEAD_DIM:(g * Q_PER_KV + h + 1) * HEAD_DIM]
                               for h in range(Q_PER_KV)], axis=0)
        qs.append(q_g)

        col = lax.broadcasted_iota(jnp.int32, (Q_BLOCK, nc), 1)
        blk = jnp.where(col < ns, 2 * col, 2 * (col - ns) + 1)
        c_mask = ((blk + 1) * CMP_BLOCK - 1) <= tpos
        s_c = (_dot_nt(q_g, kc_ref[0, :, kcol(g)]) * SCALE).reshape(Q_PER_KV, Q_BLOCK, nc)
        p_c = _masked_softmax_rows(s_c, c_mask[None])
        o_c.append(jnp.dot(p_c.reshape(n_rows, nc).astype(BF16), vc_ref[0, :, kcol(g)], preferred_element_type=F32))
        p_tok = p_c[0] + p_c[1] + p_c[2] + p_c[3]

        p_blk = p_tok.T
        imp = p_blk[:ns] + p_blk[ns:]
        sblk = lax.broadcasted_iota(jnp.int32, (ns, Q_BLOCK), 0)
        allowed = sblk * SEL_BLOCK <= tlane
        forced = (sblk == 0) | (sblk == tlane // SEL_BLOCK)
        score = jnp.where(forced, 1e30, jnp.where(allowed, imp, -1.0))
        picked = _topk_mask_cols(score, min(TOP_N, ns))
        picked = jnp.concatenate([picked, jnp.zeros((LANES - ns, Q_BLOCK), F32)], axis=0)
        sel.append(picked.T.astype(BF16))

        w0 = pl.multiple_of(jnp.maximum(q0 - WINDOW, 0), Q_BLOCK)
        wpos = w0 + lax.broadcasted_iota(jnp.int32, (1, WIN_SPAN), 1)
        dist = tpos - wpos
        w_mask = ((dist >= 0) & (dist <= WINDOW))[None]
        s_w = (_dot_nt(q_g, win_ref[pl.ds(w0, WIN_SPAN), kcol(g)]) * SCALE).reshape(Q_PER_KV, Q_BLOCK, WIN_SPAN)
        p_w = _masked_softmax_rows(s_w, w_mask)
        o_w.append(jnp.dot(p_w.reshape(n_rows, WIN_SPAN).astype(BF16), win_ref[pl.ds(w0, WIN_SPAN), vcol(g)],
                           preferred_element_type=F32))

    def sel_step(j, carry):
        k0 = pl.multiple_of(j * SEL_CHUNK, SEL_CHUNK)
        kpos = k0 + lax.broadcasted_iota(jnp.int32, (1, SEL_CHUNK), 1)
        expand = (kpos // SEL_BLOCK == lax.broadcasted_iota(jnp.int32, (LANES, SEL_CHUNK), 0)).astype(BF16)
        causal = kpos <= tpos
        out = []
        for g in range(N_KV):
            mask = (jnp.dot(sel[g], expand, preferred_element_type=F32) > 0.5) & causal
            bias = jnp.where(mask, 0.0, NEG_INF)
            k_blk = sel_ref[pl.ds(k0, SEL_CHUNK), kcol(g)]
            v_blk = sel_ref[pl.ds(k0, SEL_CHUNK), vcol(g)]
            heads = []
            for h in range(Q_PER_KV):
                m, l, acc = carry[g][h]
                sm = _dot_nt(qs[g][rows(h)], k_blk) * SCALE + bias
                m_new = jnp.maximum(m, jnp.max(sm, axis=-1, keepdims=True))
                p = jnp.exp(sm - m_new)
                alpha = jnp.exp(m - m_new)
                l = alpha * l + jnp.sum(p, axis=-1, keepdims=True)
                acc = alpha * acc + jnp.dot(p.astype(BF16), v_blk, preferred_element_type=F32)
                heads.append((m_new, l, acc))
            out.append(tuple(heads))
        return tuple(out)

    n_steps = (q0 + Q_BLOCK + SEL_CHUNK - 1) // SEL_CHUNK
    head0 = (jnp.full((Q_BLOCK, 1), NEG_INF, F32), jnp.zeros((Q_BLOCK, 1), F32), jnp.zeros((Q_BLOCK, HEAD_DIM), F32))
    swept = lax.fori_loop(0, n_steps, sel_step, ((head0,) * Q_PER_KV,) * N_KV)

    gates = gate_ref[...]
    slabs = []
    for g in range(N_KV):
        for h in range(Q_PER_KV):
            o_s = swept[g][h][2] / swept[g][h][1]
            c = g * LANES + 3 * h
            slabs.append(gates[:, c:c + 1] * o_c[g][rows(h)] + gates[:, c + 1:c + 2] * o_s
                         + gates[:, c + 2:c + 3] * o_w[g][rows(h)])
    o = jnp.concatenate(slabs, axis=1)
    o_ref[...] = _rms(o, gain_ref[...]).astype(o_ref.dtype)


def _attn_prompt(q_b, k_cmp, v_cmp, sel_b, win_b, gates, gain, n_batch, seq_len):
    n_qb = seq_len // Q_BLOCK
    nc = seq_len // CMP_BLOCK
    qrow = lambda width: pl.BlockSpec((Q_BLOCK, width), lambda b, i: (b * n_qb + i, 0))
    cmp_spec = pl.BlockSpec((1, nc, KV_WIDTH), lambda b, i: (b, 0, 0))
    seq_spec = pl.BlockSpec((seq_len, ROW_WIDTH), lambda b, i: (b, 0))
    return pl.pallas_call(
        functools.partial(_attn_prompt_kernel, seq_len=seq_len),
        grid=(n_batch, n_qb),
        in_specs=[qrow(D_ATTN), cmp_spec, cmp_spec, seq_spec, seq_spec, qrow(GATE_WIDTH), _full((1, D_ATTN))],
        out_specs=qrow(D_ATTN),
        out_shape=jax.ShapeDtypeStruct((n_batch * seq_len, D_ATTN), BF16),
        compiler_params=_params(("parallel", "arbitrary")),
        name="attn_prompt",
    )(q_b, k_cmp, v_cmp, sel_b, win_b, gates, gain.reshape(1, -1))


def _unit_rows(refs, n_tokens, first_unit):
    cols = [jnp.concatenate([r[pl.ds(first_unit + g, n_tokens, stride=UNITS), :] for r in refs], axis=0)
            for g in range(N_KV)]
    return jnp.concatenate(cols, axis=1).astype(BF16)


def _own_group(x, row):
    return jnp.where(row < Q_PER_KV, x[:, :HEAD_DIM], x[:, HEAD_DIM:])


def _split_groups(q):
    row = lax.broadcasted_iota(jnp.int32, q.shape[:-1] + (1,), q.ndim - 2)
    zero = jnp.zeros_like(q)
    return jnp.concatenate([jnp.where(row < Q_PER_KV, q, zero), jnp.where(row < Q_PER_KV, zero, q)], axis=-1)


def _select_sample_kernel(q_ref, kc_ref, vc_ref, oc_ref, sel_ref, *, ns_past):
    q2 = _split_groups(q_ref[...])
    bt = q2.shape[0]
    s_c = jnp.einsum("bhk,bck->bhc", q2, kc_ref[...], preferred_element_type=F32) * SCALE
    m_c = jnp.max(s_c, axis=-1, keepdims=True)
    e_c = jnp.exp(s_c - m_c)
    p_c = e_c / jnp.sum(e_c, axis=-1, keepdims=True)
    o2 = jnp.einsum("bhc,bck->bhk", p_c.astype(BF16), vc_ref[...], preferred_element_type=F32)
    row = lax.broadcasted_iota(jnp.int32, (bt, N_HEADS, 1), 1)
    oc_ref[...] = jnp.where(row < Q_PER_KV, o2[..., :HEAD_DIM], o2[..., HEAD_DIM:])

    pair = p_c + jnp.where(row % 2 == 0, pltpu.roll(p_c, N_HEADS - 1, 1), pltpu.roll(p_c, 1, 1))
    p_grp = pair + jnp.where(row % Q_PER_KV < 2, pltpu.roll(pair, N_HEADS - 2, 1), pltpu.roll(pair, 2, 1))
    imp = (p_grp[..., :ns_past] + p_grp[..., ns_past:]).reshape(bt * N_HEADS, ns_past)

    sblk = lax.broadcasted_iota(jnp.int32, imp.shape, 1)
    score = jnp.where(sblk == 0, 1e30, imp)
    sel = _topk_mask(score, min(TOP_N, ns_past + 1) - 1).reshape(bt, N_HEADS, ns_past)
    sel_ref[...] = jnp.concatenate([sel, jnp.zeros((bt, N_HEADS, LANES - ns_past), F32)], axis=-1)


def _select_sample(q3, k_cmp, v_cmp):
    n_batch, nc_past, _ = k_cmp.shape
    ns_past = nc_past // (SEL_BLOCK // CMP_BLOCK)
    assert ns_past <= LANES
    bt = _row_tile(n_batch, 32, 1)
    per_b = lambda rows, width: pl.BlockSpec((bt, rows, width), lambda i: (i, 0, 0))
    return pl.pallas_call(
        functools.partial(_select_sample_kernel, ns_past=ns_past),
        grid=(n_batch // bt,),
        in_specs=[per_b(N_HEADS, HEAD_DIM), per_b(nc_past, KV_WIDTH), per_b(nc_past, KV_WIDTH)],
        out_specs=[per_b(N_HEADS, HEAD_DIM), per_b(N_HEADS, LANES)],
        out_shape=[jax.ShapeDtypeStruct((n_batch, N_HEADS, HEAD_DIM), F32),
                   jax.ShapeDtypeStruct((n_batch, N_HEADS, LANES), F32)],
        compiler_params=_params(("parallel",)),
        name="select_sample",
    )(q3, k_cmp, v_cmp)


def _attn_sample_kernel(pt_ref, *refs, n_pages, page, win_buf, past_len):
    del pt_ref
    q_ref, oc_ref, sel_ref, snew_ref, win_ref, wnew_ref, gate_ref, gain_ref = refs[:8]
    pages = refs[8:8 + n_pages]
    o_ref, nwin_ref = refs[8 + n_pages:]

    q = q_ref[0]
    row = lax.broadcasted_iota(jnp.int32, (N_HEADS, 1), 0)
    q2 = _split_groups(q)
    qf = q.astype(F32)
    o_c = oc_ref[0]

    kpos = lax.broadcasted_iota(jnp.int32, (1, past_len), 1)
    expand = (kpos // SEL_BLOCK == lax.broadcasted_iota(jnp.int32, (LANES, past_len), 0)).astype(BF16)
    chosen = jnp.dot(sel_ref[0].astype(BF16), expand, preferred_element_type=F32) > 0.5

    k_sel = _unit_rows(pages, page, 0)
    v_sel = _unit_rows(pages, page, N_KV)
    s_s = _dot_nt(q2, k_sel) * SCALE
    k_new = _own_group(snew_ref[0, :, :KV_WIDTH], row)
    v_new = _own_group(snew_ref[0, :, KV_WIDTH:], row)
    s_new = jnp.sum(qf * k_new, axis=-1, keepdims=True) * SCALE
    sm = jnp.where(chosen, s_s, NEG_INF)
    m_s = jnp.maximum(jnp.max(sm, axis=-1, keepdims=True), s_new)
    p_s = jnp.where(chosen, jnp.exp(sm - m_s), 0.0)
    p_new = jnp.exp(s_new - m_s)
    l_s = jnp.sum(p_s, axis=-1, keepdims=True) + p_new
    o_s = (_own_group(jnp.dot(p_s.astype(BF16), v_sel, preferred_element_type=F32), row) + p_new * v_new) / l_s

    k_win = _unit_rows([win_ref], win_buf, 0)
    v_win = _unit_rows([win_ref], win_buf, N_KV)
    wpos = past_len - win_buf + lax.broadcasted_iota(jnp.int32, (1, win_buf), 1)
    w_mask = ((past_len - wpos) <= WINDOW) & (wpos >= 0)
    s_w = jnp.where(w_mask, _dot_nt(q2, k_win) * SCALE, NEG_INF)
    kw_new = _own_group(wnew_ref[0, :, :KV_WIDTH], row)
    vw_new = _own_group(wnew_ref[0, :, KV_WIDTH:], row)
    sw_new = jnp.sum(qf * kw_new, axis=-1, keepdims=True) * SCALE
    m_w = jnp.maximum(jnp.max(s_w, axis=-1, keepdims=True), sw_new)
    p_w = jnp.where(w_mask, jnp.exp(s_w - m_w), 0.0)
    pw_new = jnp.exp(sw_new - m_w)
    l_w = jnp.sum(p_w, axis=-1, keepdims=True) + pw_new
    o_w = (_own_group(jnp.dot(p_w.astype(BF16), v_win, preferred_element_type=F32), row) + pw_new * vw_new) / l_w

    g_own = _own_group(gate_ref[0], row)
    lane = lax.broadcasted_iota(jnp.int32, (N_HEADS, LANES), 1)
    first = 3 * (row % Q_PER_KV)
    pick = lambda j: jnp.sum(jnp.where(lane == first + j, g_own, 0.0), axis=-1, keepdims=True)
    o = pick(0) * o_c + pick(1) * o_s + pick(2) * o_w
    ms = jnp.sum(jnp.sum(o * o, axis=-1, keepdims=True), axis=0, keepdims=True) / D_ATTN
    o_ref[0] = (o * lax.rsqrt(ms + EPS) * gain_ref[...]).astype(o_ref.dtype)

    kept = (win_buf - 1) * UNITS
    nwin_ref[0:kept, :] = win_ref[UNITS:win_buf * UNITS, :]
    for u in range(UNITS):
        nwin_ref[kept + u:kept + u + 1, :] = wnew_ref[0, :, u * HEAD_DIM:(u + 1) * HEAD_DIM]


def _attn_sample(q_b, k_cmp, v_cmp, sel_new, cache_sel, state_win, win_new, gates, gain, layer, page_table):
    depth, n_phys, page = cache_sel.shape[:3]
    n_batch, n_pages = page_table.shape
    win_buf = state_win.shape[2]
    past_len = n_pages * page
    q3 = q_b.reshape(n_batch, N_HEADS, HEAD_DIM)
    o_c, sel = _select_sample(q3, k_cmp, v_cmp)
    pages = cache_sel.reshape(depth * n_phys * page * UNITS, HEAD_DIM)
    wins = state_win.reshape(depth * n_batch * win_buf * UNITS, HEAD_DIM)
    base = layer * n_phys
    per_b = lambda shape: pl.BlockSpec((1,) + shape, lambda b, pt: (b, 0, 0))
    page_specs = [pl.BlockSpec((page * UNITS, HEAD_DIM), lambda b, pt, p=p: (base + pt[b, p], 0))
                  for p in range(n_pages)]
    in_specs = [per_b((N_HEADS, HEAD_DIM)), per_b((N_HEADS, HEAD_DIM)), per_b((N_HEADS, LANES)),
                per_b((1, ROW_WIDTH)),
                pl.BlockSpec((win_buf * UNITS, HEAD_DIM), lambda b, pt: (layer * n_batch + b, 0)),
                per_b((1, ROW_WIDTH)), per_b((1, GATE_WIDTH)),
                pl.BlockSpec((N_HEADS, HEAD_DIM), lambda b, pt: (0, 0))] + page_specs
    out, new_win = pl.pallas_call(
        functools.partial(_attn_sample_kernel, n_pages=n_pages, page=page, win_buf=win_buf, past_len=past_len),
        grid_spec=pltpu.PrefetchScalarGridSpec(
            num_scalar_prefetch=1,
            grid=(n_batch,),
            in_specs=in_specs,
            out_specs=[per_b((N_HEADS, HEAD_DIM)), pl.BlockSpec((win_buf * UNITS, HEAD_DIM), lambda b, pt: (b, 0))],
        ),
        out_shape=[jax.ShapeDtypeStruct((n_batch, N_HEADS, HEAD_DIM), BF16),
                   jax.ShapeDtypeStruct((n_batch * win_buf * UNITS, HEAD_DIM), F32)],
        compiler_params=_params(("arbitrary",)),
        name="attn_sample",
    )(page_table, q3, o_c, sel,
      sel_new.reshape(n_batch, 1, ROW_WIDTH), wins, win_new.reshape(n_batch, 1, ROW_WIDTH),
      gates.reshape(n_batch, 1, GATE_WIDTH), gain.reshape(N_HEADS, HEAD_DIM), *([pages] * n_pages))
    return out.reshape(n_batch, D_ATTN), new_win.reshape(n_batch, win_buf, 2, N_KV, HEAD_DIM)


def _lru_gates(xc, wa_ref, wx_ref, ba, bx, lam):
    width = xc.shape[1]
    bd = width // LRU_BLOCKS
    xb = xc.astype(BF16)
    ra = jnp.concatenate([jnp.dot(xb[:, n * bd:(n + 1) * bd], wa_ref[n], preferred_element_type=F32)
                          for n in range(LRU_BLOCKS)], axis=1)
    rx = jnp.concatenate([jnp.dot(xb[:, n * bd:(n + 1) * bd], wx_ref[n], preferred_element_type=F32)
                          for n in range(LRU_BLOCKS)], axis=1)
    r = jax.nn.sigmoid(ra + ba)
    i = jax.nn.sigmoid(rx + bx)
    log_a = -LRU_C * r * _softplus(-lam)
    a = jnp.exp(log_a)
    b = jnp.sqrt(-jnp.tanh(log_a) * (a * a + 1.0)) * (i * xc)
    return a, b


def _scan_rows(a8, b8, h):
    row = lax.broadcasted_iota(jnp.int32, a8.shape, 0)
    s = 1
    while s < SUBLANES:
        a_prev = jnp.where(row >= s, pltpu.roll(a8, s, 0), 1.0)
        b_prev = jnp.where(row >= s, pltpu.roll(b8, s, 0), 0.0)
        b8 = a8 * b_prev + b8
        a8 = a8 * a_prev
        s *= 2
    return a8 * h + b8


def _lru_prompt_kernel(ux_ref, uy_ref, cw_ref, cb_ref, wa_ref, wx_ref, ba_ref, bx_ref, lam_ref, gain_ref,
                       r_ref, h_ref, x_s, h_s, a_s, b_s):
    tt = ux_ref.shape[0]
    i = pl.program_id(1)

    @pl.when(i == 0)
    def _():
        x_s[0:SUBLANES, :] = jnp.zeros((SUBLANES, x_s.shape[1]), F32)
        h_s[...] = jnp.zeros_like(h_s)

    @pl.when(i > 0)
    def _():
        x_s[0:SUBLANES, :] = x_s[tt:tt + SUBLANES, :]

    x_s[SUBLANES:, :] = ux_ref[...]
    xc = cb_ref[...]
    for j in range(CONV_W):
        k = CONV_W - 1 - j
        xc = xc + x_s[SUBLANES - k:SUBLANES - k + tt, :] * cw_ref[j:j + 1, :]
    a, b = _lru_gates(xc, wa_ref, wx_ref, ba_ref[...], bx_ref[...], lam_ref[...])
    a_s[...] = a
    b_s[...] = b

    def group(k, h):
        r0 = pl.multiple_of(k * SUBLANES, SUBLANES)
        hs = _scan_rows(a_s[pl.ds(r0, SUBLANES), :], b_s[pl.ds(r0, SUBLANES), :], h)
        a_s[pl.ds(r0, SUBLANES), :] = hs
        return hs[SUBLANES - 1:SUBLANES, :]

    h_last = lax.fori_loop(0, tt // SUBLANES, group, h_s[...])
    h_s[...] = h_last
    h_ref[0] = h_last
    y = a_s[...] * jax.nn.gelu(uy_ref[...])
    r_ref[...] = _rms(y, gain_ref[...]).astype(r_ref.dtype)


def _lru_prompt(proj, n_batch, seq_len, weights):
    width = D_ATTN
    tt = _row_tile(seq_len, 512, SUBLANES)
    n_t = seq_len // tt
    rows = lambda col: pl.BlockSpec((tt, width), lambda b, i: (b * n_t + i, col))
    vec = _full((1, width))
    blk = _full((LRU_BLOCKS, width // LRU_BLOCKS, width // LRU_BLOCKS))
    return pl.pallas_call(
        _lru_prompt_kernel,
        grid=(n_batch, n_t),
        in_specs=[rows(COL_UX // width), rows(COL_UY // width), _full((CONV_W, width)), vec, blk, blk,
                  vec, vec, vec, vec],
        out_specs=[pl.BlockSpec((tt, width), lambda b, i: (b * n_t + i, 0)),
                   pl.BlockSpec((1, 1, width), lambda b, i: (b, 0, 0))],
        out_shape=[jax.ShapeDtypeStruct((n_batch * seq_len, width), BF16),
                   jax.ShapeDtypeStruct((n_batch, 1, width), F32)],
        scratch_shapes=[pltpu.VMEM((tt + SUBLANES, width), F32), pltpu.VMEM((1, width), F32),
                        pltpu.VMEM((tt, width), F32), pltpu.VMEM((tt, width), F32)],
        compiler_params=_params(("parallel", "arbitrary")),
        name="lru_prompt",
    )(proj, proj, *weights)


def _lru_sample_kernel(ux_ref, uy_ref, c0_ref, c1_ref, c2_ref, h0_ref, cw_ref, cb_ref, wa_ref, wx_ref,
                       ba_ref, bx_ref, lam_ref, gain_ref, r_ref, h_ref):
    xc = (cb_ref[...] + c0_ref[...] * cw_ref[0:1, :] + c1_ref[...] * cw_ref[1:2, :]
          + c2_ref[...] * cw_ref[2:3, :] + ux_ref[...] * cw_ref[3:4, :])
    a, b = _lru_gates(xc, wa_ref, wx_ref, ba_ref[...], bx_ref[...], lam_ref[...])
    h = a * h0_ref[...] + b
    h_ref[...] = h
    r_ref[...] = _rms(h * jax.nn.gelu(uy_ref[...]), gain_ref[...]).astype(r_ref.dtype)


def _lru_sample(ux, uy, conv_state, h0, weights):
    n, width = ux.shape
    mat = _full((n, width))
    vec = _full((1, width))
    blk = _full((LRU_BLOCKS, width // LRU_BLOCKS, width // LRU_BLOCKS))
    return pl.pallas_call(
        _lru_sample_kernel,
        grid=(1,),
        in_specs=[mat] * 6 + [_full((CONV_W, width)), vec, blk, blk, vec, vec, vec, vec],
        out_specs=[mat, mat],
        out_shape=[jax.ShapeDtypeStruct((n, width), BF16), jax.ShapeDtypeStruct((n, width), F32)],
        compiler_params=_params(("arbitrary",)),
        name="lru_sample",
    )(ux, uy, conv_state[:, 0], conv_state[:, 1], conv_state[:, 2], h0, *weights)


def _swiglu_step(x, wg_ref, wu_ref, wd_ref):
    g = jnp.dot(x, wg_ref[...].astype(BF16), preferred_element_type=F32)
    u = jnp.dot(x, wu_ref[...].astype(BF16), preferred_element_type=F32)
    hid = (jax.nn.silu(g) * u).astype(BF16)
    return jnp.dot(hid, wd_ref[...].astype(BF16), preferred_element_type=F32)


def _ffn_kernel(x_ref, wg_ref, wu_ref, wd_ref, res_ref, o_ref):
    @pl.when(pl.program_id(1) == 0)
    def _():
        o_ref[...] = res_ref[...]

    o_ref[...] += _swiglu_step(x_ref[...], wg_ref, wu_ref, wd_ref)


def _ffn(hn, wg, wu, wd, res):
    n, d = hn.shape
    d_ff = wg.shape[1]
    tm = _row_tile(n, 832)
    tf = _row_tile(d_ff, 256, LANES)
    row = pl.BlockSpec((tm, d), lambda i, f: (i, 0))
    return pl.pallas_call(
        _ffn_kernel,
        grid=(n // tm, d_ff // tf),
        in_specs=[row, pl.BlockSpec((d, tf), lambda i, f: (0, f)), pl.BlockSpec((d, tf), lambda i, f: (0, f)),
                  pl.BlockSpec((tf, d), lambda i, f: (f, 0)), row],
        out_specs=row,
        out_shape=jax.ShapeDtypeStruct((n, d), F32),
        compiler_params=_params(("parallel", "arbitrary")),
        name="ffn",
    )(hn, wg, wu, wd, res)


MOE_BLOCK = 1024
MOE_TILE = 256
COMBINE_TILE = 320


def _router_kernel(x_ref, g_ref, w_ref, o_ref, *, n_experts):
    hn = _rms(x_ref[...], g_ref[...]).astype(BF16)
    logits = jnp.dot(hn, w_ref[...], preferred_element_type=F32)
    lane = lax.broadcasted_iota(jnp.int32, logits.shape, 1).astype(F32)
    logits = jnp.where(lane < n_experts, logits, -jnp.inf)
    m1 = jnp.max(logits, axis=-1, keepdims=True)
    i1 = jnp.min(jnp.where(logits == m1, lane, float(LANES)), axis=-1, keepdims=True)
    rest = jnp.where(lane == i1, -jnp.inf, logits)
    m2 = jnp.max(rest, axis=-1, keepdims=True)
    i2 = jnp.min(jnp.where(rest == m2, lane, float(LANES)), axis=-1, keepdims=True)
    e2 = jnp.exp(m2 - m1)
    den = 1.0 + e2
    o_ref[...] = (jnp.where(lane == 0.0, 1.0 / den, 0.0) + jnp.where(lane == 1.0, e2 / den, 0.0)
                  + jnp.where(lane == 2.0, i1, 0.0) + jnp.where(lane == 3.0, i2, 0.0))


def _router(x, gain, w_router):
    n, d = x.shape
    n_experts = w_router.shape[1]
    tm = _row_tile(n, 832)
    w = jnp.concatenate([w_router, jnp.zeros((d, LANES - n_experts), w_router.dtype)], axis=1).astype(BF16)
    return pl.pallas_call(
        functools.partial(_router_kernel, n_experts=n_experts),
        grid=(n // tm,),
        in_specs=[pl.BlockSpec((tm, d), lambda i: (i, 0)), _full((1, d)), _full((d, LANES))],
        out_specs=pl.BlockSpec((tm, LANES), lambda i: (i, 0)),
        out_shape=jax.ShapeDtypeStruct((n, LANES), F32),
        compiler_params=_params(("parallel",)),
        name="router",
    )(x, gain.reshape(1, d), w)


def _start_row_gather(ids_ref, first, n_rows, src_hbm, dst, sem):
    def body(r, carry):
        pltpu.make_async_copy(src_hbm.at[pl.ds(ids_ref[first + r], 1)], dst.at[pl.ds(r, 1)], sem).start()
        return carry

    lax.fori_loop(0, n_rows, body, 0, unroll=8)


def _dispatch_kernel(ids_ref, on_ref, x_hbm, g_ref, o_ref, buf, sem):
    i = pl.program_id(0)
    slot = i % 2
    start = lambda tile, s: _start_row_gather(ids_ref, tile * MOE_TILE, MOE_TILE, x_hbm, buf.at[s], sem.at[s])

    @pl.when((i == 0) & (on_ref[0] > 0))
    def _():
        start(0, 0)

    @pl.when(i + 1 < pl.num_programs(0))
    def _():
        @pl.when(on_ref[i + 1] > 0)
        def _():
            start(i + 1, 1 - slot)

    @pl.when(on_ref[i] > 0)
    def _():
        pltpu.make_async_copy(buf.at[slot], buf.at[slot], sem.at[slot]).wait()
        o_ref[...] = _rms(buf[slot], g_ref[...]).astype(o_ref.dtype)

    @pl.when(on_ref[i] == 0)
    def _():
        o_ref[...] = jnp.zeros_like(o_ref)


def _dispatch(x, gain, row_ids, tile_on):
    n, d = x.shape
    n_slots = row_ids.shape[0]
    return pl.pallas_call(
        _dispatch_kernel,
        grid_spec=pltpu.PrefetchScalarGridSpec(
            num_scalar_prefetch=2,
            grid=(n_slots // MOE_TILE,),
            in_specs=[pl.BlockSpec(memory_space=pl.ANY), pl.BlockSpec((1, d), lambda i, ids, on: (0, 0))],
            out_specs=pl.BlockSpec((MOE_TILE, d), lambda i, ids, on: (i, 0)),
            scratch_shapes=[pltpu.VMEM((2, MOE_TILE, d), F32), pltpu.SemaphoreType.DMA((2,))],
        ),
        out_shape=jax.ShapeDtypeStruct((n_slots, d), BF16),
        compiler_params=_params(("arbitrary",)),
        name="moe_dispatch",
    )(row_ids, tile_on, x, gain.reshape(1, d))


def _experts_kernel(be_ref, br_ref, x_ref, w_ref, wg_ref, wu_ref, wd_ref, o_ref):
    b, f = pl.program_id(0), pl.program_id(1)
    rows = br_ref[b]

    @pl.when(f == 0)
    def _():
        o_ref[...] = jnp.zeros_like(o_ref)

    @pl.when(rows == MOE_BLOCK)
    def _():
        o_ref[...] += _swiglu_step(x_ref[...], wg_ref.at[0], wu_ref.at[0], wd_ref.at[0])

    @pl.when((rows > 0) & (rows < MOE_BLOCK))
    def _():
        for s in range(MOE_BLOCK // MOE_TILE):
            @pl.when(s * MOE_TILE < rows)
            def _():
                sub = pl.ds(s * MOE_TILE, MOE_TILE)
                o_ref[sub, :] += _swiglu_step(x_ref[sub, :], wg_ref.at[0], wu_ref.at[0], wd_ref.at[0])

    @pl.when((f == pl.num_programs(1) - 1) & (rows > 0))
    def _():
        o_ref[...] = o_ref[...] * w_ref[...]


def _experts(xs, slot_w, blk_expert, blk_rows, wg, wu, wd):
    n_slots, d = xs.shape
    d_ff = wg.shape[2]
    tf = _row_tile(d_ff, 256, LANES)
    n_f = d_ff // tf
    fcol = lambda b, f, br: jnp.where(br[b] > 0, f, n_f - 1)
    return pl.pallas_call(
        _experts_kernel,
        grid_spec=pltpu.PrefetchScalarGridSpec(
            num_scalar_prefetch=2,
            grid=(n_slots // MOE_BLOCK, n_f),
            in_specs=[pl.BlockSpec((MOE_BLOCK, d), lambda b, f, be, br: (b, 0)),
                      pl.BlockSpec((MOE_BLOCK, 1), lambda b, f, be, br: (b, 0)),
                      pl.BlockSpec((1, d, tf), lambda b, f, be, br: (be[b], 0, fcol(b, f, br))),
                      pl.BlockSpec((1, d, tf), lambda b, f, be, br: (be[b], 0, fcol(b, f, br))),
                      pl.BlockSpec((1, tf, d), lambda b, f, be, br: (be[b], fcol(b, f, br), 0))],
            out_specs=pl.BlockSpec((MOE_BLOCK, d), lambda b, f, be, br: (b, 0)),
        ),
        out_shape=jax.ShapeDtypeStruct((n_slots, d), F32),
        compiler_params=_params(("arbitrary", "arbitrary")),
        name="moe_experts",
    )(blk_expert, blk_rows, xs, slot_w, wg, wu, wd)


def _combine_kernel(s0_ref, s1_ref, y_hbm, x_ref, o_ref, buf, sem):
    tc = x_ref.shape[0]
    i = pl.program_id(0)
    slot = i % 2

    def start(tile, s):
        _start_row_gather(s0_ref, tile * tc, tc, y_hbm, buf.at[s, 0], sem.at[s])
        _start_row_gather(s1_ref, tile * tc, tc, y_hbm, buf.at[s, 1], sem.at[s])

    @pl.when(i == 0)
    def _():
        start(0, 0)

    @pl.when(i + 1 < pl.num_programs(0))
    def _():
        start(i + 1, 1 - slot)

    pltpu.make_async_copy(buf.at[slot], buf.at[slot], sem.at[slot]).wait()
    o_ref[...] = x_ref[...] + buf[slot, 0] + buf[slot, 1]


def _combine(x, y, slot0, slot1):
    n, d = x.shape
    tc = _row_tile(n, COMBINE_TILE, SUBLANES)
    row = pl.BlockSpec((tc, d), lambda i, s0, s1: (i, 0))
    return pl.pallas_call(
        _combine_kernel,
        grid_spec=pltpu.PrefetchScalarGridSpec(
            num_scalar_prefetch=2,
            grid=(n // tc,),
            in_specs=[pl.BlockSpec(memory_space=pl.ANY), row],
            out_specs=row,
            scratch_shapes=[pltpu.VMEM((2, 2, tc, d), F32), pltpu.SemaphoreType.DMA((2,))],
        ),
        out_shape=jax.ShapeDtypeStruct((n, d), F32),
        compiler_params=_params(("arbitrary",)),
        name="moe_combine",
    )(slot0, slot1, y, x)


def _routing_tables(route, n_experts):
    n = route.shape[0]
    expert = route[:, 2:4].astype(jnp.int32).T.reshape(-1)
    weight = route[:, 0:2].T.reshape(-1)
    token = jnp.tile(jnp.arange(n, dtype=jnp.int32), 2)
    onehot = (expert[:, None] == jnp.arange(n_experts, dtype=jnp.int32)).astype(jnp.int32)
    rank = jnp.sum((jnp.cumsum(onehot, axis=0) - 1) * onehot, axis=1)
    count = jnp.sum(onehot, axis=0)
    padded = (count + MOE_BLOCK - 1) // MOE_BLOCK * MOE_BLOCK
    ends = jnp.cumsum(padded)
    starts = ends - padded
    slot = starts[expert] + rank
    n_slots = (2 * n + n_experts * (MOE_BLOCK - 1) + MOE_BLOCK - 1) // MOE_BLOCK * MOE_BLOCK
    row_ids = jnp.zeros((n_slots,), jnp.int32).at[slot].set(token)
    slot_w = jnp.zeros((n_slots,), F32).at[slot].set(weight).reshape(n_slots, 1)
    blk_first = jnp.arange(n_slots // MOE_BLOCK, dtype=jnp.int32) * MOE_BLOCK
    blk_expert = jnp.minimum(jnp.sum((ends[None, :] <= blk_first[:, None]).astype(jnp.int32), axis=1), n_experts - 1)
    used = (count + MOE_TILE - 1) // MOE_TILE * MOE_TILE
    blk_rows = jnp.clip(starts[blk_expert] + used[blk_expert] - blk_first, 0, MOE_BLOCK)
    blk_rows = jnp.where(blk_first < ends[-1], blk_rows, 0).astype(jnp.int32)
    tile_first = jnp.arange(n_slots // MOE_TILE, dtype=jnp.int32) * MOE_TILE
    tile_on = (tile_first % MOE_BLOCK < jnp.repeat(blk_rows, MOE_BLOCK // MOE_TILE)).astype(jnp.int32)
    return row_ids, slot_w, blk_expert, blk_rows, tile_on, slot[:n], slot[n:]


def _moe(x, gain, w_router, wg, wu, wd):
    route = _router(x, gain, w_router)
    row_ids, slot_w, blk_expert, blk_rows, tile_on, slot0, slot1 = _routing_tables(route, w_router.shape[1])
    xs = _dispatch(x, gain, row_ids, tile_on)
    y = _experts(xs, slot_w, blk_expert, blk_rows, wg, wu, wd)
    return _combine(x, y, slot0, slot1)


def kernel(x_prompt, x_sample, cache_cmp, cache_sel, state_win, state_lru, state_conv, page_table, ln_mix, w_in, q_norm, k_norm, cmp_pe, cmp_w1, cmp_w2, conv_w, conv_b, lru_wa, lru_ba, lru_wx, lru_bx, lru_lambda, out_norm_attn, out_norm_lru, w_out, ln_ffn, ffn_w_gate, ffn_w_up, ffn_w_down, moe_router, moe_w_gate, moe_w_up, moe_w_down):
    n_batch, seq_len, d_model = x_prompt.shape
    dec_batch, dec_seq, _ = x_sample.shape
    depth = ln_mix.shape[0]
    page = cache_cmp.shape[2]
    n_pages = page_table.shape[1]
    past_len = n_pages * page
    win_buf = state_win.shape[2]
    lru_width = d_model - D_ATTN
    assert dec_seq == 1 and lru_width == D_ATTN
    assert seq_len % Q_BLOCK == 0 and seq_len >= WIN_SPAN and seq_len % SEL_CHUNK == 0 and seq_len >= win_buf
    assert past_len % Q_BLOCK == 0 and win_buf <= past_len and page % CMP_BLOCK == 0
    n_p = n_batch * seq_len
    nc = seq_len // CMP_BLOCK
    nc_past = past_len // CMP_BLOCK
    kv_row = (2, N_KV, HEAD_DIM)

    pos = jnp.concatenate([jnp.tile(jnp.arange(seq_len), n_batch), jnp.full((dec_batch,), past_len)])
    tables = _rope_tables(pos)
    end_p = _rope_tables((jnp.arange(nc) + 1) * CMP_BLOCK - 1)
    end_s = _rope_tables((jnp.arange(nc_past) + 1) * CMP_BLOCK - 1)

    x = jnp.concatenate([x_prompt.reshape(n_p, d_model), x_sample.reshape(dec_batch, d_model)], axis=0)
    outs = [[] for _ in range(10)]
    for l in range(depth):
        hn = _rmsnorm(x, ln_mix[l])
        proj = _matmul([hn], [_reorder_w_in(w_in[l])], tm_target=832, tn_target=1280)
        q_b, sel_f, sel_b, win_f, win_b, gates = _postproj(proj, tables, q_norm[l], k_norm[l, 1], k_norm[l, 2])
        c_rows = proj[:, COL_CMP:COL_CMP + ROW_WIDTH]

        cmp_w = (cmp_pe[l], cmp_w1[l].reshape(2, CMP_BLOCK, HEAD_DIM, HEAD_DIM).astype(BF16),
                 cmp_w2[l].astype(BF16), k_norm[l, 0].reshape(1, HEAD_DIM))
        lru_w = (conv_w[l], conv_b[l].reshape(1, -1), lru_wa[l].astype(BF16), lru_wx[l].astype(BF16),
                 lru_ba[l].reshape(1, -1), lru_bx[l].reshape(1, -1), lru_lambda[l].reshape(1, -1),
                 out_norm_lru[l].reshape(1, -1))

        kc_p, vc_p = _compress_prompt(c_rows[:n_p].reshape(n_batch * nc, CMP_ROW), n_batch, nc, cmp_w, end_p)
        o_p = _attn_prompt(q_b, _deinterleave_blocks(kc_p, n_batch, nc), _deinterleave_blocks(vc_p, n_batch, nc),
                           sel_b, win_b, gates, out_norm_attn[l], n_batch, seq_len)
        r_p, h_p = _lru_prompt(proj, n_batch, seq_len, lru_w)

        kc_s, vc_s = _compress_sample(cache_cmp, l, page_table, cmp_w, end_s, nb=2 if dec_batch % 2 == 0 else 1)
        o_s, new_win_s = _attn_sample(q_b[n_p:], _deinterleave_blocks(kc_s, dec_batch, nc_past),
                                      _deinterleave_blocks(vc_s, dec_batch, nc_past), sel_f[n_p:], cache_sel,
                                      state_win, win_f[n_p:], gates[n_p:], out_norm_attn[l], l, page_table)
        ux_s = proj[n_p:, COL_UX:COL_UX + lru_width]
        r_s, h_s = _lru_sample(ux_s, proj[n_p:, COL_UY:COL_UY + lru_width], state_conv[l], state_lru[l], lru_w)

        w_o = w_out[l].astype(BF16)
        x = _matmul([jnp.concatenate([o_p, o_s], axis=0), jnp.concatenate([r_p, r_s], axis=0)],
                    [w_o[:D_ATTN], w_o[D_ATTN:]], res=x, tm_target=832, tn_target=1024)

        k = l // 2
        if l % 2 == 0:
            x = _ffn(_rmsnorm(x, ln_ffn[l]), ffn_w_gate[k], ffn_w_up[k], ffn_w_down[k], x)
        else:
            x = _moe(x, ln_ffn[l], moe_router[k], moe_w_gate[k], moe_w_up[k], moe_w_down[k])

        ux_p = proj[:n_p, COL_UX:COL_UX + lru_width].reshape(n_batch, seq_len, lru_width)
        new = (c_rows[:n_p].reshape((n_batch, seq_len) + kv_row), c_rows[n_p:].reshape((dec_batch, 1) + kv_row),
               sel_f[:n_p].reshape((n_batch, seq_len) + kv_row), sel_f[n_p:].reshape((dec_batch, 1) + kv_row),
               win_f[:n_p].reshape((n_batch, seq_len) + kv_row)[:, seq_len - win_buf:],
               new_win_s,
               h_p.reshape(n_batch, lru_width), h_s,
               ux_p[:, seq_len - (CONV_W - 1):],
               jnp.concatenate([state_conv[l][:, 1:], ux_s[:, None, :]], axis=1))
        for acc, val in zip(outs, new):
            acc.append(val)

    return (x[:n_p].reshape(n_batch, seq_len, d_model), x[n_p:].reshape(dec_batch, 1, d_model),
            *[jnp.stack(v) for v in outs])
```

```python
import functools

import jax
import jax.numpy as jnp
from jax import lax
from jax.experimental import pallas as pl
from jax.experimental.pallas import tpu as pltpu

N_HEADS = 8
HEAD_DIM = 128
N_KV = 2
Q_PER_KV = N_HEADS // N_KV
D_ATTN = N_HEADS * HEAD_DIM
KV_WIDTH = N_KV * HEAD_DIM
ROW_WIDTH = 2 * KV_WIDTH
UNITS = 2 * N_KV
ROT_DIM = HEAD_DIM // 4
ROPE_THETA = 500000.0
CMP_BLOCK = 32
SEL_BLOCK = 64
TOP_N = 16
WINDOW = 512
Q_BLOCK = 128
LRU_BLOCKS = 8
CONV_W = 4
LRU_C = 8.0
TOP_K = 2
EPS = 1e-6
SCALE = HEAD_DIM ** -0.5
NEG_INF = -1e30

LANES = 128
SUBLANES = 8
VMEM_LIMIT = 56 * 1024 * 1024

F32 = jnp.float32
BF16 = jnp.bfloat16


def _row_tile(n, target, mult=16):
    best = None
    for t in range(mult, min(n, target) + 1, mult):
        if n % t == 0:
            best = t
    return best if best is not None else n


def _params(semantics):
    return pltpu.CompilerParams(dimension_semantics=semantics, vmem_limit_bytes=VMEM_LIMIT)


def _full(shape):
    zeros = (0,) * len(shape)
    return pl.BlockSpec(shape, lambda *_: zeros)


def _rms(x, gain):
    return x * lax.rsqrt(jnp.mean(x * x, axis=-1, keepdims=True) + EPS) * gain


def _rope_apply(y, cos_t, sin_lo, sin_hi):
    half = ROT_DIM // 2
    return y * cos_t + pltpu.roll(y, HEAD_DIM - half, 1) * sin_lo + pltpu.roll(y, half, 1) * sin_hi


def _softplus(x):
    return jnp.maximum(x, 0.0) + jnp.log1p(jnp.exp(-jnp.abs(x)))


def _topk_mask(score, n_pick):
    n_lanes = score.shape[-1]
    lane = lax.broadcasted_iota(jnp.int32, score.shape, 1).astype(F32)
    sel = jnp.zeros(score.shape, F32)
    for _ in range(n_pick):
        m = jnp.max(score, axis=-1, keepdims=True)
        idx = jnp.min(jnp.where(score == m, lane, float(n_lanes)), axis=-1, keepdims=True)
        pick = lane == idx
        sel = jnp.where(pick, 1.0, sel)
        score = jnp.where(pick, -2.0, score)
    return sel


def _topk_mask_cols(score, n_pick):
    n_rows = score.shape[0]
    row = lax.broadcasted_iota(jnp.int32, score.shape, 0).astype(F32)
    sel = jnp.zeros(score.shape, F32)
    for _ in range(n_pick):
        m = jnp.max(score, axis=0, keepdims=True)
        idx = jnp.min(jnp.where(score == m, row, float(n_rows)), axis=0, keepdims=True)
        pick = row == idx
        sel = jnp.where(pick, 1.0, sel)
        score = jnp.where(pick, -2.0, score)
    return sel


def _masked_softmax_rows(s, mask):
    sm = jnp.where(mask, s, NEG_INF)
    m = jnp.max(sm, axis=-1, keepdims=True)
    e = jnp.where(mask, jnp.exp(sm - m), 0.0)
    l = jnp.sum(e, axis=-1, keepdims=True)
    return e / jnp.where(l > 0.0, l, 1.0)


def _dot_nt(a, b):
    return lax.dot_general(a, b, (((1,), (1,)), ((), ())), preferred_element_type=F32)


def _rmsnorm_kernel(x_ref, g_ref, o_ref):
    o_ref[...] = _rms(x_ref[...], g_ref[...]).astype(o_ref.dtype)


def _rmsnorm(x, gain):
    n, d = x.shape
    tm = _row_tile(n, 832)
    return pl.pallas_call(
        _rmsnorm_kernel,
        grid=(n // tm,),
        in_specs=[pl.BlockSpec((tm, d), lambda i: (i, 0)), _full((1, d))],
        out_specs=pl.BlockSpec((tm, d), lambda i: (i, 0)),
        out_shape=jax.ShapeDtypeStruct((n, d), BF16),
        compiler_params=_params(("parallel",)),
        name="rmsnorm",
    )(x, gain.reshape(1, d))


def _matmul_kernel(*refs, n_lhs, has_res):
    o_ref = refs[-1]
    acc = jnp.dot(refs[0][...], refs[n_lhs][...], preferred_element_type=F32)
    for k in range(1, n_lhs):
        acc = acc + jnp.dot(refs[k][...], refs[n_lhs + k][...], preferred_element_type=F32)
    if has_res:
        acc = acc + refs[2 * n_lhs][...]
    o_ref[...] = acc


def _matmul(xs, ws, res=None, *, tm_target, tn_target):
    n = xs[0].shape[0]
    d_out = ws[0].shape[1]
    tm = _row_tile(n, tm_target)
    tn = _row_tile(d_out, tn_target, LANES)
    in_specs = [pl.BlockSpec((tm, x.shape[1]), lambda i, j: (i, 0)) for x in xs]
    in_specs += [pl.BlockSpec((w.shape[0], tn), lambda i, j: (0, j)) for w in ws]
    args = list(xs) + list(ws)
    if res is not None:
        in_specs.append(pl.BlockSpec((tm, tn), lambda i, j: (i, j)))
        args.append(res)
    return pl.pallas_call(
        functools.partial(_matmul_kernel, n_lhs=len(xs), has_res=res is not None),
        grid=(n // tm, d_out // tn),
        in_specs=in_specs,
        out_specs=pl.BlockSpec((tm, tn), lambda i, j: (i, j)),
        out_shape=jax.ShapeDtypeStruct((n, d_out), F32),
        compiler_params=_params(("parallel", "parallel")),
        name="matmul",
    )(*args)


PROJ_WIDTH = 5120
COL_UX, COL_UY, COL_CMP, COL_SEL, COL_WIN, COL_GATE = 1024, 2048, 3072, 3584, 4096, 4608
GATE_WIDTH = 2 * LANES


def _reorder_w_in(w):
    lru = D_ATTN
    c_kv = D_ATTN
    c_g = c_kv + 3 * ROW_WIDTH
    c_ux = c_g + 3 * N_HEADS
    c_uy = c_ux + lru
    n_g = 3 * Q_PER_KV
    zpad = lambda k: jnp.zeros((w.shape[0], k), w.dtype)
    parts = [w[:, :D_ATTN], w[:, c_ux:c_uy], w[:, c_uy:c_uy + lru], w[:, c_kv:c_g],
             w[:, c_g:c_g + n_g], zpad(LANES - n_g), w[:, c_g + n_g:c_ux], zpad(LANES - n_g)]
    width = sum(p.shape[1] for p in parts)
    parts.append(zpad(PROJ_WIDTH - width))
    return jnp.concatenate(parts, axis=1).astype(BF16)


def _postproj_kernel(q_ref, sel_ref, win_ref, gate_ref, cos_ref, slo_ref, shi_ref,
                     qg_ref, ksg_ref, kwg_ref,
                     qo_ref, sel_f_ref, sel_b_ref, win_f_ref, win_b_ref, gate_o_ref):
    cos_t, slo, shi = cos_ref[...], slo_ref[...], shi_ref[...]
    for h in range(N_HEADS):
        cols = slice(h * HEAD_DIM, (h + 1) * HEAD_DIM)
        y = _rope_apply(_rms(q_ref[:, cols], qg_ref[...]), cos_t, slo, shi)
        qo_ref[:, cols] = y.astype(qo_ref.dtype)
    for src, gain_ref, out_f, out_b in ((sel_ref, ksg_ref, sel_f_ref, sel_b_ref),
                                        (win_ref, kwg_ref, win_f_ref, win_b_ref)):
        for g in range(N_KV):
            cols = slice(g * HEAD_DIM, (g + 1) * HEAD_DIM)
            y = _rope_apply(_rms(src[:, cols], gain_ref[...]), cos_t, slo, shi)
            out_f[:, cols] = y
            out_b[:, cols] = y.astype(out_b.dtype)
        v = src[:, KV_WIDTH:]
        out_f[:, KV_WIDTH:] = v
        out_b[:, KV_WIDTH:] = v.astype(out_b.dtype)
    gate_o_ref[...] = jax.nn.sigmoid(gate_ref[...])


def _postproj(proj, tables, q_gain, ks_gain, kw_gain):
    n = proj.shape[0]
    tm = _row_tile(n, 416)
    rows = lambda width, col: pl.BlockSpec((tm, width), lambda i: (i, col))
    tab = pl.BlockSpec((tm, HEAD_DIM), lambda i: (i, 0))
    gain = _full((1, HEAD_DIM))
    out = lambda width: pl.BlockSpec((tm, width), lambda i: (i, 0))
    sds = jax.ShapeDtypeStruct
    return pl.pallas_call(
        _postproj_kernel,
        grid=(n // tm,),
        in_specs=[rows(D_ATTN, 0), rows(ROW_WIDTH, COL_SEL // ROW_WIDTH), rows(ROW_WIDTH, COL_WIN // ROW_WIDTH),
                  rows(GATE_WIDTH, COL_GATE // GATE_WIDTH), tab, tab, tab, gain, gain, gain],
        out_specs=[out(D_ATTN), out(ROW_WIDTH), out(ROW_WIDTH), out(ROW_WIDTH), out(ROW_WIDTH), out(GATE_WIDTH)],
        out_shape=[sds((n, D_ATTN), BF16), sds((n, ROW_WIDTH), F32), sds((n, ROW_WIDTH), BF16),
                   sds((n, ROW_WIDTH), F32), sds((n, ROW_WIDTH), BF16), sds((n, GATE_WIDTH), F32)],
        compiler_params=_params(("parallel",)),
        name="postproj",
    )(proj, proj, proj, proj, *tables, q_gain.reshape(1, -1), ks_gain.reshape(1, -1), kw_gain.reshape(1, -1))


def _rope_tables(pos):
    half = ROT_DIM // 2
    inv_freq = ROPE_THETA ** (-2.0 * jnp.arange(half, dtype=F32) / ROT_DIM)
    ang = pos.astype(F32)[:, None] * inv_freq
    cos, sin = jnp.cos(ang), jnp.sin(ang)
    n = pos.shape[0]
    ones = jnp.ones((n, HEAD_DIM - ROT_DIM), F32)
    zeros = jnp.zeros((n, HEAD_DIM - half), F32)
    cos_t = jnp.concatenate([cos, cos, ones], axis=1)
    sin_lo = jnp.concatenate([-sin, zeros], axis=1)
    sin_hi = jnp.concatenate([jnp.zeros((n, half), F32), sin, jnp.zeros((n, HEAD_DIM - ROT_DIM), F32)], axis=1)
    return cos_t, sin_lo, sin_hi


CMP_ROW = CMP_BLOCK * ROW_WIDTH


def _compress_rows(piece, n_rows, w1_ref, w2_ref, gain_ref, cos_ref, slo_ref, shi_ref, k_ref, v_ref):
    for n in range(2):
        acc = jnp.zeros((N_KV * n_rows, HEAD_DIM), F32)
        for tp in range(CMP_BLOCK // 2):
            lhs = jnp.concatenate([jnp.concatenate([piece(n, 2 * tp, g), piece(n, 2 * tp + 1, g)], axis=1)
                                   for g in range(N_KV)], axis=0)
            acc = acc + jnp.dot(lhs, w1_ref[n, tp], preferred_element_type=F32)
        hid = jax.nn.gelu(acc).astype(BF16)
        comp = jnp.dot(hid, w2_ref[n], preferred_element_type=F32)
        for g in range(N_KV):
            part = comp[g * n_rows:(g + 1) * n_rows]
            cols = slice(g * HEAD_DIM, (g + 1) * HEAD_DIM)
            if n == 0:
                part = _rope_apply(_rms(part, gain_ref[...]), cos_ref[...], slo_ref[...], shi_ref[...])
                k_ref[:, cols] = part.astype(k_ref.dtype)
            else:
                v_ref[:, cols] = part.astype(v_ref.dtype)


def _compress_prompt_kernel(x_ref, pe_ref, *rest):
    def piece(n, t, g):
        c0 = t * ROW_WIDTH + n * KV_WIDTH + g * HEAD_DIM
        return (x_ref[:, c0:c0 + HEAD_DIM] + pe_ref[n, t:t + 1, :]).astype(BF16)

    _compress_rows(piece, x_ref.shape[0], *rest)


def _compress_sample_kernel(pt_ref, *refs, n_page_refs, page):
    del pt_ref
    pages = refs[:n_page_refs]
    pe_ref = refs[n_page_refs]
    rest = refs[n_page_refs + 1:-1]
    y_s = refs[-1]
    n_tok = 2 * page
    out_row = lax.broadcasted_iota(jnp.int32, (n_tok, n_tok), 0)
    src_tok = lax.broadcasted_iota(jnp.int32, (n_tok, n_tok), 1)
    regroup = ((out_row % SUBLANES) * CMP_BLOCK + out_row // SUBLANES == src_tok).astype(BF16)
    pe_tok = [jnp.concatenate([jnp.concatenate([pe_ref[n]] * N_KV, axis=1)] * (n_tok // CMP_BLOCK), axis=0)
              for n in range(2)]
    for pair in range(n_page_refs // 2):
        for n in range(2):
            tok = jnp.concatenate([jnp.concatenate([pages[2 * pair + k][pl.ds(n * N_KV + g, page, stride=UNITS), :]
                                                    for g in range(N_KV)], axis=1) for k in range(2)], axis=0)
            y = jnp.dot(regroup, (tok + pe_tok[n]).astype(BF16), preferred_element_type=F32)
            y_s[n, :, pair * SUBLANES:(pair + 1) * SUBLANES, :] = y.reshape(CMP_BLOCK, SUBLANES, KV_WIDTH)

    _compress_rows(lambda n, t, g: y_s[n, t, :, g * HEAD_DIM:(g + 1) * HEAD_DIM].astype(BF16), y_s.shape[2], *rest)


def _compress_weight_specs():
    return [_full((2, CMP_BLOCK, HEAD_DIM)), _full((2, CMP_BLOCK // 2, 2 * HEAD_DIM, HEAD_DIM)),
            _full((2, HEAD_DIM, HEAD_DIM)), _full((1, HEAD_DIM))]


def _compress_prompt(c_rows, n_batch, nc, weights, end_tables):
    tr = _row_tile(nc, 128)
    per_b = nc // tr
    tab = pl.BlockSpec((tr, HEAD_DIM), lambda i: (i % per_b, 0))
    out = pl.BlockSpec((tr, KV_WIDTH), lambda i: (i, 0))
    sds = jax.ShapeDtypeStruct((n_batch * nc, KV_WIDTH), BF16)
    return pl.pallas_call(
        _compress_prompt_kernel,
        grid=(n_batch * per_b,),
        in_specs=[pl.BlockSpec((tr, CMP_ROW), lambda i: (i, 0))] + _compress_weight_specs() + [tab, tab, tab],
        out_specs=[out, out],
        out_shape=[sds, sds],
        compiler_params=_params(("parallel",)),
        name="compress_prompt",
    )(c_rows, *weights, *end_tables)


def _compress_sample(cache, layer, page_table, weights, end_tables, nb):
    depth, n_phys, page = cache.shape[:3]
    n_batch, n_pages = page_table.shape
    assert n_pages % 2 == 0 and 2 * page == SUBLANES * CMP_BLOCK
    nc = n_pages * page // CMP_BLOCK
    pages = cache.reshape(depth * n_phys * page * UNITS, HEAD_DIM)
    base = layer * n_phys
    page_specs = [pl.BlockSpec((page * UNITS, HEAD_DIM), lambda i, pt, bb=bb, p=p: (base + pt[i * nb + bb, p], 0))
                  for bb in range(nb) for p in range(n_pages)]
    n_rows = nb * nc
    tab = pl.BlockSpec((n_rows, HEAD_DIM), lambda i, pt: (0, 0))
    wspecs = [pl.BlockSpec(s.block_shape, lambda i, pt, z=(0,) * len(s.block_shape): z) for s in _compress_weight_specs()]
    out = pl.BlockSpec((n_rows, KV_WIDTH), lambda i, pt: (i, 0))
    sds = jax.ShapeDtypeStruct((n_batch * nc, KV_WIDTH), BF16)
    tables = [jnp.tile(t, (nb, 1)) for t in end_tables]
    return pl.pallas_call(
        functools.partial(_compress_sample_kernel, n_page_refs=nb * n_pages, page=page),
        grid_spec=pltpu.PrefetchScalarGridSpec(
            num_scalar_prefetch=1,
            grid=(n_batch // nb,),
            in_specs=page_specs + wspecs + [tab, tab, tab],
            out_specs=[out, out],
            scratch_shapes=[pltpu.VMEM((2, CMP_BLOCK, n_rows, KV_WIDTH), F32)],
        ),
        out_shape=[sds, sds],
        compiler_params=_params(("arbitrary",)),
        name="compress_sample",
    )(page_table, *([pages] * (nb * n_pages)), *weights, *tables)


def _deinterleave_blocks(x, n_batch, nc):
    x = x.reshape(n_batch, nc, -1)
    return jnp.concatenate([x[:, 0::2], x[:, 1::2]], axis=1)


SEL_CHUNK = 256
WIN_SPAN = WINDOW + Q_BLOCK


def _attn_prompt_kernel(q_ref, kc_ref, vc_ref, sel_ref, win_ref, gate_ref, gain_ref, o_ref, *, seq_len):
    nc = seq_len // CMP_BLOCK
    ns = seq_len // SEL_BLOCK
    i = pl.program_id(1)
    q0 = i * Q_BLOCK
    tpos = q0 + lax.broadcasted_iota(jnp.int32, (Q_BLOCK, 1), 0)
    tlane = q0 + lax.broadcasted_iota(jnp.int32, (1, Q_BLOCK), 1)
    kcol = lambda g: slice(g * HEAD_DIM, (g + 1) * HEAD_DIM)
    vcol = lambda g: slice(KV_WIDTH + g * HEAD_DIM, KV_WIDTH + (g + 1) * HEAD_DIM)
    rows = lambda h: slice(h * Q_BLOCK, (h + 1) * Q_BLOCK)
    n_rows = Q_PER_KV * Q_BLOCK

    qs, o_c, o_w, sel = [], [], [], []
    for g in range(N_KV):
        q_g = jnp.concatenate([q_ref[:, (g * Q_PER_KV + h) * HEAD_DIM:(g * Q_PER_KV + h + 1) * HEAD_DIM]
                               for h in range(Q_PER_KV)], axis=0)
        qs.append(q_g)

        col = lax.broadcasted_iota(jnp.int32, (Q_BLOCK, nc), 1)
        blk = jnp.where(col < ns, 2 * col, 2 * (col - ns) + 1)
        c_mask = ((blk + 1) * CMP_BLOCK - 1) <= tpos
        s_c = (_dot_nt(q_g, kc_ref[0, :, kcol(g)]) * SCALE).reshape(Q_PER_KV, Q_BLOCK, nc)
        p_c = _masked_softmax_rows(s_c, c_mask[None])
        o_c.append(jnp.dot(p_c.reshape(n_rows, nc).astype(BF16), vc_ref[0, :, kcol(g)], preferred_element_type=F32))
        p_tok = p_c[0] + p_c[1] + p_c[2] + p_c[3]

        p_blk = p_tok.T
        imp = p_blk[:ns] + p_blk[ns:]
        sblk = lax.broadcasted_iota(jnp.int32, (ns, Q_BLOCK), 0)
        allowed = sblk * SEL_BLOCK <= tlane
        forced = (sblk == 0) | (sblk == tlane // SEL_BLOCK)
        score = jnp.where(forced, 1e30, jnp.where(allowed, imp, -1.0))
        picked = _topk_mask_cols(score, min(TOP_N, ns))
        picked = jnp.concatenate([picked, jnp.zeros((LANES - ns, Q_BLOCK), F32)], axis=0)
        sel.append(picked.T.astype(BF16))

        w0 = pl.multiple_of(jnp.maximum(q0 - WINDOW, 0), Q_BLOCK)
        wpos = w0 + lax.broadcasted_iota(jnp.int32, (1, WIN_SPAN), 1)
        dist = tpos - wpos
        w_mask = ((dist >= 0) & (dist <= WINDOW))[None]
        s_w = (_dot_nt(q_g, win_ref[pl.ds(w0, WIN_SPAN), kcol(g)]) * SCALE).reshape(Q_PER_KV, Q_BLOCK, WIN_SPAN)
        p_w = _masked_softmax_rows(s_w, w_mask)
        o_w.append(jnp.dot(p_w.reshape(n_rows, WIN_SPAN).astype(BF16), win_ref[pl.ds(w0, WIN_SPAN), vcol(g)],
                           preferred_element_type=F32))

    def sel_step(j, carry):
        k0 = pl.multiple_of(j * SEL_CHUNK, SEL_CHUNK)
        kpos = k0 + lax.broadcasted_iota(jnp.int32, (1, SEL_CHUNK), 1)
        expand = (kpos // SEL_BLOCK == lax.broadcasted_iota(jnp.int32, (LANES, SEL_CHUNK), 0)).astype(BF16)
        causal = kpos <= tpos
        out = []
        for g in range(N_KV):
            mask = (jnp.dot(sel[g], expand, preferred_element_type=F32) > 0.5) & causal
            bias = jnp.where(mask, 0.0, NEG_INF)
            k_blk = sel_ref[pl.ds(k0, SEL_CHUNK), kcol(g)]
            v_blk = sel_ref[pl.ds(k0, SEL_CHUNK), vcol(g)]
            heads = []
            for h in range(Q_PER_KV):
                m, l, acc = carry[g][h]
                sm = _dot_nt(qs[g][rows(h)], k_blk) * SCALE + bias
                m_new = jnp.maximum(m, jnp.max(sm, axis=-1, keepdims=True))
                p = jnp.exp(sm - m_new)
                alpha = jnp.exp(m - m_new)
                l = alpha * l + jnp.sum(p, axis=-1, keepdims=True)
                acc = alpha * acc + jnp.dot(p.astype(BF16), v_blk, preferred_element_type=F32)
                heads.append((m_new, l, acc))
            out.append(tuple(heads))
        return tuple(out)

    n_steps = (q0 + Q_BLOCK + SEL_CHUNK - 1) // SEL_CHUNK
    head0 = (jnp.full((Q_BLOCK, 1), NEG_INF, F32), jnp.zeros((Q_BLOCK, 1), F32), jnp.zeros((Q_BLOCK, HEAD_DIM), F32))
    swept = lax.fori_loop(0, n_steps, sel_step, ((head0,) * Q_PER_KV,) * N_KV)

    gates = gate_ref[...]
    slabs = []
    for g in range(N_KV):
        for h in range(Q_PER_KV):
            o_s = swept[g][h][2] / swept[g][h][1]
            c = g * LANES + 3 * h
            slabs.append(gates[:, c:c + 1] * o_c[g][rows(h)] + gates[:, c + 1:c + 2] * o_s
                         + gates[:, c + 2:c + 3] * o_w[g][rows(h)])
    o = jnp.concatenate(slabs, axis=1)
    o_ref[...] = _rms(o, gain_ref[...]).astype(o_ref.dtype)


def _attn_prompt(q_b, k_cmp, v_cmp, sel_b, win_b, gates, gain, n_batch, seq_len):
    n_qb = seq_len // Q_BLOCK
    nc = seq_len // CMP_BLOCK
    qrow = lambda width: pl.BlockSpec((Q_BLOCK, width), lambda b, i: (b * n_qb + i, 0))
    cmp_spec = pl.BlockSpec((1, nc, KV_WIDTH), lambda b, i: (b, 0, 0))
    seq_spec = pl.BlockSpec((seq_len, ROW_WIDTH), lambda b, i: (b, 0))
    return pl.pallas_call(
        functools.partial(_attn_prompt_kernel, seq_len=seq_len),
        grid=(n_batch, n_qb),
        in_specs=[qrow(D_ATTN), cmp_spec, cmp_spec, seq_spec, seq_spec, qrow(GATE_WIDTH), _full((1, D_ATTN))],
        out_specs=qrow(D_ATTN),
        out_shape=jax.ShapeDtypeStruct((n_batch * seq_len, D_ATTN), BF16),
        compiler_params=_params(("parallel", "arbitrary")),
        name="attn_prompt",
    )(q_b, k_cmp, v_cmp, sel_b, win_b, gates, gain.reshape(1, -1))


def _unit_rows(refs, n_tokens, first_unit):
    cols = [jnp.concatenate([r[pl.ds(first_unit + g, n_tokens, stride=UNITS), :] for r in refs], axis=0)
            for g in range(N_KV)]
    return jnp.concatenate(cols, axis=1).astype(BF16)


def _own_group(x, row):
    return jnp.where(row < Q_PER_KV, x[:, :HEAD_DIM], x[:, HEAD_DIM:])


def _split_groups(q):
    row = lax.broadcasted_iota(jnp.int32, q.shape[:-1] + (1,), q.ndim - 2)
    zero = jnp.zeros_like(q)
    return jnp.concatenate([jnp.where(row < Q_PER_KV, q, zero), jnp.where(row < Q_PER_KV, zero, q)], axis=-1)


def _select_sample_kernel(q_ref, kc_ref, vc_ref, oc_ref, sel_ref, *, ns_past):
    q2 = _split_groups(q_ref[...])
    bt = q2.shape[0]
    s_c = jnp.einsum("bhk,bck->bhc", q2, kc_ref[...], preferred_element_type=F32) * SCALE
    m_c = jnp.max(s_c, axis=-1, keepdims=True)
    e_c = jnp.exp(s_c - m_c)
    p_c = e_c / jnp.sum(e_c, axis=-1, keepdims=True)
    o2 = jnp.einsum("bhc,bck->bhk", p_c.astype(BF16), vc_ref[...], preferred_element_type=F32)
    row = lax.broadcasted_iota(jnp.int32, (bt, N_HEADS, 1), 1)
    oc_ref[...] = jnp.where(row < Q_PER_KV, o2[..., :HEAD_DIM], o2[..., HEAD_DIM:])

    pair = p_c + jnp.where(row % 2 == 0, pltpu.roll(p_c, N_HEADS - 1, 1), pltpu.roll(p_c, 1, 1))
    p_grp = pair + jnp.where(row % Q_PER_KV < 2, pltpu.roll(pair, N_HEADS - 2, 1), pltpu.roll(pair, 2, 1))
    imp = (p_grp[..., :ns_past] + p_grp[..., ns_past:]).reshape(bt * N_HEADS, ns_past)

    sblk = lax.broadcasted_iota(jnp.int32, imp.shape, 1)
    score = jnp.where(sblk == 0, 1e30, imp)
    sel = _topk_mask(score, min(TOP_N, ns_past + 1) - 1).reshape(bt, N_HEADS, ns_past)
    sel_ref[...] = jnp.concatenate([sel, jnp.zeros((bt, N_HEADS, LANES - ns_past), F32)], axis=-1)


def _select_sample(q3, k_cmp, v_cmp):
    n_batch, nc_past, _ = k_cmp.shape
    ns_past = nc_past // (SEL_BLOCK // CMP_BLOCK)
    assert ns_past <= LANES
    bt = _row_tile(n_batch, 32, 1)
    per_b = lambda rows, width: pl.BlockSpec((bt, rows, width), lambda i: (i, 0, 0))
    return pl.pallas_call(
        functools.partial(_select_sample_kernel, ns_past=ns_past),
        grid=(n_batch // bt,),
        in_specs=[per_b(N_HEADS, HEAD_DIM), per_b(nc_past, KV_WIDTH), per_b(nc_past, KV_WIDTH)],
        out_specs=[per_b(N_HEADS, HEAD_DIM), per_b(N_HEADS, LANES)],
        out_shape=[jax.ShapeDtypeStruct((n_batch, N_HEADS, HEAD_DIM), F32),
                   jax.ShapeDtypeStruct((n_batch, N_HEADS, LANES), F32)],
        compiler_params=_params(("parallel",)),
        name="select_sample",
    )(q3, k_cmp, v_cmp)


def _attn_sample_kernel(pt_ref, *refs, n_pages, page, win_buf, past_len):
    del pt_ref
    q_ref, oc_ref, sel_ref, snew_ref, win_ref, wnew_ref, gate_ref, gain_ref = refs[:8]
    pages = refs[8:8 + n_pages]
    o_ref, nwin_ref = refs[8 + n_pages:]

    q = q_ref[0]
    row = lax.broadcasted_iota(jnp.int32, (N_HEADS, 1), 0)
    q2 = _split_groups(q)
    qf = q.astype(F32)
    o_c = oc_ref[0]

    kpos = lax.broadcasted_iota(jnp.int32, (1, past_len), 1)
    expand = (kpos // SEL_BLOCK == lax.broadcasted_iota(jnp.int32, (LANES, past_len), 0)).astype(BF16)
    chosen = jnp.dot(sel_ref[0].astype(BF16), expand, preferred_element_type=F32) > 0.5

    k_sel = _unit_rows(pages, page, 0)
    v_sel = _unit_rows(pages, page, N_KV)
    s_s = _dot_nt(q2, k_sel) * SCALE
    k_new = _own_group(snew_ref[0, :, :KV_WIDTH], row)
    v_new = _own_group(snew_ref[0, :, KV_WIDTH:], row)
    s_new = jnp.sum(qf * k_new, axis=-1, keepdims=True) * SCALE
    sm = jnp.where(chosen, s_s, NEG_INF)
    m_s = jnp.maximum(jnp.max(sm, axis=-1, keepdims=True), s_new)
    p_s = jnp.where(chosen, jnp.exp(sm - m_s), 0.0)
    p_new = jnp.exp(s_new - m_s)
    l_s = jnp.sum(p_s, axis=-1, keepdims=True) + p_new
    o_s = (_own_group(jnp.dot(p_s.astype(BF16), v_sel, preferred_element_type=F32), row) + p_new * v_new) / l_s

    k_win = _unit_rows([win_ref], win_buf, 0)
    v_win = _unit_rows([win_ref], win_buf, N_KV)
    wpos = past_len - win_buf + lax.broadcasted_iota(jnp.int32, (1, win_buf), 1)
    w_mask = ((past_len - wpos) <= WINDOW) & (wpos >= 0)
    s_w = jnp.where(w_mask, _dot_nt(q2, k_win) * SCALE, NEG_INF)
    kw_new = _own_group(wnew_ref[0, :, :KV_WIDTH], row)
    vw_new = _own_group(wnew_ref[0, :, KV_WIDTH:], row)
    sw_new = jnp.sum(qf * kw_new, axis=-1, keepdims=True) * SCALE
    m_w = jnp.maximum(jnp.max(s_w, axis=-1, keepdims=True), sw_new)
    p_w = jnp.where(w_mask, jnp.exp(s_w - m_w), 0.0)
    pw_new = jnp.exp(sw_new - m_w)
    l_w = jnp.sum(p_w, axis=-1, keepdims=True) + pw_new
    o_w = (_own_group(jnp.dot(p_w.astype(BF16), v_win, preferred_element_type=F32), row) + pw_new * vw_new) / l_w

    g_own = _own_group(gate_ref[0], row)
    lane = lax.broadcasted_iota(jnp.int32, (N_HEADS, LANES), 1)
    first = 3 * (row % Q_PER_KV)
    pick = lambda j: jnp.sum(jnp.where(lane == first + j, g_own, 0.0), axis=-1, keepdims=True)
    o = pick(0) * o_c + pick(1) * o_s + pick(2) * o_w
    ms = jnp.sum(jnp.sum(o * o, axis=-1, keepdims=True), axis=0, keepdims=True) / D_ATTN
    o_ref[0] = (o * lax.rsqrt(ms + EPS) * gain_ref[...]).astype(o_ref.dtype)

    kept = (win_buf - 1) * UNITS
    nwin_ref[0:kept, :] = win_ref[UNITS:win_buf * UNITS, :]
    for u in range(UNITS):
        nwin_ref[kept + u:kept + u + 1, :] = wnew_ref[0, :, u * HEAD_DIM:(u + 1) * HEAD_DIM]


def _attn_sample(q_b, k_cmp, v_cmp, sel_new, cache_sel, state_win, win_new, gates, gain, layer, page_table):
    depth, n_phys, page = cache_sel.shape[:3]
    n_batch, n_pages = page_table.shape
    win_buf = state_win.shape[2]
    past_len = n_pages * page
    q3 = q_b.reshape(n_batch, N_HEADS, HEAD_DIM)
    o_c, sel = _select_sample(q3, k_cmp, v_cmp)
    pages = cache_sel.reshape(depth * n_phys * page * UNITS, HEAD_DIM)
    wins = state_win.reshape(depth * n_batch * win_buf * UNITS, HEAD_DIM)
    base = layer * n_phys
    per_b = lambda shape: pl.BlockSpec((1,) + shape, lambda b, pt: (b, 0, 0))
    page_specs = [pl.BlockSpec((page * UNITS, HEAD_DIM), lambda b, pt, p=p: (base + pt[b, p], 0))
                  for p in range(n_pages)]
    in_specs = [per_b((N_HEADS, HEAD_DIM)), per_b((N_HEADS, HEAD_DIM)), per_b((N_HEADS, LANES)),
                per_b((1, ROW_WIDTH)),
                pl.BlockSpec((win_buf * UNITS, HEAD_DIM), lambda b, pt: (layer * n_batch + b, 0)),
                per_b((1, ROW_WIDTH)), per_b((1, GATE_WIDTH)),
                pl.BlockSpec((N_HEADS, HEAD_DIM), lambda b, pt: (0, 0))] + page_specs
    out, new_win = pl.pallas_call(
        functools.partial(_attn_sample_kernel, n_pages=n_pages, page=page, win_buf=win_buf, past_len=past_len),
        grid_spec=pltpu.PrefetchScalarGridSpec(
            num_scalar_prefetch=1,
            grid=(n_batch,),
            in_specs=in_specs,
            out_specs=[per_b((N_HEADS, HEAD_DIM)), pl.BlockSpec((win_buf * UNITS, HEAD_DIM), lambda b, pt: (b, 0))],
        ),
        out_shape=[jax.ShapeDtypeStruct((n_batch, N_HEADS, HEAD_DIM), BF16),
                   jax.ShapeDtypeStruct((n_batch * win_buf * UNITS, HEAD_DIM), F32)],
        compiler_params=_params(("arbitrary",)),
        name="attn_sample",
    )(page_table, q3, o_c, sel,
      sel_new.reshape(n_batch, 1, ROW_WIDTH), wins, win_new.reshape(n_batch, 1, ROW_WIDTH),
      gates.reshape(n_batch, 1, GATE_WIDTH), gain.reshape(N_HEADS, HEAD_DIM), *([pages] * n_pages))
    return out.reshape(n_batch, D_ATTN), new_win.reshape(n_batch, win_buf, 2, N_KV, HEAD_DIM)


def _lru_gates(xc, wa_ref, wx_ref, ba, bx, lam):
    width = xc.shape[1]
    bd = width // LRU_BLOCKS
    xb = xc.astype(BF16)
    ra = jnp.concatenate([jnp.dot(xb[:, n * bd:(n + 1) * bd], wa_ref[n], preferred_element_type=F32)
                          for n in range(LRU_BLOCKS)], axis=1)
    rx = jnp.concatenate([jnp.dot(xb[:, n * bd:(n + 1) * bd], wx_ref[n], preferred_element_type=F32)
                          for n in range(LRU_BLOCKS)], axis=1)
    r = jax.nn.sigmoid(ra + ba)
    i = jax.nn.sigmoid(rx + bx)
    log_a = -LRU_C * r * _softplus(-lam)
    a = jnp.exp(log_a)
    b = jnp.sqrt(-jnp.tanh(log_a) * (a * a + 1.0)) * (i * xc)
    return a, b


def _scan_rows(a8, b8, h):
    row = lax.broadcasted_iota(jnp.int32, a8.shape, 0)
    s = 1
    while s < SUBLANES:
        a_prev = jnp.where(row >= s, pltpu.roll(a8, s, 0), 1.0)
        b_prev = jnp.where(row >= s, pltpu.roll(b8, s, 0), 0.0)
        b8 = a8 * b_prev + b8
        a8 = a8 * a_prev
        s *= 2
    return a8 * h + b8


def _lru_prompt_kernel(ux_ref, uy_ref, cw_ref, cb_ref, wa_ref, wx_ref, ba_ref, bx_ref, lam_ref, gain_ref,
                       r_ref, h_ref, x_s, h_s, a_s, b_s):
    tt = ux_ref.shape[0]
    i = pl.program_id(1)

    @pl.when(i == 0)
    def _():
        x_s[0:SUBLANES, :] = jnp.zeros((SUBLANES, x_s.shape[1]), F32)
        h_s[...] = jnp.zeros_like(h_s)

    @pl.when(i > 0)
    def _():
        x_s[0:SUBLANES, :] = x_s[tt:tt + SUBLANES, :]

    x_s[SUBLANES:, :] = ux_ref[...]
    xc = cb_ref[...]
    for j in range(CONV_W):
        k = CONV_W - 1 - j
        xc = xc + x_s[SUBLANES - k:SUBLANES - k + tt, :] * cw_ref[j:j + 1, :]
    a, b = _lru_gates(xc, wa_ref, wx_ref, ba_ref[...], bx_ref[...], lam_ref[...])
    a_s[...] = a
    b_s[...] = b

    def group(k, h):
        r0 = pl.multiple_of(k * SUBLANES, SUBLANES)
        hs = _scan_rows(a_s[pl.ds(r0, SUBLANES), :], b_s[pl.ds(r0, SUBLANES), :], h)
        a_s[pl.ds(r0, SUBLANES), :] = hs
        return hs[SUBLANES - 1:SUBLANES, :]

    h_last = lax.fori_loop(0, tt // SUBLANES, group, h_s[...])
    h_s[...] = h_last
    h_ref[0] = h_last
    y = a_s[...] * jax.nn.gelu(uy_ref[...])
    r_ref[...] = _rms(y, gain_ref[...]).astype(r_ref.dtype)


def _lru_prompt(proj, n_batch, seq_len, weights):
    width = D_ATTN
    tt = _row_tile(seq_len, 512, SUBLANES)
    n_t = seq_len // tt
    rows = lambda col: pl.BlockSpec((tt, width), lambda b, i: (b * n_t + i, col))
    vec = _full((1, width))
    blk = _full((LRU_BLOCKS, width // LRU_BLOCKS, width // LRU_BLOCKS))
    return pl.pallas_call(
        _lru_prompt_kernel,
        grid=(n_batch, n_t),
        in_specs=[rows(COL_UX // width), rows(COL_UY // width), _full((CONV_W, width)), vec, blk, blk,
                  vec, vec, vec, vec],
        out_specs=[pl.BlockSpec((tt, width), lambda b, i: (b * n_t + i, 0)),
                   pl.BlockSpec((1, 1, width), lambda b, i: (b, 0, 0))],
        out_shape=[jax.ShapeDtypeStruct((n_batch * seq_len, width), BF16),
                   jax.ShapeDtypeStruct((n_batch, 1, width), F32)],
        scratch_shapes=[pltpu.VMEM((tt + SUBLANES, width), F32), pltpu.VMEM((1, width), F32),
                        pltpu.VMEM((tt, width), F32), pltpu.VMEM((tt, width), F32)],
        compiler_params=_params(("parallel", "arbitrary")),
        name="lru_prompt",
    )(proj, proj, *weights)


def _lru_sample_kernel(ux_ref, uy_ref, c0_ref, c1_ref, c2_ref, h0_ref, cw_ref, cb_ref, wa_ref, wx_ref,
                       ba_ref, bx_ref, lam_ref, gain_ref, r_ref, h_ref):
    xc = (cb_ref[...] + c0_ref[...] * cw_ref[0:1, :] + c1_ref[...] * cw_ref[1:2, :]
          + c2_ref[...] * cw_ref[2:3, :] + ux_ref[...] * cw_ref[3:4, :])
    a, b = _lru_gates(xc, wa_ref, wx_ref, ba_ref[...], bx_ref[...], lam_ref[...])
    h = a * h0_ref[...] + b
    h_ref[...] = h
    r_ref[...] = _rms(h * jax.nn.gelu(uy_ref[...]), gain_ref[...]).astype(r_ref.dtype)


def _lru_sample(ux, uy, conv_state, h0, weights):
    n, width = ux.shape
    mat = _full((n, width))
    vec = _full((1, width))
    blk = _full((LRU_BLOCKS, width // LRU_BLOCKS, width // LRU_BLOCKS))
    return pl.pallas_call(
        _lru_sample_kernel,
        grid=(1,),
        in_specs=[mat] * 6 + [_full((CONV_W, width)), vec, blk, blk, vec, vec, vec, vec],
        out_specs=[mat, mat],
        out_shape=[jax.ShapeDtypeStruct((n, width), BF16), jax.ShapeDtypeStruct((n, width), F32)],
        compiler_params=_params(("arbitrary",)),
        name="lru_sample",
    )(ux, uy, conv_state[:, 0], conv_state[:, 1], conv_state[:, 2], h0, *weights)


def _swiglu_step(x, wg_ref, wu_ref, wd_ref):
    g = jnp.dot(x, wg_ref[...].astype(BF16), preferred_element_type=F32)
    u = jnp.dot(x, wu_ref[...].astype(BF16), preferred_element_type=F32)
    hid = (jax.nn.silu(g) * u).astype(BF16)
    return jnp.dot(hid, wd_ref[...].astype(BF16), preferred_element_type=F32)


def _ffn_kernel(x_ref, wg_ref, wu_ref, wd_ref, res_ref, o_ref):
    @pl.when(pl.program_id(1) == 0)
    def _():
        o_ref[...] = res_ref[...]

    o_ref[...] += _swiglu_step(x_ref[...], wg_ref, wu_ref, wd_ref)


def _ffn(hn, wg, wu, wd, res):
    n, d = hn.shape
    d_ff = wg.shape[1]
    tm = _row_tile(n, 832)
    tf = _row_tile(d_ff, 256, LANES)
    row = pl.BlockSpec((tm, d), lambda i, f: (i, 0))
    return pl.pallas_call(
        _ffn_kernel,
        grid=(n // tm, d_ff // tf),
        in_specs=[row, pl.BlockSpec((d, tf), lambda i, f: (0, f)), pl.BlockSpec((d, tf), lambda i, f: (0, f)),
                  pl.BlockSpec((tf, d), lambda i, f: (f, 0)), row],
        out_specs=row,
        out_shape=jax.ShapeDtypeStruct((n, d), F32),
        compiler_params=_params(("parallel", "arbitrary")),
        name="ffn",
    )(hn, wg, wu, wd, res)


MOE_BLOCK = 1280
MOE_TILE = 256
COMBINE_TILE = 320


def _router_kernel(x_ref, g_ref, w_ref, o_ref, *, n_experts):
    hn = _rms(x_ref[...], g_ref[...]).astype(BF16)
    logits = jnp.dot(hn, w_ref[...], preferred_element_type=F32)
    lane = lax.broadcasted_iota(jnp.int32, logits.shape, 1).astype(F32)
    logits = jnp.where(lane < n_experts, logits, -jnp.inf)
    m1 = jnp.max(logits, axis=-1, keepdims=True)
    i1 = jnp.min(jnp.where(logits == m1, lane, float(LANES)), axis=-1, keepdims=True)
    rest = jnp.where(lane == i1, -jnp.inf, logits)
    m2 = jnp.max(rest, axis=-1, keepdims=True)
    i2 = jnp.min(jnp.where(rest == m2, lane, float(LANES)), axis=-1, keepdims=True)
    e2 = jnp.exp(m2 - m1)
    den = 1.0 + e2
    o_ref[...] = (jnp.where(lane == 0.0, 1.0 / den, 0.0) + jnp.where(lane == 1.0, e2 / den, 0.0)
                  + jnp.where(lane == 2.0, i1, 0.0) + jnp.where(lane == 3.0, i2, 0.0))


def _router(x, gain, w_router):
    n, d = x.shape
    n_experts = w_router.shape[1]
    tm = _row_tile(n, 832)
    w = jnp.concatenate([w_router, jnp.zeros((d, LANES - n_experts), w_router.dtype)], axis=1).astype(BF16)
    return pl.pallas_call(
        functools.partial(_router_kernel, n_experts=n_experts),
        grid=(n // tm,),
        in_specs=[pl.BlockSpec((tm, d), lambda i: (i, 0)), _full((1, d)), _full((d, LANES))],
        out_specs=pl.BlockSpec((tm, LANES), lambda i: (i, 0)),
        out_shape=jax.ShapeDtypeStruct((n, LANES), F32),
        compiler_params=_params(("parallel",)),
        name="router",
    )(x, gain.reshape(1, d), w)


def _start_row_gather(ids_ref, first, n_rows, src_hbm, dst, sem):
    def body(r, carry):
        pltpu.make_async_copy(src_hbm.at[pl.ds(ids_ref[first + r], 1)], dst.at[pl.ds(r, 1)], sem).start()
        return carry

    lax.fori_loop(0, n_rows, body, 0, unroll=8)


def _dispatch_kernel(ids_ref, on_ref, x_hbm, g_ref, o_ref, buf, sem):
    i = pl.program_id(0)
    slot = i % 2
    start = lambda tile, s: _start_row_gather(ids_ref, tile * MOE_TILE, MOE_TILE, x_hbm, buf.at[s], sem.at[s])

    @pl.when((i == 0) & (on_ref[0] > 0))
    def _():
        start(0, 0)

    @pl.when(i + 1 < pl.num_programs(0))
    def _():
        @pl.when(on_ref[i + 1] > 0)
        def _():
            start(i + 1, 1 - slot)

    @pl.when(on_ref[i] > 0)
    def _():
        pltpu.make_async_copy(buf.at[slot], buf.at[slot], sem.at[slot]).wait()
        o_ref[...] = _rms(buf[slot], g_ref[...]).astype(o_ref.dtype)

    @pl.when(on_ref[i] == 0)
    def _():
        o_ref[...] = jnp.zeros_like(o_ref)


def _dispatch(x, gain, row_ids, tile_on):
    n, d = x.shape
    n_slots = row_ids.shape[0]
    return pl.pallas_call(
        _dispatch_kernel,
        grid_spec=pltpu.PrefetchScalarGridSpec(
            num_scalar_prefetch=2,
            grid=(n_slots // MOE_TILE,),
            in_specs=[pl.BlockSpec(memory_space=pl.ANY), pl.BlockSpec((1, d), lambda i, ids, on: (0, 0))],
            out_specs=pl.BlockSpec((MOE_TILE, d), lambda i, ids, on: (i, 0)),
            scratch_shapes=[pltpu.VMEM((2, MOE_TILE, d), F32), pltpu.SemaphoreType.DMA((2,))],
        ),
        out_shape=jax.ShapeDtypeStruct((n_slots, d), BF16),
        compiler_params=_params(("arbitrary",)),
        name="moe_dispatch",
    )(row_ids, tile_on, x, gain.reshape(1, d))


def _experts_kernel(be_ref, br_ref, x_ref, w_ref, wg_ref, wu_ref, wd_ref, o_ref):
    b, f = pl.program_id(0), pl.program_id(1)
    rows = br_ref[b]

    @pl.when(f == 0)
    def _():
        o_ref[...] = jnp.zeros_like(o_ref)

    @pl.when(rows == MOE_BLOCK)
    def _():
        o_ref[...] += _swiglu_step(x_ref[...], wg_ref.at[0], wu_ref.at[0], wd_ref.at[0])

    @pl.when((rows > 0) & (rows < MOE_BLOCK))
    def _():
        for s in range(MOE_BLOCK // MOE_TILE):
            @pl.when(s * MOE_TILE < rows)
            def _():
                sub = pl.ds(s * MOE_TILE, MOE_TILE)
                o_ref[sub, :] += _swiglu_step(x_ref[sub, :], wg_ref.at[0], wu_ref.at[0], wd_ref.at[0])

    @pl.when((f == pl.num_programs(1) - 1) & (rows > 0))
    def _():
        o_ref[...] = o_ref[...] * w_ref[...]


def _experts(xs, slot_w, blk_expert, blk_rows, wg, wu, wd):
    n_slots, d = xs.shape
    d_ff = wg.shape[2]
    tf = _row_tile(d_ff, 256, LANES)
    n_f = d_ff // tf
    fcol = lambda b, f, br: jnp.where(br[b] > 0, f, n_f - 1)
    return pl.pallas_call(
        _experts_kernel,
        grid_spec=pltpu.PrefetchScalarGridSpec(
            num_scalar_prefetch=2,
            grid=(n_slots // MOE_BLOCK, n_f),
            in_specs=[pl.BlockSpec((MOE_BLOCK, d), lambda b, f, be, br: (b, 0)),
                      pl.BlockSpec((MOE_BLOCK, 1), lambda b, f, be, br: (b, 0)),
                      pl.BlockSpec((1, d, tf), lambda b, f, be, br: (be[b], 0, fcol(b, f, br))),
                      pl.BlockSpec((1, d, tf), lambda b, f, be, br: (be[b], 0, fcol(b, f, br))),
                      pl.BlockSpec((1, tf, d), lambda b, f, be, br: (be[b], fcol(b, f, br), 0))],
            out_specs=pl.BlockSpec((MOE_BLOCK, d), lambda b, f, be, br: (b, 0)),
        ),
        out_shape=jax.ShapeDtypeStruct((n_slots, d), F32),
        compiler_params=_params(("arbitrary", "arbitrary")),
        name="moe_experts",
    )(blk_expert, blk_rows, xs, slot_w, wg, wu, wd)


def _combine_kernel(s0_ref, s1_ref, y_hbm, x_ref, o_ref, buf, sem):
    tc = x_ref.shape[0]
    i = pl.program_id(0)
    slot = i % 2

    def start(tile, s):
        _start_row_gather(s0_ref, tile * tc, tc, y_hbm, buf.at[s, 0], sem.at[s])
        _start_row_gather(s1_ref, tile * tc, tc, y_hbm, buf.at[s, 1], sem.at[s])

    @pl.when(i == 0)
    def _():
        start(0, 0)

    @pl.when(i + 1 < pl.num_programs(0))
    def _():
        start(i + 1, 1 - slot)

    pltpu.make_async_copy(buf.at[slot], buf.at[slot], sem.at[slot]).wait()
    o_ref[...] = x_ref[...] + buf[slot, 0] + buf[slot, 1]


def _combine(x, y, slot0, slot1):
    n, d = x.shape
    tc = _row_tile(n, COMBINE_TILE, SUBLANES)
    row = pl.BlockSpec((tc, d), lambda i, s0, s1: (i, 0))
    return pl.pallas_call(
        _combine_kernel,
        grid_spec=pltpu.PrefetchScalarGridSpec(
            num_scalar_prefetch=2,
            grid=(n // tc,),
            in_specs=[pl.BlockSpec(memory_space=pl.ANY), row],
            out_specs=row,
            scratch_shapes=[pltpu.VMEM((2, 2, tc, d), F32), pltpu.SemaphoreType.DMA((2,))],
        ),
        out_shape=jax.ShapeDtypeStruct((n, d), F32),
        compiler_params=_params(("arbitrary",)),
        name="moe_combine",
    )(slot0, slot1, y, x)


def _routing_tables(route, n_experts):
    n = route.shape[0]
    expert = route[:, 2:4].astype(jnp.int32).T.reshape(-1)
    weight = route[:, 0:2].T.reshape(-1)
    token = jnp.tile(jnp.arange(n, dtype=jnp.int32), 2)
    onehot = (expert[:, None] == jnp.arange(n_experts, dtype=jnp.int32)).astype(jnp.int32)
    rank = jnp.sum((jnp.cumsum(onehot, axis=0) - 1) * onehot, axis=1)
    count = jnp.sum(onehot, axis=0)
    padded = (count + MOE_BLOCK - 1) // MOE_BLOCK * MOE_BLOCK
    ends = jnp.cumsum(padded)
    starts = ends - padded
    slot = starts[expert] + rank
    n_slots = (2 * n + n_experts * (MOE_BLOCK - 1) + MOE_BLOCK - 1) // MOE_BLOCK * MOE_BLOCK
    row_ids = jnp.zeros((n_slots,), jnp.int32).at[slot].set(token)
    slot_w = jnp.zeros((n_slots,), F32).at[slot].set(weight).reshape(n_slots, 1)
    blk_first = jnp.arange(n_slots // MOE_BLOCK, dtype=jnp.int32) * MOE_BLOCK
    blk_expert = jnp.minimum(jnp.sum((ends[None, :] <= blk_first[:, None]).astype(jnp.int32), axis=1), n_experts - 1)
    used = (count + MOE_TILE - 1) // MOE_TILE * MOE_TILE
    blk_rows = jnp.clip(starts[blk_expert] + used[blk_expert] - blk_first, 0, MOE_BLOCK)
    blk_rows = jnp.where(blk_first < ends[-1], blk_rows, 0).astype(jnp.int32)
    tile_first = jnp.arange(n_slots // MOE_TILE, dtype=jnp.int32) * MOE_TILE
    tile_on = (tile_first % MOE_BLOCK < jnp.repeat(blk_rows, MOE_BLOCK // MOE_TILE)).astype(jnp.int32)
    return row_ids, slot_w, blk_expert, blk_rows, tile_on, slot[:n], slot[n:]


def _moe(x, gain, w_router, wg, wu, wd):
    route = _router(x, gain, w_router)
    row_ids, slot_w, blk_expert, blk_rows, tile_on, slot0, slot1 = _routing_tables(route, w_router.shape[1])
    xs = _dispatch(x, gain, row_ids, tile_on)
    y = _experts(xs, slot_w, blk_expert, blk_rows, wg, wu, wd)
    return _combine(x, y, slot0, slot1)


def kernel(x_prompt, x_sample, cache_cmp, cache_sel, state_win, state_lru, state_conv, page_table, ln_mix, w_in, q_norm, k_norm, cmp_pe, cmp_w1, cmp_w2, conv_w, conv_b, lru_wa, lru_ba, lru_wx, lru_bx, lru_lambda, out_norm_attn, out_norm_lru, w_out, ln_ffn, ffn_w_gate, ffn_w_up, ffn_w_down, moe_router, moe_w_gate, moe_w_up, moe_w_down):
    n_batch, seq_len, d_model = x_prompt.shape
    dec_batch, dec_seq, _ = x_sample.shape
    depth = ln_mix.shape[0]
    page = cache_cmp.shape[2]
    n_pages = page_table.shape[1]
    past_len = n_pages * page
    win_buf = state_win.shape[2]
    lru_width = d_model - D_ATTN
    assert dec_seq == 1 and lru_width == D_ATTN
    assert seq_len % Q_BLOCK == 0 and seq_len >= WIN_SPAN and seq_len % SEL_CHUNK == 0 and seq_len >= win_buf
    assert past_len % Q_BLOCK == 0 and win_buf <= past_len and page % CMP_BLOCK == 0
    n_p = n_batch * seq_len
    nc = seq_len // CMP_BLOCK
    nc_past = past_len // CMP_BLOCK
    kv_row = (2, N_KV, HEAD_DIM)

    pos = jnp.concatenate([jnp.tile(jnp.arange(seq_len), n_batch), jnp.full((dec_batch,), past_len)])
    tables = _rope_tables(pos)
    end_p = _rope_tables((jnp.arange(nc) + 1) * CMP_BLOCK - 1)
    end_s = _rope_tables((jnp.arange(nc_past) + 1) * CMP_BLOCK - 1)

    x = jnp.concatenate([x_prompt.reshape(n_p, d_model), x_sample.reshape(dec_batch, d_model)], axis=0)
    outs = [[] for _ in range(10)]
    for l in range(depth):
        hn = _rmsnorm(x, ln_mix[l])
        proj = _matmul([hn], [_reorder_w_in(w_in[l])], tm_target=832, tn_target=1280)
        q_b, sel_f, sel_b, win_f, win_b, gates = _postproj(proj, tables, q_norm[l], k_norm[l, 1], k_norm[l, 2])
        c_rows = proj[:, COL_CMP:COL_CMP + ROW_WIDTH]

        cmp_w = (cmp_pe[l], cmp_w1[l].reshape(2, CMP_BLOCK // 2, 2 * HEAD_DIM, HEAD_DIM).astype(BF16),
                 cmp_w2[l].astype(BF16), k_norm[l, 0].reshape(1, HEAD_DIM))
        lru_w = (conv_w[l], conv_b[l].reshape(1, -1), lru_wa[l].astype(BF16), lru_wx[l].astype(BF16),
                 lru_ba[l].reshape(1, -1), lru_bx[l].reshape(1, -1), lru_lambda[l].reshape(1, -1),
                 out_norm_lru[l].reshape(1, -1))

        kc_p, vc_p = _compress_prompt(c_rows[:n_p].reshape(n_batch * nc, CMP_ROW), n_batch, nc, cmp_w, end_p)
        o_p = _attn_prompt(q_b, _deinterleave_blocks(kc_p, n_batch, nc), _deinterleave_blocks(vc_p, n_batch, nc),
                           sel_b, win_b, gates, out_norm_attn[l], n_batch, seq_len)
        r_p, h_p = _lru_prompt(proj, n_batch, seq_len, lru_w)

        kc_s, vc_s = _compress_sample(cache_cmp, l, page_table, cmp_w, end_s, nb=2 if dec_batch % 2 == 0 else 1)
        o_s, new_win_s = _attn_sample(q_b[n_p:], _deinterleave_blocks(kc_s, dec_batch, nc_past),
                                      _deinterleave_blocks(vc_s, dec_batch, nc_past), sel_f[n_p:], cache_sel,
                                      state_win, win_f[n_p:], gates[n_p:], out_norm_attn[l], l, page_table)
        ux_s = proj[n_p:, COL_UX:COL_UX + lru_width]
        r_s, h_s = _lru_sample(ux_s, proj[n_p:, COL_UY:COL_UY + lru_width], state_conv[l], state_lru[l], lru_w)

        w_o = w_out[l].astype(BF16)
        x = _matmul([jnp.concatenate([o_p, o_s], axis=0), jnp.concatenate([r_p, r_s], axis=0)],
                    [w_o[:D_ATTN], w_o[D_ATTN:]], res=x, tm_target=832, tn_target=1024)

        k = l // 2
        if l % 2 == 0:
            x = _ffn(_rmsnorm(x, ln_ffn[l]), ffn_w_gate[k], ffn_w_up[k], ffn_w_down[k], x)
        else:
            x = _moe(x, ln_ffn[l], moe_router[k], moe_w_gate[k], moe_w_up[k], moe_w_down[k])

        ux_p = proj[:n_p, COL_UX:COL_UX + lru_width].reshape(n_batch, seq_len, lru_width)
        new = (c_rows[:n_p].reshape((n_batch, seq_len) + kv_row), c_rows[n_p:].reshape((dec_batch, 1) + kv_row),
               sel_f[:n_p].reshape((n_batch, seq_len) + kv_row), sel_f[n_p:].reshape((dec_batch, 1) + kv_row),
               win_f[:n_p].reshape((n_batch, seq_len) + kv_row)[:, seq_len - win_buf:],
               new_win_s,
               h_p.reshape(n_batch, lru_width), h_s,
               ux_p[:, seq_len - (CONV_W - 1):],
               jnp.concatenate([state_conv[l][:, 1:], ux_s[:, None, :]], axis=1))
        for acc, val in zip(outs, new):
            acc.append(val)

    return (x[:n_p].reshape(n_batch, seq_len, d_model), x[n_p:].reshape(dec_batch, 1, d_model),
            *[jnp.stack(v) for v in outs])
```

```python
import functools

import jax
import jax.numpy as jnp
from jax import lax
from jax.experimental import pallas as pl
from jax.experimental.pallas import tpu as pltpu

N_HEADS = 8
HEAD_DIM = 128
N_KV = 2
Q_PER_KV = N_HEADS // N_KV
D_ATTN = N_HEADS * HEAD_DIM
KV_WIDTH = N_KV * HEAD_DIM
ROW_WIDTH = 2 * KV_WIDTH
UNITS = 2 * N_KV
ROT_DIM = HEAD_DIM // 4
ROPE_THETA = 500000.0
CMP_BLOCK = 32
SEL_BLOCK = 64
TOP_N = 16
WINDOW = 512
Q_BLOCK = 128
LRU_BLOCKS = 8
CONV_W = 4
LRU_C = 8.0
TOP_K = 2
EPS = 1e-6
SCALE = HEAD_DIM ** -0.5
NEG_INF = -1e30

LANES = 128
SUBLANES = 8
VMEM_LIMIT = 56 * 1024 * 1024

F32 = jnp.float32
BF16 = jnp.bfloat16


def _row_tile(n, target, mult=16):
    best = None
    for t in range(mult, min(n, target) + 1, mult):
        if n % t == 0:
            best = t
    return best if best is not None else n


def _params(semantics):
    return pltpu.CompilerParams(dimension_semantics=semantics, vmem_limit_bytes=VMEM_LIMIT)


def _full(shape):
    zeros = (0,) * len(shape)
    return pl.BlockSpec(shape, lambda *_: zeros)


def _rms(x, gain):
    return x * lax.rsqrt(jnp.mean(x * x, axis=-1, keepdims=True) + EPS) * gain


def _rope_apply(y, cos_t, sin_lo, sin_hi):
    half = ROT_DIM // 2
    return y * cos_t + pltpu.roll(y, HEAD_DIM - half, 1) * sin_lo + pltpu.roll(y, half, 1) * sin_hi


def _softplus(x):
    return jnp.maximum(x, 0.0) + jnp.log1p(jnp.exp(-jnp.abs(x)))


def _topk_mask(score, n_pick):
    n_lanes = score.shape[-1]
    lane = lax.broadcasted_iota(jnp.int32, score.shape, 1).astype(F32)
    sel = jnp.zeros(score.shape, F32)
    for _ in range(n_pick):
        m = jnp.max(score, axis=-1, keepdims=True)
        idx = jnp.min(jnp.where(score == m, lane, float(n_lanes)), axis=-1, keepdims=True)
        pick = lane == idx
        sel = jnp.where(pick, 1.0, sel)
        score = jnp.where(pick, -2.0, score)
    return sel


def _topk_mask_cols(score, n_pick):
    n_rows = score.shape[0]
    row = lax.broadcasted_iota(jnp.int32, score.shape, 0).astype(F32)
    sel = jnp.zeros(score.shape, F32)
    for _ in range(n_pick):
        m = jnp.max(score, axis=0, keepdims=True)
        idx = jnp.min(jnp.where(score == m, row, float(n_rows)), axis=0, keepdims=True)
        pick = row == idx
        sel = jnp.where(pick, 1.0, sel)
        score = jnp.where(pick, -2.0, score)
    return sel


def _masked_softmax_rows(s, mask):
    sm = jnp.where(mask, s, NEG_INF)
    m = jnp.max(sm, axis=-1, keepdims=True)
    e = jnp.where(mask, jnp.exp(sm - m), 0.0)
    l = jnp.sum(e, axis=-1, keepdims=True)
    return e / jnp.where(l > 0.0, l, 1.0)


def _dot_nt(a, b):
    return lax.dot_general(a, b, (((1,), (1,)), ((), ())), preferred_element_type=F32)


def _matmul_kernel(*refs, n_lhs, has_res, has_gain):
    o_ref = refs[-1]
    lhs = refs[0][...]
    if has_gain:
        lhs = _rms(lhs, refs[2 * n_lhs + has_res][...]).astype(BF16)
    acc = jnp.dot(lhs, refs[n_lhs][...], preferred_element_type=F32)
    for k in range(1, n_lhs):
        acc = acc + jnp.dot(refs[k][...], refs[n_lhs + k][...], preferred_element_type=F32)
    if has_res:
        acc = acc + refs[2 * n_lhs][...]
    o_ref[...] = acc


def _matmul(xs, ws, res=None, gain=None, *, tm_target, tn_target):
    n = xs[0].shape[0]
    d_out = ws[0].shape[1]
    tm = _row_tile(n, tm_target)
    tn = _row_tile(d_out, tn_target, LANES)
    in_specs = [pl.BlockSpec((tm, x.shape[1]), lambda i, j: (i, 0)) for x in xs]
    in_specs += [pl.BlockSpec((w.shape[0], tn), lambda i, j: (0, j)) for w in ws]
    args = list(xs) + list(ws)
    if res is not None:
        in_specs.append(pl.BlockSpec((tm, tn), lambda i, j: (i, j)))
        args.append(res)
    if gain is not None:
        in_specs.append(pl.BlockSpec((1, gain.shape[0]), lambda i, j: (0, 0)))
        args.append(gain.reshape(1, -1))
    return pl.pallas_call(
        functools.partial(_matmul_kernel, n_lhs=len(xs), has_res=res is not None, has_gain=gain is not None),
        grid=(n // tm, d_out // tn),
        in_specs=in_specs,
        out_specs=pl.BlockSpec((tm, tn), lambda i, j: (i, j)),
        out_shape=jax.ShapeDtypeStruct((n, d_out), F32),
        compiler_params=_params(("parallel", "parallel")),
        name="matmul",
    )(*args)


PROJ_WIDTH = 5120
COL_UX, COL_UY, COL_CMP, COL_SEL, COL_WIN, COL_GATE = 1024, 2048, 3072, 3584, 4096, 4608
GATE_WIDTH = 2 * LANES


def _reorder_w_in(w):
    lru = D_ATTN
    c_kv = D_ATTN
    c_g = c_kv + 3 * ROW_WIDTH
    c_ux = c_g + 3 * N_HEADS
    c_uy = c_ux + lru
    n_g = 3 * Q_PER_KV
    zpad = lambda k: jnp.zeros((w.shape[0], k), w.dtype)
    parts = [w[:, :D_ATTN], w[:, c_ux:c_uy], w[:, c_uy:c_uy + lru], w[:, c_kv:c_g],
             w[:, c_g:c_g + n_g], zpad(LANES - n_g), w[:, c_g + n_g:c_ux], zpad(LANES - n_g)]
    width = sum(p.shape[1] for p in parts)
    parts.append(zpad(PROJ_WIDTH - width))
    return jnp.concatenate(parts, axis=1).astype(BF16)


def _postproj_kernel(q_ref, sel_ref, win_ref, gate_ref, cos_ref, slo_ref, shi_ref,
                     qg_ref, ksg_ref, kwg_ref,
                     qo_ref, sel_f_ref, sel_b_ref, win_f_ref, win_b_ref, gate_o_ref):
    cos_t, slo, shi = cos_ref[...], slo_ref[...], shi_ref[...]
    for h in range(N_HEADS):
        cols = slice(h * HEAD_DIM, (h + 1) * HEAD_DIM)
        y = _rope_apply(_rms(q_ref[:, cols], qg_ref[...]), cos_t, slo, shi)
        qo_ref[:, cols] = y.astype(qo_ref.dtype)
    for src, gain_ref, out_f, out_b in ((sel_ref, ksg_ref, sel_f_ref, sel_b_ref),
                                        (win_ref, kwg_ref, win_f_ref, win_b_ref)):
        for g in range(N_KV):
            cols = slice(g * HEAD_DIM, (g + 1) * HEAD_DIM)
            y = _rope_apply(_rms(src[:, cols], gain_ref[...]), cos_t, slo, shi)
            out_f[:, cols] = y
            out_b[:, cols] = y.astype(out_b.dtype)
        v = src[:, KV_WIDTH:]
        out_f[:, KV_WIDTH:] = v
        out_b[:, KV_WIDTH:] = v.astype(out_b.dtype)
    gate_o_ref[...] = jax.nn.sigmoid(gate_ref[...])


def _postproj(proj, tables, q_gain, ks_gain, kw_gain):
    n = proj.shape[0]
    tm = _row_tile(n, 416)
    rows = lambda width, col: pl.BlockSpec((tm, width), lambda i: (i, col))
    tab = pl.BlockSpec((tm, HEAD_DIM), lambda i: (i, 0))
    gain = _full((1, HEAD_DIM))
    out = lambda width: pl.BlockSpec((tm, width), lambda i: (i, 0))
    sds = jax.ShapeDtypeStruct
    return pl.pallas_call(
        _postproj_kernel,
        grid=(n // tm,),
        in_specs=[rows(D_ATTN, 0), rows(ROW_WIDTH, COL_SEL // ROW_WIDTH), rows(ROW_WIDTH, COL_WIN // ROW_WIDTH),
                  rows(GATE_WIDTH, COL_GATE // GATE_WIDTH), tab, tab, tab, gain, gain, gain],
        out_specs=[out(D_ATTN), out(ROW_WIDTH), out(ROW_WIDTH), out(ROW_WIDTH), out(ROW_WIDTH), out(GATE_WIDTH)],
        out_shape=[sds((n, D_ATTN), BF16), sds((n, ROW_WIDTH), F32), sds((n, ROW_WIDTH), BF16),
                   sds((n, ROW_WIDTH), F32), sds((n, ROW_WIDTH), BF16), sds((n, GATE_WIDTH), F32)],
        compiler_params=_params(("parallel",)),
        name="postproj",
    )(proj, proj, proj, proj, *tables, q_gain.reshape(1, -1), ks_gain.reshape(1, -1), kw_gain.reshape(1, -1))


def _rope_tables(pos):
    half = ROT_DIM // 2
    inv_freq = ROPE_THETA ** (-2.0 * jnp.arange(half, dtype=F32) / ROT_DIM)
    ang = pos.astype(F32)[:, None] * inv_freq
    cos, sin = jnp.cos(ang), jnp.sin(ang)
    n = pos.shape[0]
    ones = jnp.ones((n, HEAD_DIM - ROT_DIM), F32)
    zeros = jnp.zeros((n, HEAD_DIM - half), F32)
    cos_t = jnp.concatenate([cos, cos, ones], axis=1)
    sin_lo = jnp.concatenate([-sin, zeros], axis=1)
    sin_hi = jnp.concatenate([jnp.zeros((n, half), F32), sin, jnp.zeros((n, HEAD_DIM - ROT_DIM), F32)], axis=1)
    return cos_t, sin_lo, sin_hi


CMP_ROW = CMP_BLOCK * ROW_WIDTH


def _compress_rows(piece, n_rows, w1_ref, w2_ref, gain_ref, cos_ref, slo_ref, shi_ref, k_ref, v_ref):
    for n in range(2):
        acc = jnp.zeros((N_KV * n_rows, HEAD_DIM), F32)
        for tp in range(CMP_BLOCK // 2):
            lhs = jnp.concatenate([jnp.concatenate([piece(n, 2 * tp, g), piece(n, 2 * tp + 1, g)], axis=1)
                                   for g in range(N_KV)], axis=0)
            acc = acc + jnp.dot(lhs, w1_ref[n, tp], preferred_element_type=F32)
        hid = jax.nn.gelu(acc).astype(BF16)
        comp = jnp.dot(hid, w2_ref[n], preferred_element_type=F32)
        for g in range(N_KV):
            part = comp[g * n_rows:(g + 1) * n_rows]
            cols = slice(g * HEAD_DIM, (g + 1) * HEAD_DIM)
            if n == 0:
                part = _rope_apply(_rms(part, gain_ref[...]), cos_ref[...], slo_ref[...], shi_ref[...])
                k_ref[:, cols] = part.astype(k_ref.dtype)
            else:
                v_ref[:, cols] = part.astype(v_ref.dtype)


def _compress_prompt_kernel(x_ref, pe_ref, *rest):
    def piece(n, t, g):
        c0 = t * ROW_WIDTH + n * KV_WIDTH + g * HEAD_DIM
        return (x_ref[:, c0:c0 + HEAD_DIM] + pe_ref[n, t:t + 1, :]).astype(BF16)

    _compress_rows(piece, x_ref.shape[0], *rest)


def _compress_sample_kernel(pt_ref, *refs, n_page_refs, page):
    del pt_ref
    pages = refs[:n_page_refs]
    pe_ref = refs[n_page_refs]
    rest = refs[n_page_refs + 1:-1]
    y_s = refs[-1]
    n_tok = 2 * page
    out_row = lax.broadcasted_iota(jnp.int32, (n_tok, n_tok), 0)
    src_tok = lax.broadcasted_iota(jnp.int32, (n_tok, n_tok), 1)
    regroup = ((out_row % SUBLANES) * CMP_BLOCK + out_row // SUBLANES == src_tok).astype(BF16)
    pe_tok = [jnp.concatenate([jnp.concatenate([pe_ref[n]] * N_KV, axis=1)] * (n_tok // CMP_BLOCK), axis=0)
              for n in range(2)]
    for pair in range(n_page_refs // 2):
        for n in range(2):
            tok = jnp.concatenate([jnp.concatenate([pages[2 * pair + k][pl.ds(n * N_KV + g, page, stride=UNITS), :]
                                                    for g in range(N_KV)], axis=1) for k in range(2)], axis=0)
            y = jnp.dot(regroup, (tok + pe_tok[n]).astype(BF16), preferred_element_type=F32)
            y_s[n, :, pair * SUBLANES:(pair + 1) * SUBLANES, :] = y.reshape(CMP_BLOCK, SUBLANES, KV_WIDTH)

    _compress_rows(lambda n, t, g: y_s[n, t, :, g * HEAD_DIM:(g + 1) * HEAD_DIM].astype(BF16), y_s.shape[2], *rest)


def _compress_weight_specs():
    return [_full((2, CMP_BLOCK, HEAD_DIM)), _full((2, CMP_BLOCK // 2, 2 * HEAD_DIM, HEAD_DIM)),
            _full((2, HEAD_DIM, HEAD_DIM)), _full((1, HEAD_DIM))]


def _compress_prompt(c_rows, n_batch, nc, weights, end_tables):
    tr = _row_tile(nc, 128)
    per_b = nc // tr
    tab = pl.BlockSpec((tr, HEAD_DIM), lambda i: (i % per_b, 0))
    out = pl.BlockSpec((tr, KV_WIDTH), lambda i: (i, 0))
    sds = jax.ShapeDtypeStruct((n_batch * nc, KV_WIDTH), BF16)
    return pl.pallas_call(
        _compress_prompt_kernel,
        grid=(n_batch * per_b,),
        in_specs=[pl.BlockSpec((tr, CMP_ROW), lambda i: (i, 0))] + _compress_weight_specs() + [tab, tab, tab],
        out_specs=[out, out],
        out_shape=[sds, sds],
        compiler_params=_params(("parallel",)),
        name="compress_prompt",
    )(c_rows, *weights, *end_tables)


def _compress_sample(cache, layer, page_table, weights, end_tables, nb):
    depth, n_phys, page = cache.shape[:3]
    n_batch, n_pages = page_table.shape
    assert n_pages % 2 == 0 and 2 * page == SUBLANES * CMP_BLOCK
    nc = n_pages * page // CMP_BLOCK
    pages = cache.reshape(depth * n_phys * page * UNITS, HEAD_DIM)
    base = layer * n_phys
    page_specs = [pl.BlockSpec((page * UNITS, HEAD_DIM), lambda i, pt, bb=bb, p=p: (base + pt[i * nb + bb, p], 0))
                  for bb in range(nb) for p in range(n_pages)]
    n_rows = nb * nc
    tab = pl.BlockSpec((n_rows, HEAD_DIM), lambda i, pt: (0, 0))
    wspecs = [pl.BlockSpec(s.block_shape, lambda i, pt, z=(0,) * len(s.block_shape): z) for s in _compress_weight_specs()]
    out = pl.BlockSpec((n_rows, KV_WIDTH), lambda i, pt: (i, 0))
    sds = jax.ShapeDtypeStruct((n_batch * nc, KV_WIDTH), BF16)
    tables = [jnp.tile(t, (nb, 1)) for t in end_tables]
    return pl.pallas_call(
        functools.partial(_compress_sample_kernel, n_page_refs=nb * n_pages, page=page),
        grid_spec=pltpu.PrefetchScalarGridSpec(
            num_scalar_prefetch=1,
            grid=(n_batch // nb,),
            in_specs=page_specs + wspecs + [tab, tab, tab],
            out_specs=[out, out],
            scratch_shapes=[pltpu.VMEM((2, CMP_BLOCK, n_rows, KV_WIDTH), F32)],
        ),
        out_shape=[sds, sds],
        compiler_params=_params(("arbitrary",)),
        name="compress_sample",
    )(page_table, *([pages] * (nb * n_pages)), *weights, *tables)


def _deinterleave_blocks(x, n_batch, nc):
    x = x.reshape(n_batch, nc, -1)
    return jnp.concatenate([x[:, 0::2], x[:, 1::2]], axis=1)


SEL_CHUNK = 256
WIN_SPAN = WINDOW + Q_BLOCK


def _attn_prompt_kernel(q_ref, kc_ref, vc_ref, sel_ref, win_ref, gate_ref, gain_ref, o_ref, *, seq_len):
    nc = seq_len // CMP_BLOCK
    ns = seq_len // SEL_BLOCK
    i = pl.program_id(1)
    q0 = i * Q_BLOCK
    tpos = q0 + lax.broadcasted_iota(jnp.int32, (Q_BLOCK, 1), 0)
    tlane = q0 + lax.broadcasted_iota(jnp.int32, (1, Q_BLOCK), 1)
    kcol = lambda g: slice(g * HEAD_DIM, (g + 1) * HEAD_DIM)
    vcol = lambda g: slice(KV_WIDTH + g * HEAD_DIM, KV_WIDTH + (g + 1) * HEAD_DIM)
    rows = lambda h: slice(h * Q_BLOCK, (h + 1) * Q_BLOCK)
    n_rows = Q_PER_KV * Q_BLOCK

    qs, o_c, o_w, sel = [], [], [], []
    for g in range(N_KV):
        q_g = jnp.concatenate([q_ref[:, (g * Q_PER_KV + h) * HEAD_DIM:(g * Q_PER_KV + h + 1) * HEAD_DIM]
                               for h in range(Q_PER_KV)], axis=0)
        qs.append(q_g)

        col = lax.broadcasted_iota(jnp.int32, (Q_BLOCK, nc), 1)
        blk = jnp.where(col < ns, 2 * col, 2 * (col - ns) + 1)
        c_mask = ((blk + 1) * CMP_BLOCK - 1) <= tpos
        s_c = (_dot_nt(q_g, kc_ref[0, :, kcol(g)]) * SCALE).reshape(Q_PER_KV, Q_BLOCK, nc)
        p_c = _masked_softmax_rows(s_c, c_mask[None])
        o_c.append(jnp.dot(p_c.reshape(n_rows, nc).astype(BF16), vc_ref[0, :, kcol(g)], preferred_element_type=F32))
        p_tok = p_c[0] + p_c[1] + p_c[2] + p_c[3]

        p_blk = p_tok.T
        imp = p_blk[:ns] + p_blk[ns:]
        sblk = lax.broadcasted_iota(jnp.int32, (ns, Q_BLOCK), 0)
        allowed = sblk * SEL_BLOCK <= tlane
        forced = (sblk == 0) | (sblk == tlane // SEL_BLOCK)
        score = jnp.where(forced, 1e30, jnp.where(allowed, imp, -1.0))
        picked = _topk_mask_cols(score, min(TOP_N, ns))
        picked = jnp.concatenate([picked, jnp.zeros((LANES - ns, Q_BLOCK), F32)], axis=0)
        sel.append(picked.T.astype(BF16))

        w0 = pl.multiple_of(jnp.maximum(q0 - WINDOW, 0), Q_BLOCK)
        wpos = w0 + lax.broadcasted_iota(jnp.int32, (1, WIN_SPAN), 1)
        dist = tpos - wpos
        w_mask = ((dist >= 0) & (dist <= WINDOW))[None]
        s_w = (_dot_nt(q_g, win_ref[pl.ds(w0, WIN_SPAN), kcol(g)]) * SCALE).reshape(Q_PER_KV, Q_BLOCK, WIN_SPAN)
        p_w = _masked_softmax_rows(s_w, w_mask)
        o_w.append(jnp.dot(p_w.reshape(n_rows, WIN_SPAN).astype(BF16), win_ref[pl.ds(w0, WIN_SPAN), vcol(g)],
                           preferred_element_type=F32))

    def sel_step(j, carry):
        k0 = pl.multiple_of(j * SEL_CHUNK, SEL_CHUNK)
        kpos = k0 + lax.broadcasted_iota(jnp.int32, (1, SEL_CHUNK), 1)
        expand = (kpos // SEL_BLOCK == lax.broadcasted_iota(jnp.int32, (LANES, SEL_CHUNK), 0)).astype(BF16)
        causal = kpos <= tpos
        out = []
        for g in range(N_KV):
            mask = (jnp.dot(sel[g], expand, preferred_element_type=F32) > 0.5) & causal
            bias = jnp.where(mask, 0.0, NEG_INF)
            k_blk = sel_ref[pl.ds(k0, SEL_CHUNK), kcol(g)]
            v_blk = sel_ref[pl.ds(k0, SEL_CHUNK), vcol(g)]
            heads = []
            for h in range(Q_PER_KV):
                m, l, acc = carry[g][h]
                sm = _dot_nt(qs[g][rows(h)], k_blk) * SCALE + bias
                m_new = jnp.maximum(m, jnp.max(sm, axis=-1, keepdims=True))
                p = jnp.exp(sm - m_new)
                alpha = jnp.exp(m - m_new)
                l = alpha * l + jnp.sum(p, axis=-1, keepdims=True)
                acc = alpha * acc + jnp.dot(p.astype(BF16), v_blk, preferred_element_type=F32)
                heads.append((m_new, l, acc))
            out.append(tuple(heads))
        return tuple(out)

    n_steps = (q0 + Q_BLOCK + SEL_CHUNK - 1) // SEL_CHUNK
    head0 = (jnp.full((Q_BLOCK, 1), NEG_INF, F32), jnp.zeros((Q_BLOCK, 1), F32), jnp.zeros((Q_BLOCK, HEAD_DIM), F32))
    swept = lax.fori_loop(0, n_steps, sel_step, ((head0,) * Q_PER_KV,) * N_KV)

    gates = gate_ref[...]
    slabs = []
    for g in range(N_KV):
        for h in range(Q_PER_KV):
            o_s = swept[g][h][2] / swept[g][h][1]
            c = g * LANES + 3 * h
            slabs.append(gates[:, c:c + 1] * o_c[g][rows(h)] + gates[:, c + 1:c + 2] * o_s
                         + gates[:, c + 2:c + 3] * o_w[g][rows(h)])
    o = jnp.concatenate(slabs, axis=1)
    o_ref[...] = _rms(o, gain_ref[...]).astype(o_ref.dtype)


def _attn_prompt(q_b, k_cmp, v_cmp, sel_b, win_b, gates, gain, n_batch, seq_len):
    n_qb = seq_len // Q_BLOCK
    nc = seq_len // CMP_BLOCK
    qrow = lambda width: pl.BlockSpec((Q_BLOCK, width), lambda b, i: (b * n_qb + i, 0))
    cmp_spec = pl.BlockSpec((1, nc, KV_WIDTH), lambda b, i: (b, 0, 0))
    seq_spec = pl.BlockSpec((seq_len, ROW_WIDTH), lambda b, i: (b, 0))
    return pl.pallas_call(
        functools.partial(_attn_prompt_kernel, seq_len=seq_len),
        grid=(n_batch, n_qb),
        in_specs=[qrow(D_ATTN), cmp_spec, cmp_spec, seq_spec, seq_spec, qrow(GATE_WIDTH), _full((1, D_ATTN))],
        out_specs=qrow(D_ATTN),
        out_shape=jax.ShapeDtypeStruct((n_batch * seq_len, D_ATTN), BF16),
        compiler_params=_params(("parallel", "arbitrary")),
        name="attn_prompt",
    )(q_b, k_cmp, v_cmp, sel_b, win_b, gates, gain.reshape(1, -1))


def _unit_rows(refs, n_tokens, first_unit):
    cols = [jnp.concatenate([r[pl.ds(first_unit + g, n_tokens, stride=UNITS), :] for r in refs], axis=0)
            for g in range(N_KV)]
    return jnp.concatenate(cols, axis=1).astype(BF16)


def _own_group(x, row):
    return jnp.where(row < Q_PER_KV, x[:, :HEAD_DIM], x[:, HEAD_DIM:])


def _split_groups(q):
    row = lax.broadcasted_iota(jnp.int32, q.shape[:-1] + (1,), q.ndim - 2)
    zero = jnp.zeros_like(q)
    return jnp.concatenate([jnp.where(row < Q_PER_KV, q, zero), jnp.where(row < Q_PER_KV, zero, q)], axis=-1)


def _select_sample_kernel(q_ref, kc_ref, vc_ref, oc_ref, sel_ref, *, ns_past):
    q2 = _split_groups(q_ref[...])
    bt = q2.shape[0]
    s_c = jnp.einsum("bhk,bck->bhc", q2, kc_ref[...], preferred_element_type=F32) * SCALE
    m_c = jnp.max(s_c, axis=-1, keepdims=True)
    e_c = jnp.exp(s_c - m_c)
    p_c = e_c / jnp.sum(e_c, axis=-1, keepdims=True)
    o2 = jnp.einsum("bhc,bck->bhk", p_c.astype(BF16), vc_ref[...], preferred_element_type=F32)
    row = lax.broadcasted_iota(jnp.int32, (bt, N_HEADS, 1), 1)
    oc_ref[...] = jnp.where(row < Q_PER_KV, o2[..., :HEAD_DIM], o2[..., HEAD_DIM:])

    pair = p_c + jnp.where(row % 2 == 0, pltpu.roll(p_c, N_HEADS - 1, 1), pltpu.roll(p_c, 1, 1))
    p_grp = pair + jnp.where(row % Q_PER_KV < 2, pltpu.roll(pair, N_HEADS - 2, 1), pltpu.roll(pair, 2, 1))
    imp = (p_grp[..., :ns_past] + p_grp[..., ns_past:]).reshape(bt * N_HEADS, ns_past)

    sblk = lax.broadcasted_iota(jnp.int32, imp.shape, 1)
    score = jnp.where(sblk == 0, 1e30, imp)
    sel = _topk_mask(score, min(TOP_N, ns_past + 1) - 1).reshape(bt, N_HEADS, ns_past)
    sel_ref[...] = jnp.concatenate([sel, jnp.zeros((bt, N_HEADS, LANES - ns_past), F32)], axis=-1)


def _select_sample(q3, k_cmp, v_cmp):
    n_batch, nc_past, _ = k_cmp.shape
    ns_past = nc_past // (SEL_BLOCK // CMP_BLOCK)
    assert ns_past <= LANES
    bt = _row_tile(n_batch, 32, 1)
    per_b = lambda rows, width: pl.BlockSpec((bt, rows, width), lambda i: (i, 0, 0))
    return pl.pallas_call(
        functools.partial(_select_sample_kernel, ns_past=ns_past),
        grid=(n_batch // bt,),
        in_specs=[per_b(N_HEADS, HEAD_DIM), per_b(nc_past, KV_WIDTH), per_b(nc_past, KV_WIDTH)],
        out_specs=[per_b(N_HEADS, HEAD_DIM), per_b(N_HEADS, LANES)],
        out_shape=[jax.ShapeDtypeStruct((n_batch, N_HEADS, HEAD_DIM), F32),
                   jax.ShapeDtypeStruct((n_batch, N_HEADS, LANES), F32)],
        compiler_params=_params(("parallel",)),
        name="select_sample",
    )(q3, k_cmp, v_cmp)


def _attn_sample_kernel(pt_ref, *refs, n_pages, page, win_buf, past_len):
    del pt_ref
    q_ref, oc_ref, sel_ref, snew_ref, win_ref, wnew_ref, gate_ref, gain_ref = refs[:8]
    pages = refs[8:8 + n_pages]
    o_ref, nwin_ref = refs[8 + n_pages:]

    q = q_ref[0]
    row = lax.broadcasted_iota(jnp.int32, (N_HEADS, 1), 0)
    q2 = _split_groups(q)
    qf = q.astype(F32)
    o_c = oc_ref[0]

    kpos = lax.broadcasted_iota(jnp.int32, (1, past_len), 1)
    expand = (kpos // SEL_BLOCK == lax.broadcasted_iota(jnp.int32, (LANES, past_len), 0)).astype(BF16)
    chosen = jnp.dot(sel_ref[0].astype(BF16), expand, preferred_element_type=F32) > 0.5

    k_sel = _unit_rows(pages, page, 0)
    v_sel = _unit_rows(pages, page, N_KV)
    s_s = _dot_nt(q2, k_sel) * SCALE
    k_new = _own_group(snew_ref[0, :, :KV_WIDTH], row)
    v_new = _own_group(snew_ref[0, :, KV_WIDTH:], row)
    s_new = jnp.sum(qf * k_new, axis=-1, keepdims=True) * SCALE
    sm = jnp.where(chosen, s_s, NEG_INF)
    m_s = jnp.maximum(jnp.max(sm, axis=-1, keepdims=True), s_new)
    p_s = jnp.where(chosen, jnp.exp(sm - m_s), 0.0)
    p_new = jnp.exp(s_new - m_s)
    l_s = jnp.sum(p_s, axis=-1, keepdims=True) + p_new
    o_s = (_own_group(jnp.dot(p_s.astype(BF16), v_sel, preferred_element_type=F32), row) + p_new * v_new) / l_s

    k_win = _unit_rows([win_ref], win_buf, 0)
    v_win = _unit_rows([win_ref], win_buf, N_KV)
    wpos = past_len - win_buf + lax.broadcasted_iota(jnp.int32, (1, win_buf), 1)
    w_mask = ((past_len - wpos) <= WINDOW) & (wpos >= 0)
    s_w = jnp.where(w_mask, _dot_nt(q2, k_win) * SCALE, NEG_INF)
    kw_new = _own_group(wnew_ref[0, :, :KV_WIDTH], row)
    vw_new = _own_group(wnew_ref[0, :, KV_WIDTH:], row)
    sw_new = jnp.sum(qf * kw_new, axis=-1, keepdims=True) * SCALE
    m_w = jnp.maximum(jnp.max(s_w, axis=-1, keepdims=True), sw_new)
    p_w = jnp.where(w_mask, jnp.exp(s_w - m_w), 0.0)
    pw_new = jnp.exp(sw_new - m_w)
    l_w = jnp.sum(p_w, axis=-1, keepdims=True) + pw_new
    o_w = (_own_group(jnp.dot(p_w.astype(BF16), v_win, preferred_element_type=F32), row) + pw_new * vw_new) / l_w

    g_own = _own_group(gate_ref[0], row)
    lane = lax.broadcasted_iota(jnp.int32, (N_HEADS, LANES), 1)
    first = 3 * (row % Q_PER_KV)
    pick = lambda j: jnp.sum(jnp.where(lane == first + j, g_own, 0.0), axis=-1, keepdims=True)
    o = pick(0) * o_c + pick(1) * o_s + pick(2) * o_w
    ms = jnp.sum(jnp.sum(o * o, axis=-1, keepdims=True), axis=0, keepdims=True) / D_ATTN
    o_ref[0] = (o * lax.rsqrt(ms + EPS) * gain_ref[...]).astype(o_ref.dtype)

    kept = (win_buf - 1) * UNITS
    nwin_ref[0:kept, :] = win_ref[UNITS:win_buf * UNITS, :]
    for u in range(UNITS):
        nwin_ref[kept + u:kept + u + 1, :] = wnew_ref[0, :, u * HEAD_DIM:(u + 1) * HEAD_DIM]


def _attn_sample(q_b, k_cmp, v_cmp, sel_new, cache_sel, state_win, win_new, gates, gain, layer, page_table):
    depth, n_phys, page = cache_sel.shape[:3]
    n_batch, n_pages = page_table.shape
    win_buf = state_win.shape[2]
    past_len = n_pages * page
    q3 = q_b.reshape(n_batch, N_HEADS, HEAD_DIM)
    o_c, sel = _select_sample(q3, k_cmp, v_cmp)
    pages = cache_sel.reshape(depth * n_phys * page * UNITS, HEAD_DIM)
    wins = state_win.reshape(depth * n_batch * win_buf * UNITS, HEAD_DIM)
    base = layer * n_phys
    per_b = lambda shape: pl.BlockSpec((1,) + shape, lambda b, pt: (b, 0, 0))
    page_specs = [pl.BlockSpec((page * UNITS, HEAD_DIM), lambda b, pt, p=p: (base + pt[b, p], 0))
                  for p in range(n_pages)]
    in_specs = [per_b((N_HEADS, HEAD_DIM)), per_b((N_HEADS, HEAD_DIM)), per_b((N_HEADS, LANES)),
                per_b((1, ROW_WIDTH)),
                pl.BlockSpec((win_buf * UNITS, HEAD_DIM), lambda b, pt: (layer * n_batch + b, 0)),
                per_b((1, ROW_WIDTH)), per_b((1, GATE_WIDTH)),
                pl.BlockSpec((N_HEADS, HEAD_DIM), lambda b, pt: (0, 0))] + page_specs
    out, new_win = pl.pallas_call(
        functools.partial(_attn_sample_kernel, n_pages=n_pages, page=page, win_buf=win_buf, past_len=past_len),
        grid_spec=pltpu.PrefetchScalarGridSpec(
            num_scalar_prefetch=1,
            grid=(n_batch,),
            in_specs=in_specs,
            out_specs=[per_b((N_HEADS, HEAD_DIM)), pl.BlockSpec((win_buf * UNITS, HEAD_DIM), lambda b, pt: (b, 0))],
        ),
        out_shape=[jax.ShapeDtypeStruct((n_batch, N_HEADS, HEAD_DIM), BF16),
                   jax.ShapeDtypeStruct((n_batch * win_buf * UNITS, HEAD_DIM), F32)],
        compiler_params=_params(("arbitrary",)),
        name="attn_sample",
    )(page_table, q3, o_c, sel,
      sel_new.reshape(n_batch, 1, ROW_WIDTH), wins, win_new.reshape(n_batch, 1, ROW_WIDTH),
      gates.reshape(n_batch, 1, GATE_WIDTH), gain.reshape(N_HEADS, HEAD_DIM), *([pages] * n_pages))
    return out.reshape(n_batch, D_ATTN), new_win.reshape(n_batch, win_buf, 2, N_KV, HEAD_DIM)


def _lru_gates(xc, wa_ref, wx_ref, ba, bx, lam):
    width = xc.shape[1]
    bd = width // LRU_BLOCKS
    xb = xc.astype(BF16)
    ra = jnp.concatenate([jnp.dot(xb[:, n * bd:(n + 1) * bd], wa_ref[n], preferred_element_type=F32)
                          for n in range(LRU_BLOCKS)], axis=1)
    rx = jnp.concatenate([jnp.dot(xb[:, n * bd:(n + 1) * bd], wx_ref[n], preferred_element_type=F32)
                          for n in range(LRU_BLOCKS)], axis=1)
    r = jax.nn.sigmoid(ra + ba)
    i = jax.nn.sigmoid(rx + bx)
    log_a = -LRU_C * r * _softplus(-lam)
    a = jnp.exp(log_a)
    b = jnp.sqrt(-jnp.tanh(log_a) * (a * a + 1.0)) * (i * xc)
    return a, b


def _scan_rows(a8, b8, h):
    row = lax.broadcasted_iota(jnp.int32, a8.shape, 0)
    s = 1
    while s < SUBLANES:
        a_prev = jnp.where(row >= s, pltpu.roll(a8, s, 0), 1.0)
        b_prev = jnp.where(row >= s, pltpu.roll(b8, s, 0), 0.0)
        b8 = a8 * b_prev + b8
        a8 = a8 * a_prev
        s *= 2
    return a8 * h + b8


def _lru_prompt_kernel(ux_ref, uy_ref, cw_ref, cb_ref, wa_ref, wx_ref, ba_ref, bx_ref, lam_ref, gain_ref,
                       r_ref, h_ref, x_s, h_s, a_s, b_s):
    tt = ux_ref.shape[0]
    i = pl.program_id(1)

    @pl.when(i == 0)
    def _():
        x_s[0:SUBLANES, :] = jnp.zeros((SUBLANES, x_s.shape[1]), F32)
        h_s[...] = jnp.zeros_like(h_s)

    @pl.when(i > 0)
    def _():
        x_s[0:SUBLANES, :] = x_s[tt:tt + SUBLANES, :]

    x_s[SUBLANES:, :] = ux_ref[...]
    xc = cb_ref[...]
    for j in range(CONV_W):
        k = CONV_W - 1 - j
        xc = xc + x_s[SUBLANES - k:SUBLANES - k + tt, :] * cw_ref[j:j + 1, :]
    a, b = _lru_gates(xc, wa_ref, wx_ref, ba_ref[...], bx_ref[...], lam_ref[...])
    a_s[...] = a
    b_s[...] = b

    def group(k, h):
        r0 = pl.multiple_of(k * SUBLANES, SUBLANES)
        hs = _scan_rows(a_s[pl.ds(r0, SUBLANES), :], b_s[pl.ds(r0, SUBLANES), :], h)
        a_s[pl.ds(r0, SUBLANES), :] = hs
        return hs[SUBLANES - 1:SUBLANES, :]

    h_last = lax.fori_loop(0, tt // SUBLANES, group, h_s[...])
    h_s[...] = h_last
    h_ref[0] = h_last
    y = a_s[...] * jax.nn.gelu(uy_ref[...])
    r_ref[...] = _rms(y, gain_ref[...]).astype(r_ref.dtype)


def _lru_prompt(proj, n_batch, seq_len, weights):
    width = D_ATTN
    tt = _row_tile(seq_len, 512, SUBLANES)
    n_t = seq_len // tt
    rows = lambda col: pl.BlockSpec((tt, width), lambda b, i: (b * n_t + i, col))
    vec = _full((1, width))
    blk = _full((LRU_BLOCKS, width // LRU_BLOCKS, width // LRU_BLOCKS))
    return pl.pallas_call(
        _lru_prompt_kernel,
        grid=(n_batch, n_t),
        in_specs=[rows(COL_UX // width), rows(COL_UY // width), _full((CONV_W, width)), vec, blk, blk,
                  vec, vec, vec, vec],
        out_specs=[pl.BlockSpec((tt, width), lambda b, i: (b * n_t + i, 0)),
                   pl.BlockSpec((1, 1, width), lambda b, i: (b, 0, 0))],
        out_shape=[jax.ShapeDtypeStruct((n_batch * seq_len, width), BF16),
                   jax.ShapeDtypeStruct((n_batch, 1, width), F32)],
        scratch_shapes=[pltpu.VMEM((tt + SUBLANES, width), F32), pltpu.VMEM((1, width), F32),
                        pltpu.VMEM((tt, width), F32), pltpu.VMEM((tt, width), F32)],
        compiler_params=_params(("parallel", "arbitrary")),
        name="lru_prompt",
    )(proj, proj, *weights)


def _lru_sample_kernel(ux_ref, uy_ref, c0_ref, c1_ref, c2_ref, h0_ref, cw_ref, cb_ref, wa_ref, wx_ref,
                       ba_ref, bx_ref, lam_ref, gain_ref, r_ref, h_ref):
    xc = (cb_ref[...] + c0_ref[...] * cw_ref[0:1, :] + c1_ref[...] * cw_ref[1:2, :]
          + c2_ref[...] * cw_ref[2:3, :] + ux_ref[...] * cw_ref[3:4, :])
    a, b = _lru_gates(xc, wa_ref, wx_ref, ba_ref[...], bx_ref[...], lam_ref[...])
    h = a * h0_ref[...] + b
    h_ref[...] = h
    r_ref[...] = _rms(h * jax.nn.gelu(uy_ref[...]), gain_ref[...]).astype(r_ref.dtype)


def _lru_sample(ux, uy, conv_state, h0, weights):
    n, width = ux.shape
    mat = _full((n, width))
    vec = _full((1, width))
    blk = _full((LRU_BLOCKS, width // LRU_BLOCKS, width // LRU_BLOCKS))
    return pl.pallas_call(
        _lru_sample_kernel,
        grid=(1,),
        in_specs=[mat] * 6 + [_full((CONV_W, width)), vec, blk, blk, vec, vec, vec, vec],
        out_specs=[mat, mat],
        out_shape=[jax.ShapeDtypeStruct((n, width), BF16), jax.ShapeDtypeStruct((n, width), F32)],
        compiler_params=_params(("arbitrary",)),
        name="lru_sample",
    )(ux, uy, conv_state[:, 0], conv_state[:, 1], conv_state[:, 2], h0, *weights)


def _swiglu_step(x, wg_ref, wu_ref, wd_ref):
    g = jnp.dot(x, wg_ref[...].astype(BF16), preferred_element_type=F32)
    u = jnp.dot(x, wu_ref[...].astype(BF16), preferred_element_type=F32)
    hid = (jax.nn.silu(g) * u).astype(BF16)
    return jnp.dot(hid, wd_ref[...].astype(BF16), preferred_element_type=F32)


def _ffn_kernel(x_ref, g_ref, wg_ref, wu_ref, wd_ref, o_ref, hn_s):
    @pl.when(pl.program_id(1) == 0)
    def _():
        x = x_ref[...]
        o_ref[...] = x
        hn_s[...] = _rms(x, g_ref[...]).astype(hn_s.dtype)

    o_ref[...] += _swiglu_step(hn_s[...], wg_ref, wu_ref, wd_ref)


def _ffn(x, gain, wg, wu, wd):
    n, d = x.shape
    d_ff = wg.shape[1]
    tm = _row_tile(n, 832)
    tf = _row_tile(d_ff, 256, LANES)
    row = pl.BlockSpec((tm, d), lambda i, f: (i, 0))
    return pl.pallas_call(
        _ffn_kernel,
        grid=(n // tm, d_ff // tf),
        in_specs=[row, pl.BlockSpec((1, d), lambda i, f: (0, 0)),
                  pl.BlockSpec((d, tf), lambda i, f: (0, f)), pl.BlockSpec((d, tf), lambda i, f: (0, f)),
                  pl.BlockSpec((tf, d), lambda i, f: (f, 0))],
        out_specs=row,
        out_shape=jax.ShapeDtypeStruct((n, d), F32),
        scratch_shapes=[pltpu.VMEM((tm, d), BF16)],
        compiler_params=_params(("parallel", "arbitrary")),
        name="ffn",
    )(x, gain.reshape(1, d), wg, wu, wd)


MOE_BLOCK = 1280
MOE_TILE = 256
COMBINE_TILE = 320


def _router_kernel(x_ref, g_ref, w_ref, o_ref, *, n_experts):
    hn = _rms(x_ref[...], g_ref[...]).astype(BF16)
    logits = jnp.dot(hn, w_ref[...], preferred_element_type=F32)
    lane = lax.broadcasted_iota(jnp.int32, logits.shape, 1).astype(F32)
    logits = jnp.where(lane < n_experts, logits, -jnp.inf)
    m1 = jnp.max(logits, axis=-1, keepdims=True)
    i1 = jnp.min(jnp.where(logits == m1, lane, float(LANES)), axis=-1, keepdims=True)
    rest = jnp.where(lane == i1, -jnp.inf, logits)
    m2 = jnp.max(rest, axis=-1, keepdims=True)
    i2 = jnp.min(jnp.where(rest == m2, lane, float(LANES)), axis=-1, keepdims=True)
    e2 = jnp.exp(m2 - m1)
    den = 1.0 + e2
    o_ref[...] = (jnp.where(lane == 0.0, 1.0 / den, 0.0) + jnp.where(lane == 1.0, e2 / den, 0.0)
                  + jnp.where(lane == 2.0, i1, 0.0) + jnp.where(lane == 3.0, i2, 0.0))


def _router(x, gain, w_router):
    n, d = x.shape
    n_experts = w_router.shape[1]
    tm = _row_tile(n, 832)
    w = jnp.concatenate([w_router, jnp.zeros((d, LANES - n_experts), w_router.dtype)], axis=1).astype(BF16)
    return pl.pallas_call(
        functools.partial(_router_kernel, n_experts=n_experts),
        grid=(n // tm,),
        in_specs=[pl.BlockSpec((tm, d), lambda i: (i, 0)), _full((1, d)), _full((d, LANES))],
        out_specs=pl.BlockSpec((tm, LANES), lambda i: (i, 0)),
        out_shape=jax.ShapeDtypeStruct((n, LANES), F32),
        compiler_params=_params(("parallel",)),
        name="router",
    )(x, gain.reshape(1, d), w)


def _start_row_gather(ids_ref, first, n_rows, src_hbm, dst, sem):
    def body(r, carry):
        pltpu.make_async_copy(src_hbm.at[pl.ds(ids_ref[first + r], 1)], dst.at[pl.ds(r, 1)], sem).start()
        return carry

    lax.fori_loop(0, n_rows, body, 0, unroll=8)


def _dispatch_kernel(ids_ref, on_ref, x_hbm, g_ref, o_ref, buf, sem):
    i = pl.program_id(0)
    slot = i % 2
    start = lambda tile, s: _start_row_gather(ids_ref, tile * MOE_TILE, MOE_TILE, x_hbm, buf.at[s], sem.at[s])

    @pl.when((i == 0) & (on_ref[0] > 0))
    def _():
        start(0, 0)

    @pl.when(i + 1 < pl.num_programs(0))
    def _():
        @pl.when(on_ref[i + 1] > 0)
        def _():
            start(i + 1, 1 - slot)

    @pl.when(on_ref[i] > 0)
    def _():
        pltpu.make_async_copy(buf.at[slot], buf.at[slot], sem.at[slot]).wait()
        o_ref[...] = _rms(buf[slot], g_ref[...]).astype(o_ref.dtype)

    @pl.when(on_ref[i] == 0)
    def _():
        o_ref[...] = jnp.zeros_like(o_ref)


def _dispatch(x, gain, row_ids, tile_on):
    n, d = x.shape
    n_slots = row_ids.shape[0]
    return pl.pallas_call(
        _dispatch_kernel,
        grid_spec=pltpu.PrefetchScalarGridSpec(
            num_scalar_prefetch=2,
            grid=(n_slots // MOE_TILE,),
            in_specs=[pl.BlockSpec(memory_space=pl.ANY), pl.BlockSpec((1, d), lambda i, ids, on: (0, 0))],
            out_specs=pl.BlockSpec((MOE_TILE, d), lambda i, ids, on: (i, 0)),
            scratch_shapes=[pltpu.VMEM((2, MOE_TILE, d), F32), pltpu.SemaphoreType.DMA((2,))],
        ),
        out_shape=jax.ShapeDtypeStruct((n_slots, d), BF16),
        compiler_params=_params(("arbitrary",)),
        name="moe_dispatch",
    )(row_ids, tile_on, x, gain.reshape(1, d))


def _experts_kernel(be_ref, br_ref, x_ref, w_ref, wg_ref, wu_ref, wd_ref, o_ref):
    b, f = pl.program_id(0), pl.program_id(1)
    rows = br_ref[b]

    @pl.when(f == 0)
    def _():
        o_ref[...] = jnp.zeros_like(o_ref)

    for used in range(MOE_TILE, MOE_BLOCK + 1, MOE_TILE):
        @pl.when(rows == used)
        def _():
            o_ref[0:used, :] += _swiglu_step(x_ref[0:used, :], wg_ref.at[0], wu_ref.at[0], wd_ref.at[0])

    @pl.when((f == pl.num_programs(1) - 1) & (rows > 0))
    def _():
        o_ref[...] = o_ref[...] * w_ref[...]


def _experts(xs, slot_w, blk_expert, blk_rows, wg, wu, wd):
    n_slots, d = xs.shape
    d_ff = wg.shape[2]
    tf = _row_tile(d_ff, 256, LANES)
    n_f = d_ff // tf
    fcol = lambda b, f, br: jnp.where(br[b] > 0, f, n_f - 1)
    return pl.pallas_call(
        _experts_kernel,
        grid_spec=pltpu.PrefetchScalarGridSpec(
            num_scalar_prefetch=2,
            grid=(n_slots // MOE_BLOCK, n_f),
            in_specs=[pl.BlockSpec((MOE_BLOCK, d), lambda b, f, be, br: (b, 0)),
                      pl.BlockSpec((MOE_BLOCK, 1), lambda b, f, be, br: (b, 0)),
                      pl.BlockSpec((1, d, tf), lambda b, f, be, br: (be[b], 0, fcol(b, f, br))),
                      pl.BlockSpec((1, d, tf), lambda b, f, be, br: (be[b], 0, fcol(b, f, br))),
                      pl.BlockSpec((1, tf, d), lambda b, f, be, br: (be[b], fcol(b, f, br), 0))],
            out_specs=pl.BlockSpec((MOE_BLOCK, d), lambda b, f, be, br: (b, 0)),
        ),
        out_shape=jax.ShapeDtypeStruct((n_slots, d), F32),
        compiler_params=_params(("arbitrary", "arbitrary")),
        name="moe_experts",
    )(blk_expert, blk_rows, xs, slot_w, wg, wu, wd)


def _combine_kernel(s0_ref, s1_ref, y_hbm, x_ref, o_ref, buf, sem):
    tc = x_ref.shape[0]
    i = pl.program_id(0)
    slot = i % 2

    def start(tile, s):
        _start_row_gather(s0_ref, tile * tc, tc, y_hbm, buf.at[s, 0], sem.at[s])
        _start_row_gather(s1_ref, tile * tc, tc, y_hbm, buf.at[s, 1], sem.at[s])

    @pl.when(i == 0)
    def _():
        start(0, 0)

    @pl.when(i + 1 < pl.num_programs(0))
    def _():
        start(i + 1, 1 - slot)

    pltpu.make_async_copy(buf.at[slot], buf.at[slot], sem.at[slot]).wait()
    o_ref[...] = x_ref[...] + buf[slot, 0] + buf[slot, 1]


def _combine(x, y, slot0, slot1):
    n, d = x.shape
    tc = _row_tile(n, COMBINE_TILE, SUBLANES)
    row = pl.BlockSpec((tc, d), lambda i, s0, s1: (i, 0))
    return pl.pallas_call(
        _combine_kernel,
        grid_spec=pltpu.PrefetchScalarGridSpec(
            num_scalar_prefetch=2,
            grid=(n // tc,),
            in_specs=[pl.BlockSpec(memory_space=pl.ANY), row],
            out_specs=row,
            scratch_shapes=[pltpu.VMEM((2, 2, tc, d), F32), pltpu.SemaphoreType.DMA((2,))],
        ),
        out_shape=jax.ShapeDtypeStruct((n, d), F32),
        compiler_params=_params(("arbitrary",)),
        name="moe_combine",
    )(slot0, slot1, y, x)


def _routing_tables(route, n_experts):
    n = route.shape[0]
    expert = route[:, 2:4].astype(jnp.int32).T.reshape(-1)
    weight = route[:, 0:2].T.reshape(-1)
    token = jnp.tile(jnp.arange(n, dtype=jnp.int32), 2)
    onehot = (expert[:, None] == jnp.arange(n_experts, dtype=jnp.int32)).astype(jnp.int32)
    rank = jnp.sum((jnp.cumsum(onehot, axis=0) - 1) * onehot, axis=1)
    count = jnp.sum(onehot, axis=0)
    padded = (count + MOE_BLOCK - 1) // MOE_BLOCK * MOE_BLOCK
    ends = jnp.cumsum(padded)
    starts = ends - padded
    slot = starts[expert] + rank
    n_slots = (2 * n + n_experts * (MOE_BLOCK - 1) + MOE_BLOCK - 1) // MOE_BLOCK * MOE_BLOCK
    per_slot = jnp.zeros((n_slots, 2), F32).at[slot].set(jnp.stack([token.astype(F32), weight], axis=1))
    row_ids = per_slot[:, 0].astype(jnp.int32)
    slot_w = per_slot[:, 1:2]
    blk_first = jnp.arange(n_slots // MOE_BLOCK, dtype=jnp.int32) * MOE_BLOCK
    blk_expert = jnp.minimum(jnp.sum((ends[None, :] <= blk_first[:, None]).astype(jnp.int32), axis=1), n_experts - 1)
    used = (count + MOE_TILE - 1) // MOE_TILE * MOE_TILE
    blk_rows = jnp.clip(starts[blk_expert] + used[blk_expert] - blk_first, 0, MOE_BLOCK)
    blk_rows = jnp.where(blk_first < ends[-1], blk_rows, 0).astype(jnp.int32)
    tile_first = jnp.arange(n_slots // MOE_TILE, dtype=jnp.int32) * MOE_TILE
    tile_on = (tile_first % MOE_BLOCK < jnp.repeat(blk_rows, MOE_BLOCK // MOE_TILE)).astype(jnp.int32)
    return row_ids, slot_w, blk_expert, blk_rows, tile_on, slot[:n], slot[n:]


def _moe(x, gain, w_router, wg, wu, wd):
    route = _router(x, gain, w_router)
    row_ids, slot_w, blk_expert, blk_rows, tile_on, slot0, slot1 = _routing_tables(route, w_router.shape[1])
    xs = _dispatch(x, gain, row_ids, tile_on)
    y = _experts(xs, slot_w, blk_expert, blk_rows, wg, wu, wd)
    return _combine(x, y, slot0, slot1)


def kernel(x_prompt, x_sample, cache_cmp, cache_sel, state_win, state_lru, state_conv, page_table, ln_mix, w_in, q_norm, k_norm, cmp_pe, cmp_w1, cmp_w2, conv_w, conv_b, lru_wa, lru_ba, lru_wx, lru_bx, lru_lambda, out_norm_attn, out_norm_lru, w_out, ln_ffn, ffn_w_gate, ffn_w_up, ffn_w_down, moe_router, moe_w_gate, moe_w_up, moe_w_down):
    n_batch, seq_len, d_model = x_prompt.shape
    dec_batch, dec_seq, _ = x_sample.shape
    depth = ln_mix.shape[0]
    page = cache_cmp.shape[2]
    n_pages = page_table.shape[1]
    past_len = n_pages * page
    win_buf = state_win.shape[2]
    lru_width = d_model - D_ATTN
    assert dec_seq == 1 and lru_width == D_ATTN
    assert seq_len % Q_BLOCK == 0 and seq_len >= WIN_SPAN and seq_len % SEL_CHUNK == 0 and seq_len >= win_buf
    assert past_len % Q_BLOCK == 0 and win_buf <= past_len and page % CMP_BLOCK == 0
    n_p = n_batch * seq_len
    nc = seq_len // CMP_BLOCK
    nc_past = past_len // CMP_BLOCK
    kv_row = (2, N_KV, HEAD_DIM)

    pos = jnp.concatenate([jnp.tile(jnp.arange(seq_len), n_batch), jnp.full((dec_batch,), past_len)])
    tables = _rope_tables(pos)
    end_p = _rope_tables((jnp.arange(nc) + 1) * CMP_BLOCK - 1)
    end_s = _rope_tables((jnp.arange(nc_past) + 1) * CMP_BLOCK - 1)

    x = jnp.concatenate([x_prompt.reshape(n_p, d_model), x_sample.reshape(dec_batch, d_model)], axis=0)
    outs = [[] for _ in range(10)]
    for l in range(depth):
        proj = _matmul([x], [_reorder_w_in(w_in[l])], gain=ln_mix[l], tm_target=832, tn_target=1280)
        q_b, sel_f, sel_b, win_f, win_b, gates = _postproj(proj, tables, q_norm[l], k_norm[l, 1], k_norm[l, 2])
        c_rows = proj[:, COL_CMP:COL_CMP + ROW_WIDTH]

        cmp_w = (cmp_pe[l], cmp_w1[l].reshape(2, CMP_BLOCK // 2, 2 * HEAD_DIM, HEAD_DIM).astype(BF16),
                 cmp_w2[l].astype(BF16), k_norm[l, 0].reshape(1, HEAD_DIM))
        lru_w = (conv_w[l], conv_b[l].reshape(1, -1), lru_wa[l].astype(BF16), lru_wx[l].astype(BF16),
                 lru_ba[l].reshape(1, -1), lru_bx[l].reshape(1, -1), lru_lambda[l].reshape(1, -1),
                 out_norm_lru[l].reshape(1, -1))

        kc_p, vc_p = _compress_prompt(c_rows[:n_p].reshape(n_batch * nc, CMP_ROW), n_batch, nc, cmp_w, end_p)
        o_p = _attn_prompt(q_b, _deinterleave_blocks(kc_p, n_batch, nc), _deinterleave_blocks(vc_p, n_batch, nc),
                           sel_b, win_b, gates, out_norm_attn[l], n_batch, seq_len)
        r_p, h_p = _lru_prompt(proj, n_batch, seq_len, lru_w)

        kc_s, vc_s = _compress_sample(cache_cmp, l, page_table, cmp_w, end_s, nb=2 if dec_batch % 2 == 0 else 1)
        o_s, new_win_s = _attn_sample(q_b[n_p:], _deinterleave_blocks(kc_s, dec_batch, nc_past),
                                      _deinterleave_blocks(vc_s, dec_batch, nc_past), sel_f[n_p:], cache_sel,
                                      state_win, win_f[n_p:], gates[n_p:], out_norm_attn[l], l, page_table)
        ux_s = proj[n_p:, COL_UX:COL_UX + lru_width]
        r_s, h_s = _lru_sample(ux_s, proj[n_p:, COL_UY:COL_UY + lru_width], state_conv[l], state_lru[l], lru_w)

        w_o = w_out[l].astype(BF16)
        x = _matmul([jnp.concatenate([o_p, o_s], axis=0), jnp.concatenate([r_p, r_s], axis=0)],
                    [w_o[:D_ATTN], w_o[D_ATTN:]], res=x, tm_target=832, tn_target=1024)

        k = l // 2
        if l % 2 == 0:
            x = _ffn(x, ln_ffn[l], ffn_w_gate[k], ffn_w_up[k], ffn_w_down[k])
        else:
            x = _moe(x, ln_ffn[l], moe_router[k], moe_w_gate[k], moe_w_up[k], moe_w_down[k])

        ux_p = proj[:n_p, COL_UX:COL_UX + lru_width].reshape(n_batch, seq_len, lru_width)
        new = (c_rows[:n_p].reshape((n_batch, seq_len) + kv_row), c_rows[n_p:].reshape((dec_batch, 1) + kv_row),
               sel_f[:n_p].reshape((n_batch, seq_len) + kv_row), sel_f[n_p:].reshape((dec_batch, 1) + kv_row),
               win_f[:n_p].reshape((n_batch, seq_len) + kv_row)[:, seq_len - win_buf:],
               new_win_s,
               h_p.reshape(n_batch, lru_width), h_s,
               ux_p[:, seq_len - (CONV_W - 1):],
               jnp.concatenate([state_conv[l][:, 1:], ux_s[:, None, :]], axis=1))
        for acc, val in zip(outs, new):
            acc.append(val)

    return (x[:n_p].reshape(n_batch, seq_len, d_model), x[n_p:].reshape(dec_batch, 1, d_model),
            *[jnp.stack(v) for v in outs])
```

```python
import functools

import jax
import jax.numpy as jnp
from jax import lax
from jax.experimental import pallas as pl
from jax.experimental.pallas import tpu as pltpu

N_HEADS = 8
HEAD_DIM = 128
N_KV = 2
Q_PER_KV = N_HEADS // N_KV
D_ATTN = N_HEADS * HEAD_DIM
KV_WIDTH = N_KV * HEAD_DIM
ROW_WIDTH = 2 * KV_WIDTH
UNITS = 2 * N_KV
ROT_DIM = HEAD_DIM // 4
ROPE_THETA = 500000.0
CMP_BLOCK = 32
SEL_BLOCK = 64
TOP_N = 16
WINDOW = 512
Q_BLOCK = 128
LRU_BLOCKS = 8
CONV_W = 4
LRU_C = 8.0
TOP_K = 2
EPS = 1e-6
SCALE = HEAD_DIM ** -0.5
NEG_INF = -1e30

LANES = 128
SUBLANES = 8
VMEM_LIMIT = 56 * 1024 * 1024

F32 = jnp.float32
BF16 = jnp.bfloat16


def _row_tile(n, target, mult=16):
    best = None
    for t in range(mult, min(n, target) + 1, mult):
        if n % t == 0:
            best = t
    return best if best is not None else n


def _params(semantics):
    return pltpu.CompilerParams(dimension_semantics=semantics, vmem_limit_bytes=VMEM_LIMIT)


def _full(shape):
    zeros = (0,) * len(shape)
    return pl.BlockSpec(shape, lambda *_: zeros)


def _rms(x, gain):
    return x * lax.rsqrt(jnp.mean(x * x, axis=-1, keepdims=True) + EPS) * gain


def _rope_apply(y, cos_t, sin_lo, sin_hi):
    half = ROT_DIM // 2
    return y * cos_t + pltpu.roll(y, HEAD_DIM - half, 1) * sin_lo + pltpu.roll(y, half, 1) * sin_hi


def _softplus(x):
    return jnp.maximum(x, 0.0) + jnp.log1p(jnp.exp(-jnp.abs(x)))


def _topk_mask(score, n_pick):
    n_lanes = score.shape[-1]
    lane = lax.broadcasted_iota(jnp.int32, score.shape, 1).astype(F32)
    sel = jnp.zeros(score.shape, F32)
    for _ in range(n_pick):
        m = jnp.max(score, axis=-1, keepdims=True)
        idx = jnp.min(jnp.where(score == m, lane, float(n_lanes)), axis=-1, keepdims=True)
        pick = lane == idx
        sel = jnp.where(pick, 1.0, sel)
        score = jnp.where(pick, -2.0, score)
    return sel


def _topk_mask_cols(score, n_pick):
    n_rows = score.shape[0]
    row = lax.broadcasted_iota(jnp.int32, score.shape, 0).astype(F32)
    sel = jnp.zeros(score.shape, F32)
    for _ in range(n_pick):
        m = jnp.max(score, axis=0, keepdims=True)
        idx = jnp.min(jnp.where(score == m, row, float(n_rows)), axis=0, keepdims=True)
        pick = row == idx
        sel = jnp.where(pick, 1.0, sel)
        score = jnp.where(pick, -2.0, score)
    return sel


def _masked_softmax_rows(s, mask):
    sm = jnp.where(mask, s, NEG_INF)
    m = jnp.max(sm, axis=-1, keepdims=True)
    e = jnp.where(mask, jnp.exp(sm - m), 0.0)
    l = jnp.sum(e, axis=-1, keepdims=True)
    return e / jnp.where(l > 0.0, l, 1.0)


def _dot_nt(a, b):
    return lax.dot_general(a, b, (((1,), (1,)), ((), ())), preferred_element_type=F32)


def _matmul_kernel(*refs, n_lhs, has_res, has_gain):
    o_ref = refs[-1]
    lhs = refs[0][...]
    if has_gain:
        lhs = _rms(lhs, refs[2 * n_lhs + has_res][...]).astype(BF16)
    acc = jnp.dot(lhs, refs[n_lhs][...], preferred_element_type=F32)
    for k in range(1, n_lhs):
        acc = acc + jnp.dot(refs[k][...], refs[n_lhs + k][...], preferred_element_type=F32)
    if has_res:
        acc = acc + refs[2 * n_lhs][...]
    o_ref[...] = acc


def _matmul(xs, ws, res=None, gain=None, *, tm_target, tn_target):
    n = xs[0].shape[0]
    d_out = ws[0].shape[1]
    tm = _row_tile(n, tm_target)
    tn = _row_tile(d_out, tn_target, LANES)
    in_specs = [pl.BlockSpec((tm, x.shape[1]), lambda i, j: (i, 0)) for x in xs]
    in_specs += [pl.BlockSpec((w.shape[0], tn), lambda i, j: (0, j)) for w in ws]
    args = list(xs) + list(ws)
    if res is not None:
        in_specs.append(pl.BlockSpec((tm, tn), lambda i, j: (i, j)))
        args.append(res)
    if gain is not None:
        in_specs.append(pl.BlockSpec((1, gain.shape[0]), lambda i, j: (0, 0)))
        args.append(gain.reshape(1, -1))
    return pl.pallas_call(
        functools.partial(_matmul_kernel, n_lhs=len(xs), has_res=res is not None, has_gain=gain is not None),
        grid=(n // tm, d_out // tn),
        in_specs=in_specs,
        out_specs=pl.BlockSpec((tm, tn), lambda i, j: (i, j)),
        out_shape=jax.ShapeDtypeStruct((n, d_out), F32),
        compiler_params=_params(("parallel", "parallel")),
        name="matmul",
    )(*args)


PROJ_WIDTH = 5120
COL_UX, COL_UY, COL_CMP, COL_SEL, COL_WIN, COL_GATE = 1024, 2048, 3072, 3584, 4096, 4608
GATE_WIDTH = 2 * LANES


def _reorder_w_in(w):
    lru = D_ATTN
    c_kv = D_ATTN
    c_g = c_kv + 3 * ROW_WIDTH
    c_ux = c_g + 3 * N_HEADS
    c_uy = c_ux + lru
    n_g = 3 * Q_PER_KV
    zpad = lambda k: jnp.zeros((w.shape[0], k), w.dtype)
    parts = [w[:, :D_ATTN], w[:, c_ux:c_uy], w[:, c_uy:c_uy + lru], w[:, c_kv:c_g],
             w[:, c_g:c_g + n_g], zpad(LANES - n_g), w[:, c_g + n_g:c_ux], zpad(LANES - n_g)]
    width = sum(p.shape[1] for p in parts)
    parts.append(zpad(PROJ_WIDTH - width))
    return jnp.concatenate(parts, axis=1).astype(BF16)


def _postproj_kernel(q_ref, sel_ref, win_ref, gate_ref, cos_ref, slo_ref, shi_ref,
                     qg_ref, ksg_ref, kwg_ref,
                     qo_ref, sel_f_ref, sel_b_ref, win_f_ref, win_b_ref, gate_o_ref):
    cos_t, slo, shi = cos_ref[...], slo_ref[...], shi_ref[...]
    for h in range(N_HEADS):
        cols = slice(h * HEAD_DIM, (h + 1) * HEAD_DIM)
        y = _rope_apply(_rms(q_ref[:, cols], qg_ref[...]), cos_t, slo, shi)
        qo_ref[:, cols] = y.astype(qo_ref.dtype)
    for src, gain_ref, out_f, out_b in ((sel_ref, ksg_ref, sel_f_ref, sel_b_ref),
                                        (win_ref, kwg_ref, win_f_ref, win_b_ref)):
        for g in range(N_KV):
            cols = slice(g * HEAD_DIM, (g + 1) * HEAD_DIM)
            y = _rope_apply(_rms(src[:, cols], gain_ref[...]), cos_t, slo, shi)
            out_f[:, cols] = y
            out_b[:, cols] = y.astype(out_b.dtype)
        v = src[:, KV_WIDTH:]
        out_f[:, KV_WIDTH:] = v
        out_b[:, KV_WIDTH:] = v.astype(out_b.dtype)
    gate_o_ref[...] = jax.nn.sigmoid(gate_ref[...])


def _postproj(proj, tables, q_gain, ks_gain, kw_gain):
    n = proj.shape[0]
    tm = _row_tile(n, 416)
    rows = lambda width, col: pl.BlockSpec((tm, width), lambda i: (i, col))
    tab = pl.BlockSpec((tm, HEAD_DIM), lambda i: (i, 0))
    gain = _full((1, HEAD_DIM))
    out = lambda width: pl.BlockSpec((tm, width), lambda i: (i, 0))
    sds = jax.ShapeDtypeStruct
    return pl.pallas_call(
        _postproj_kernel,
        grid=(n // tm,),
        in_specs=[rows(D_ATTN, 0), rows(ROW_WIDTH, COL_SEL // ROW_WIDTH), rows(ROW_WIDTH, COL_WIN // ROW_WIDTH),
                  rows(GATE_WIDTH, COL_GATE // GATE_WIDTH), tab, tab, tab, gain, gain, gain],
        out_specs=[out(D_ATTN), out(ROW_WIDTH), out(ROW_WIDTH), out(ROW_WIDTH), out(ROW_WIDTH), out(GATE_WIDTH)],
        out_shape=[sds((n, D_ATTN), BF16), sds((n, ROW_WIDTH), F32), sds((n, ROW_WIDTH), BF16),
                   sds((n, ROW_WIDTH), F32), sds((n, ROW_WIDTH), BF16), sds((n, GATE_WIDTH), F32)],
        compiler_params=_params(("parallel",)),
        name="postproj",
    )(proj, proj, proj, proj, *tables, q_gain.reshape(1, -1), ks_gain.reshape(1, -1), kw_gain.reshape(1, -1))


def _rope_tables(pos):
    half = ROT_DIM // 2
    inv_freq = ROPE_THETA ** (-2.0 * jnp.arange(half, dtype=F32) / ROT_DIM)
    ang = pos.astype(F32)[:, None] * inv_freq
    cos, sin = jnp.cos(ang), jnp.sin(ang)
    n = pos.shape[0]
    ones = jnp.ones((n, HEAD_DIM - ROT_DIM), F32)
    zeros = jnp.zeros((n, HEAD_DIM - half), F32)
    cos_t = jnp.concatenate([cos, cos, ones], axis=1)
    sin_lo = jnp.concatenate([-sin, zeros], axis=1)
    sin_hi = jnp.concatenate([jnp.zeros((n, half), F32), sin, jnp.zeros((n, HEAD_DIM - ROT_DIM), F32)], axis=1)
    return cos_t, sin_lo, sin_hi


CMP_ROW = CMP_BLOCK * ROW_WIDTH


def _compress_rows(piece, n_rows, group, w1_ref, w2_ref, gain_ref, cos_ref, slo_ref, shi_ref, k_ref, v_ref, perm_s):
    def store(out_ref, cols, part):
        perm_s[...] = part
        half = group // 2
        for b0 in range(0, n_rows, group):
            out_ref[b0:b0 + half, cols] = perm_s[pl.ds(b0, half, stride=2), :].astype(out_ref.dtype)
            out_ref[b0 + half:b0 + group, cols] = perm_s[pl.ds(b0 + 1, half, stride=2), :].astype(out_ref.dtype)

    for n in range(2):
        acc = jnp.zeros((N_KV * n_rows, HEAD_DIM), F32)
        for tp in range(CMP_BLOCK // 2):
            lhs = jnp.concatenate([jnp.concatenate([piece(n, 2 * tp, g), piece(n, 2 * tp + 1, g)], axis=1)
                                   for g in range(N_KV)], axis=0)
            acc = acc + jnp.dot(lhs, w1_ref[n, tp], preferred_element_type=F32)
        hid = jax.nn.gelu(acc).astype(BF16)
        comp = jnp.dot(hid, w2_ref[n], preferred_element_type=F32)
        for g in range(N_KV):
            part = comp[g * n_rows:(g + 1) * n_rows]
            cols = slice(g * HEAD_DIM, (g + 1) * HEAD_DIM)
            if n == 0:
                store(k_ref, cols, _rope_apply(_rms(part, gain_ref[...]), cos_ref[...], slo_ref[...], shi_ref[...]))
            else:
                store(v_ref, cols, part)


def _compress_prompt_kernel(x_ref, pe_ref, *rest):
    def piece(n, t, g):
        c0 = t * ROW_WIDTH + n * KV_WIDTH + g * HEAD_DIM
        return (x_ref[:, c0:c0 + HEAD_DIM] + pe_ref[n, t:t + 1, :]).astype(BF16)

    _compress_rows(piece, x_ref.shape[0], x_ref.shape[0], *rest)


def _compress_sample_kernel(pt_ref, *refs, n_page_refs, page, group):
    del pt_ref
    pages = refs[:n_page_refs]
    pe_ref = refs[n_page_refs]
    rest = refs[n_page_refs + 1:-2] + refs[-1:]
    y_s = refs[-2]
    n_tok = 2 * page
    out_row = lax.broadcasted_iota(jnp.int32, (n_tok, n_tok), 0)
    src_tok = lax.broadcasted_iota(jnp.int32, (n_tok, n_tok), 1)
    regroup = ((out_row % SUBLANES) * CMP_BLOCK + out_row // SUBLANES == src_tok).astype(BF16)
    pe_tok = [jnp.concatenate([jnp.concatenate([pe_ref[n]] * N_KV, axis=1)] * (n_tok // CMP_BLOCK), axis=0)
              for n in range(2)]
    for pair in range(n_page_refs // 2):
        for n in range(2):
            tok = jnp.concatenate([jnp.concatenate([pages[2 * pair + k][pl.ds(n * N_KV + g, page, stride=UNITS), :]
                                                    for g in range(N_KV)], axis=1) for k in range(2)], axis=0)
            y = jnp.dot(regroup, (tok + pe_tok[n]).astype(BF16), preferred_element_type=F32)
            y_s[n, :, pair * SUBLANES:(pair + 1) * SUBLANES, :] = y.reshape(CMP_BLOCK, SUBLANES, KV_WIDTH)

    _compress_rows(lambda n, t, g: y_s[n, t, :, g * HEAD_DIM:(g + 1) * HEAD_DIM].astype(BF16), y_s.shape[2], group,
                   *rest)


def _compress_weight_specs():
    return [_full((2, CMP_BLOCK, HEAD_DIM)), _full((2, CMP_BLOCK // 2, 2 * HEAD_DIM, HEAD_DIM)),
            _full((2, HEAD_DIM, HEAD_DIM)), _full((1, HEAD_DIM))]


def _compress_prompt(c_rows, n_batch, nc, weights, end_tables):
    tab = pl.BlockSpec((nc, HEAD_DIM), lambda i: (0, 0))
    out = pl.BlockSpec((nc, KV_WIDTH), lambda i: (i, 0))
    sds = jax.ShapeDtypeStruct((n_batch * nc, KV_WIDTH), BF16)
    k_cmp, v_cmp = pl.pallas_call(
        _compress_prompt_kernel,
        grid=(n_batch,),
        in_specs=[pl.BlockSpec((nc, CMP_ROW), lambda i: (i, 0))] + _compress_weight_specs() + [tab, tab, tab],
        out_specs=[out, out],
        out_shape=[sds, sds],
        scratch_shapes=[pltpu.VMEM((nc, HEAD_DIM), F32)],
        compiler_params=_params(("parallel",)),
        name="compress_prompt",
    )(c_rows, *weights, *end_tables)
    return k_cmp.reshape(n_batch, nc, KV_WIDTH), v_cmp.reshape(n_batch, nc, KV_WIDTH)


def _compress_sample(cache, layer, page_table, weights, end_tables, nb):
    depth, n_phys, page = cache.shape[:3]
    n_batch, n_pages = page_table.shape
    assert n_pages % 2 == 0 and 2 * page == SUBLANES * CMP_BLOCK
    nc = n_pages * page // CMP_BLOCK
    pages = cache.reshape(depth * n_phys * page * UNITS, HEAD_DIM)
    base = layer * n_phys
    page_specs = [pl.BlockSpec((page * UNITS, HEAD_DIM), lambda i, pt, bb=bb, p=p: (base + pt[i * nb + bb, p], 0))
                  for bb in range(nb) for p in range(n_pages)]
    n_rows = nb * nc
    tab = pl.BlockSpec((n_rows, HEAD_DIM), lambda i, pt: (0, 0))
    wspecs = [pl.BlockSpec(s.block_shape, lambda i, pt, z=(0,) * len(s.block_shape): z) for s in _compress_weight_specs()]
    out = pl.BlockSpec((n_rows, KV_WIDTH), lambda i, pt: (i, 0))
    sds = jax.ShapeDtypeStruct((n_batch * nc, KV_WIDTH), BF16)
    tables = [jnp.tile(t, (nb, 1)) for t in end_tables]
    k_cmp, v_cmp = pl.pallas_call(
        functools.partial(_compress_sample_kernel, n_page_refs=nb * n_pages, page=page, group=nc),
        grid_spec=pltpu.PrefetchScalarGridSpec(
            num_scalar_prefetch=1,
            grid=(n_batch // nb,),
            in_specs=page_specs + wspecs + [tab, tab, tab],
            out_specs=[out, out],
            scratch_shapes=[pltpu.VMEM((2, CMP_BLOCK, n_rows, KV_WIDTH), F32), pltpu.VMEM((n_rows, HEAD_DIM), F32)],
        ),
        out_shape=[sds, sds],
        compiler_params=_params(("arbitrary",)),
        name="compress_sample",
    )(page_table, *([pages] * (nb * n_pages)), *weights, *tables)
    return k_cmp.reshape(n_batch, nc, KV_WIDTH), v_cmp.reshape(n_batch, nc, KV_WIDTH)


SEL_CHUNK = 256
WIN_SPAN = WINDOW + Q_BLOCK


def _attn_prompt_kernel(q_ref, kc_ref, vc_ref, sel_ref, win_ref, gate_ref, gain_ref, o_ref, *, seq_len):
    nc = seq_len // CMP_BLOCK
    ns = seq_len // SEL_BLOCK
    i = pl.program_id(1)
    q0 = i * Q_BLOCK
    tpos = q0 + lax.broadcasted_iota(jnp.int32, (Q_BLOCK, 1), 0)
    tlane = q0 + lax.broadcasted_iota(jnp.int32, (1, Q_BLOCK), 1)
    kcol = lambda g: slice(g * HEAD_DIM, (g + 1) * HEAD_DIM)
    vcol = lambda g: slice(KV_WIDTH + g * HEAD_DIM, KV_WIDTH + (g + 1) * HEAD_DIM)
    rows = lambda h: slice(h * Q_BLOCK, (h + 1) * Q_BLOCK)
    n_rows = Q_PER_KV * Q_BLOCK

    qs, o_c, o_w, sel = [], [], [], []
    for g in range(N_KV):
        q_g = jnp.concatenate([q_ref[:, (g * Q_PER_KV + h) * HEAD_DIM:(g * Q_PER_KV + h + 1) * HEAD_DIM]
                               for h in range(Q_PER_KV)], axis=0)
        qs.append(q_g)

        col = lax.broadcasted_iota(jnp.int32, (Q_BLOCK, nc), 1)
        blk = jnp.where(col < ns, 2 * col, 2 * (col - ns) + 1)
        c_mask = ((blk + 1) * CMP_BLOCK - 1) <= tpos
        s_c = (_dot_nt(q_g, kc_ref[0, :, kcol(g)]) * SCALE).reshape(Q_PER_KV, Q_BLOCK, nc)
        p_c = _masked_softmax_rows(s_c, c_mask[None])
        o_c.append(jnp.dot(p_c.reshape(n_rows, nc).astype(BF16), vc_ref[0, :, kcol(g)], preferred_element_type=F32))
        p_tok = p_c[0] + p_c[1] + p_c[2] + p_c[3]

        p_blk = p_tok.T
        imp = p_blk[:ns] + p_blk[ns:]
        sblk = lax.broadcasted_iota(jnp.int32, (ns, Q_BLOCK), 0)
        allowed = sblk * SEL_BLOCK <= tlane
        forced = (sblk == 0) | (sblk == tlane // SEL_BLOCK)
        score = jnp.where(forced, 1e30, jnp.where(allowed, imp, -1.0))
        picked = _topk_mask_cols(score, min(TOP_N, ns))
        picked = jnp.concatenate([picked, jnp.zeros((LANES - ns, Q_BLOCK), F32)], axis=0)
        sel.append(picked.T.astype(BF16))

        w0 = pl.multiple_of(jnp.maximum(q0 - WINDOW, 0), Q_BLOCK)
        wpos = w0 + lax.broadcasted_iota(jnp.int32, (1, WIN_SPAN), 1)
        dist = tpos - wpos
        w_mask = ((dist >= 0) & (dist <= WINDOW))[None]
        s_w = (_dot_nt(q_g, win_ref[pl.ds(w0, WIN_SPAN), kcol(g)]) * SCALE).reshape(Q_PER_KV, Q_BLOCK, WIN_SPAN)
        p_w = _masked_softmax_rows(s_w, w_mask)
        o_w.append(jnp.dot(p_w.reshape(n_rows, WIN_SPAN).astype(BF16), win_ref[pl.ds(w0, WIN_SPAN), vcol(g)],
                           preferred_element_type=F32))

    def sel_step(j, carry):
        k0 = pl.multiple_of(j * SEL_CHUNK, SEL_CHUNK)
        kpos = k0 + lax.broadcasted_iota(jnp.int32, (1, SEL_CHUNK), 1)
        expand = (kpos // SEL_BLOCK == lax.broadcasted_iota(jnp.int32, (LANES, SEL_CHUNK), 0)).astype(BF16)
        causal = kpos <= tpos
        out = []
        for g in range(N_KV):
            mask = (jnp.dot(sel[g], expand, preferred_element_type=F32) > 0.5) & causal
            bias = jnp.where(mask, 0.0, NEG_INF)
            k_blk = sel_ref[pl.ds(k0, SEL_CHUNK), kcol(g)]
            v_blk = sel_ref[pl.ds(k0, SEL_CHUNK), vcol(g)]
            heads = []
            for h in range(Q_PER_KV):
                m, l, acc = carry[g][h]
                sm = _dot_nt(qs[g][rows(h)], k_blk) * SCALE + bias
                m_new = jnp.maximum(m, jnp.max(sm, axis=-1, keepdims=True))
                p = jnp.exp(sm - m_new)
                alpha = jnp.exp(m - m_new)
                l = alpha * l + jnp.sum(p, axis=-1, keepdims=True)
                acc = alpha * acc + jnp.dot(p.astype(BF16), v_blk, preferred_element_type=F32)
                heads.append((m_new, l, acc))
            out.append(tuple(heads))
        return tuple(out)

    n_steps = (q0 + Q_BLOCK + SEL_CHUNK - 1) // SEL_CHUNK
    head0 = (jnp.full((Q_BLOCK, 1), NEG_INF, F32), jnp.zeros((Q_BLOCK, 1), F32), jnp.zeros((Q_BLOCK, HEAD_DIM), F32))
    swept = lax.fori_loop(0, n_steps, sel_step, ((head0,) * Q_PER_KV,) * N_KV)

    gates = gate_ref[...]
    slabs = []
    for g in range(N_KV):
        for h in range(Q_PER_KV):
            o_s = swept[g][h][2] / swept[g][h][1]
            c = g * LANES + 3 * h
            slabs.append(gates[:, c:c + 1] * o_c[g][rows(h)] + gates[:, c + 1:c + 2] * o_s
                         + gates[:, c + 2:c + 3] * o_w[g][rows(h)])
    o = jnp.concatenate(slabs, axis=1)
    o_ref[...] = _rms(o, gain_ref[...]).astype(o_ref.dtype)


def _attn_prompt(q_b, k_cmp, v_cmp, sel_b, win_b, gates, gain, n_batch, seq_len):
    n_qb = seq_len // Q_BLOCK
    nc = seq_len // CMP_BLOCK
    qrow = lambda width: pl.BlockSpec((Q_BLOCK, width), lambda b, i: (b * n_qb + i, 0))
    cmp_spec = pl.BlockSpec((1, nc, KV_WIDTH), lambda b, i: (b, 0, 0))
    seq_spec = pl.BlockSpec((seq_len, ROW_WIDTH), lambda b, i: (b, 0))
    return pl.pallas_call(
        functools.partial(_attn_prompt_kernel, seq_len=seq_len),
        grid=(n_batch, n_qb),
        in_specs=[qrow(D_ATTN), cmp_spec, cmp_spec, seq_spec, seq_spec, qrow(GATE_WIDTH), _full((1, D_ATTN))],
        out_specs=qrow(D_ATTN),
        out_shape=jax.ShapeDtypeStruct((n_batch * seq_len, D_ATTN), BF16),
        compiler_params=_params(("parallel", "arbitrary")),
        name="attn_prompt",
    )(q_b, k_cmp, v_cmp, sel_b, win_b, gates, gain.reshape(1, -1))


def _unit_rows(refs, n_tokens, first_unit):
    cols = [jnp.concatenate([r[pl.ds(first_unit + g, n_tokens, stride=UNITS), :] for r in refs], axis=0)
            for g in range(N_KV)]
    return jnp.concatenate(cols, axis=1).astype(BF16)


def _own_group(x, row):
    return jnp.where(row < Q_PER_KV, x[:, :HEAD_DIM], x[:, HEAD_DIM:])


def _split_groups(q):
    row = lax.broadcasted_iota(jnp.int32, q.shape[:-1] + (1,), q.ndim - 2)
    zero = jnp.zeros_like(q)
    return jnp.concatenate([jnp.where(row < Q_PER_KV, q, zero), jnp.where(row < Q_PER_KV, zero, q)], axis=-1)


def _select_sample_kernel(q_ref, kc_ref, vc_ref, oc_ref, sel_ref, *, ns_past):
    q2 = _split_groups(q_ref[...])
    bt = q2.shape[0]
    s_c = jnp.einsum("bhk,bck->bhc", q2, kc_ref[...], preferred_element_type=F32) * SCALE
    m_c = jnp.max(s_c, axis=-1, keepdims=True)
    e_c = jnp.exp(s_c - m_c)
    p_c = e_c / jnp.sum(e_c, axis=-1, keepdims=True)
    o2 = jnp.einsum("bhc,bck->bhk", p_c.astype(BF16), vc_ref[...], preferred_element_type=F32)
    row = lax.broadcasted_iota(jnp.int32, (bt, N_HEADS, 1), 1)
    oc_ref[...] = jnp.where(row < Q_PER_KV, o2[..., :HEAD_DIM], o2[..., HEAD_DIM:])

    pair = p_c + jnp.where(row % 2 == 0, pltpu.roll(p_c, N_HEADS - 1, 1), pltpu.roll(p_c, 1, 1))
    p_grp = pair + jnp.where(row % Q_PER_KV < 2, pltpu.roll(pair, N_HEADS - 2, 1), pltpu.roll(pair, 2, 1))
    imp = (p_grp[..., :ns_past] + p_grp[..., ns_past:]).reshape(bt * N_HEADS, ns_past)

    sblk = lax.broadcasted_iota(jnp.int32, imp.shape, 1)
    score = jnp.where(sblk == 0, 1e30, imp)
    sel = _topk_mask(score, min(TOP_N, ns_past + 1) - 1).reshape(bt, N_HEADS, ns_past)
    sel_ref[...] = jnp.concatenate([sel, jnp.zeros((bt, N_HEADS, LANES - ns_past), F32)], axis=-1)


def _select_sample(q3, k_cmp, v_cmp):
    n_batch, nc_past, _ = k_cmp.shape
    ns_past = nc_past // (SEL_BLOCK // CMP_BLOCK)
    assert ns_past <= LANES
    bt = _row_tile(n_batch, 32, 1)
    per_b = lambda rows, width: pl.BlockSpec((bt, rows, width), lambda i: (i, 0, 0))
    return pl.pallas_call(
        functools.partial(_select_sample_kernel, ns_past=ns_past),
        grid=(n_batch // bt,),
        in_specs=[per_b(N_HEADS, HEAD_DIM), per_b(nc_past, KV_WIDTH), per_b(nc_past, KV_WIDTH)],
        out_specs=[per_b(N_HEADS, HEAD_DIM), per_b(N_HEADS, LANES)],
        out_shape=[jax.ShapeDtypeStruct((n_batch, N_HEADS, HEAD_DIM), F32),
                   jax.ShapeDtypeStruct((n_batch, N_HEADS, LANES), F32)],
        compiler_params=_params(("parallel",)),
        name="select_sample",
    )(q3, k_cmp, v_cmp)


def _attn_sample_kernel(pt_ref, *refs, n_pages, page, win_buf, past_len):
    del pt_ref
    q_ref, oc_ref, sel_ref, snew_ref, win_ref, wnew_ref, gate_ref, gain_ref = refs[:8]
    pages = refs[8:8 + n_pages]
    o_ref, nwin_ref = refs[-2:]

    q = q_ref[0]
    row = lax.broadcasted_iota(jnp.int32, (N_HEADS, 1), 0)
    q2 = _split_groups(q)
    qf = q.astype(F32)
    o_c = oc_ref[0]

    kpos = lax.broadcasted_iota(jnp.int32, (1, past_len), 1)
    expand = (kpos // SEL_BLOCK == lax.broadcasted_iota(jnp.int32, (LANES, past_len), 0)).astype(BF16)
    chosen = jnp.dot(sel_ref[0].astype(BF16), expand, preferred_element_type=F32) > 0.5

    k_sel = _unit_rows(pages, page, 0)
    v_sel = _unit_rows(pages, page, N_KV)
    s_s = _dot_nt(q2, k_sel) * SCALE
    k_new = _own_group(snew_ref[0, :, :KV_WIDTH], row)
    v_new = _own_group(snew_ref[0, :, KV_WIDTH:], row)
    s_new = jnp.sum(qf * k_new, axis=-1, keepdims=True) * SCALE
    sm = jnp.where(chosen, s_s, NEG_INF)
    m_s = jnp.maximum(jnp.max(sm, axis=-1, keepdims=True), s_new)
    p_s = jnp.where(chosen, jnp.exp(sm - m_s), 0.0)
    p_new = jnp.exp(s_new - m_s)
    l_s = jnp.sum(p_s, axis=-1, keepdims=True) + p_new
    o_s = (_own_group(jnp.dot(p_s.astype(BF16), v_sel, preferred_element_type=F32), row) + p_new * v_new) / l_s

    k_win = _unit_rows([win_ref], win_buf, 0)
    v_win = _unit_rows([win_ref], win_buf, N_KV)
    wpos = past_len - win_buf + lax.broadcasted_iota(jnp.int32, (1, win_buf), 1)
    w_mask = ((past_len - wpos) <= WINDOW) & (wpos >= 0)
    s_w = jnp.where(w_mask, _dot_nt(q2, k_win) * SCALE, NEG_INF)
    kw_new = _own_group(wnew_ref[0, :, :KV_WIDTH], row)
    vw_new = _own_group(wnew_ref[0, :, KV_WIDTH:], row)
    sw_new = jnp.sum(qf * kw_new, axis=-1, keepdims=True) * SCALE
    m_w = jnp.maximum(jnp.max(s_w, axis=-1, keepdims=True), sw_new)
    p_w = jnp.where(w_mask, jnp.exp(s_w - m_w), 0.0)
    pw_new = jnp.exp(sw_new - m_w)
    l_w = jnp.sum(p_w, axis=-1, keepdims=True) + pw_new
    o_w = (_own_group(jnp.dot(p_w.astype(BF16), v_win, preferred_element_type=F32), row) + pw_new * vw_new) / l_w

    g_own = _own_group(gate_ref[0], row)
    lane = lax.broadcasted_iota(jnp.int32, (N_HEADS, LANES), 1)
    first = 3 * (row % Q_PER_KV)
    pick = lambda j: jnp.sum(jnp.where(lane == first + j, g_own, 0.0), axis=-1, keepdims=True)
    o = pick(0) * o_c + pick(1) * o_s + pick(2) * o_w
    ms = jnp.sum(jnp.sum(o * o, axis=-1, keepdims=True), axis=0, keepdims=True) / D_ATTN
    o_ref[0] = (o * lax.rsqrt(ms + EPS) * gain_ref[...]).astype(o_ref.dtype)

    kept = (win_buf - 1) * UNITS
    nwin_ref[0:kept, :] = win_ref[UNITS:win_buf * UNITS, :]
    for u in range(UNITS):
        nwin_ref[kept + u:kept + u + 1, :] = wnew_ref[0, :, u * HEAD_DIM:(u + 1) * HEAD_DIM]


def _attn_sample(q_b, k_cmp, v_cmp, sel_new, cache_sel, state_win, win_new, gates, gain, layer, page_table, win_acc):
    depth, n_phys, page = cache_sel.shape[:3]
    n_batch, n_pages = page_table.shape
    win_buf = state_win.shape[2]
    past_len = n_pages * page
    q3 = q_b.reshape(n_batch, N_HEADS, HEAD_DIM)
    o_c, sel = _select_sample(q3, k_cmp, v_cmp)
    pages = cache_sel.reshape(depth * n_phys * page * UNITS, HEAD_DIM)
    wins = state_win.reshape(depth * n_batch * win_buf * UNITS, HEAD_DIM)
    base = layer * n_phys
    per_b = lambda shape: pl.BlockSpec((1,) + shape, lambda b, pt: (b, 0, 0))
    page_specs = [pl.BlockSpec((page * UNITS, HEAD_DIM), lambda b, pt, p=p: (base + pt[b, p], 0))
                  for p in range(n_pages)]
    in_specs = [per_b((N_HEADS, HEAD_DIM)), per_b((N_HEADS, HEAD_DIM)), per_b((N_HEADS, LANES)),
                per_b((1, ROW_WIDTH)),
                pl.BlockSpec((win_buf * UNITS, HEAD_DIM), lambda b, pt: (layer * n_batch + b, 0)),
                per_b((1, ROW_WIDTH)), per_b((1, GATE_WIDTH)),
                pl.BlockSpec((N_HEADS, HEAD_DIM), lambda b, pt: (0, 0))] + page_specs
    args = [page_table, q3, o_c, sel, sel_new.reshape(n_batch, 1, ROW_WIDTH), wins,
            win_new.reshape(n_batch, 1, ROW_WIDTH), gates.reshape(n_batch, 1, GATE_WIDTH),
            gain.reshape(N_HEADS, HEAD_DIM)] + [pages] * n_pages + [win_acc]
    in_specs.append(pl.BlockSpec(memory_space=pl.ANY))
    out, win_acc = pl.pallas_call(
        functools.partial(_attn_sample_kernel, n_pages=n_pages, page=page, win_buf=win_buf, past_len=past_len),
        grid_spec=pltpu.PrefetchScalarGridSpec(
            num_scalar_prefetch=1,
            grid=(n_batch,),
            in_specs=in_specs,
            out_specs=[per_b((N_HEADS, HEAD_DIM)),
                       pl.BlockSpec((win_buf * UNITS, HEAD_DIM), lambda b, pt: (layer * n_batch + b, 0))],
        ),
        out_shape=[jax.ShapeDtypeStruct((n_batch, N_HEADS, HEAD_DIM), BF16),
                   jax.ShapeDtypeStruct(win_acc.shape, F32)],
        input_output_aliases={len(args) - 1: 1},
        compiler_params=_params(("arbitrary",)),
        name="attn_sample",
    )(*args)
    return out.reshape(n_batch, D_ATTN), win_acc


def _lru_gates(xc, wa_ref, wx_ref, ba, bx, lam):
    width = xc.shape[1]
    bd = width // LRU_BLOCKS
    xb = xc.astype(BF16)
    ra = jnp.concatenate([jnp.dot(xb[:, n * bd:(n + 1) * bd], wa_ref[n], preferred_element_type=F32)
                          for n in range(LRU_BLOCKS)], axis=1)
    rx = jnp.concatenate([jnp.dot(xb[:, n * bd:(n + 1) * bd], wx_ref[n], preferred_element_type=F32)
                          for n in range(LRU_BLOCKS)], axis=1)
    r = jax.nn.sigmoid(ra + ba)
    i = jax.nn.sigmoid(rx + bx)
    log_a = -LRU_C * r * _softplus(-lam)
    a = jnp.exp(log_a)
    b = jnp.sqrt(-jnp.tanh(log_a) * (a * a + 1.0)) * (i * xc)
    return a, b


def _scan_rows(a8, b8, h):
    row = lax.broadcasted_iota(jnp.int32, a8.shape, 0)
    s = 1
    while s < SUBLANES:
        a_prev = jnp.where(row >= s, pltpu.roll(a8, s, 0), 1.0)
        b_prev = jnp.where(row >= s, pltpu.roll(b8, s, 0), 0.0)
        b8 = a8 * b_prev + b8
        a8 = a8 * a_prev
        s *= 2
    return a8 * h + b8


def _lru_prompt_kernel(ux_ref, uy_ref, cw_ref, cb_ref, wa_ref, wx_ref, ba_ref, bx_ref, lam_ref, gain_ref,
                       r_ref, h_ref, x_s, h_s, a_s, b_s):
    tt = ux_ref.shape[0]
    i = pl.program_id(1)

    @pl.when(i == 0)
    def _():
        x_s[0:SUBLANES, :] = jnp.zeros((SUBLANES, x_s.shape[1]), F32)
        h_s[...] = jnp.zeros_like(h_s)

    @pl.when(i > 0)
    def _():
        x_s[0:SUBLANES, :] = x_s[tt:tt + SUBLANES, :]

    x_s[SUBLANES:, :] = ux_ref[...]
    xc = cb_ref[...]
    for j in range(CONV_W):
        k = CONV_W - 1 - j
        xc = xc + x_s[SUBLANES - k:SUBLANES - k + tt, :] * cw_ref[j:j + 1, :]
    a, b = _lru_gates(xc, wa_ref, wx_ref, ba_ref[...], bx_ref[...], lam_ref[...])
    a_s[...] = a
    b_s[...] = b

    def group(k, h):
        r0 = pl.multiple_of(k * SUBLANES, SUBLANES)
        hs = _scan_rows(a_s[pl.ds(r0, SUBLANES), :], b_s[pl.ds(r0, SUBLANES), :], h)
        a_s[pl.ds(r0, SUBLANES), :] = hs
        return hs[SUBLANES - 1:SUBLANES, :]

    h_last = lax.fori_loop(0, tt // SUBLANES, group, h_s[...])
    h_s[...] = h_last
    h_ref[0] = h_last
    y = a_s[...] * jax.nn.gelu(uy_ref[...])
    r_ref[...] = _rms(y, gain_ref[...]).astype(r_ref.dtype)


def _lru_prompt(proj, n_batch, seq_len, weights):
    width = D_ATTN
    tt = _row_tile(seq_len, 512, SUBLANES)
    n_t = seq_len // tt
    rows = lambda col: pl.BlockSpec((tt, width), lambda b, i: (b * n_t + i, col))
    vec = _full((1, width))
    blk = _full((LRU_BLOCKS, width // LRU_BLOCKS, width // LRU_BLOCKS))
    return pl.pallas_call(
        _lru_prompt_kernel,
        grid=(n_batch, n_t),
        in_specs=[rows(COL_UX // width), rows(COL_UY // width), _full((CONV_W, width)), vec, blk, blk,
                  vec, vec, vec, vec],
        out_specs=[pl.BlockSpec((tt, width), lambda b, i: (b * n_t + i, 0)),
                   pl.BlockSpec((1, 1, width), lambda b, i: (b, 0, 0))],
        out_shape=[jax.ShapeDtypeStruct((n_batch * seq_len, width), BF16),
                   jax.ShapeDtypeStruct((n_batch, 1, width), F32)],
        scratch_shapes=[pltpu.VMEM((tt + SUBLANES, width), F32), pltpu.VMEM((1, width), F32),
                        pltpu.VMEM((tt, width), F32), pltpu.VMEM((tt, width), F32)],
        compiler_params=_params(("parallel", "arbitrary")),
        name="lru_prompt",
    )(proj, proj, *weights)


def _lru_sample_kernel(ux_ref, uy_ref, c0_ref, c1_ref, c2_ref, h0_ref, cw_ref, cb_ref, wa_ref, wx_ref,
                       ba_ref, bx_ref, lam_ref, gain_ref, r_ref, h_ref):
    xc = (cb_ref[...] + c0_ref[...] * cw_ref[0:1, :] + c1_ref[...] * cw_ref[1:2, :]
          + c2_ref[...] * cw_ref[2:3, :] + ux_ref[...] * cw_ref[3:4, :])
    a, b = _lru_gates(xc, wa_ref, wx_ref, ba_ref[...], bx_ref[...], lam_ref[...])
    h = a * h0_ref[...] + b
    h_ref[...] = h
    r_ref[...] = _rms(h * jax.nn.gelu(uy_ref[...]), gain_ref[...]).astype(r_ref.dtype)


def _lru_sample(ux, uy, conv_state, h0, weights):
    n, width = ux.shape
    mat = _full((n, width))
    vec = _full((1, width))
    blk = _full((LRU_BLOCKS, width // LRU_BLOCKS, width // LRU_BLOCKS))
    return pl.pallas_call(
        _lru_sample_kernel,
        grid=(1,),
        in_specs=[mat] * 6 + [_full((CONV_W, width)), vec, blk, blk, vec, vec, vec, vec],
        out_specs=[mat, mat],
        out_shape=[jax.ShapeDtypeStruct((n, width), BF16), jax.ShapeDtypeStruct((n, width), F32)],
        compiler_params=_params(("arbitrary",)),
        name="lru_sample",
    )(ux, uy, conv_state[:, 0], conv_state[:, 1], conv_state[:, 2], h0, *weights)


def _swiglu_step(x, wg_ref, wu_ref, wd_ref):
    g = jnp.dot(x, wg_ref[...].astype(BF16), preferred_element_type=F32)
    u = jnp.dot(x, wu_ref[...].astype(BF16), preferred_element_type=F32)
    hid = (jax.nn.silu(g) * u).astype(BF16)
    return jnp.dot(hid, wd_ref[...].astype(BF16), preferred_element_type=F32)


def _ffn_kernel(x_ref, g_ref, wg_ref, wu_ref, wd_ref, o_ref, hn_s):
    @pl.when(pl.program_id(1) == 0)
    def _():
        x = x_ref[...]
        o_ref[...] = x
        hn_s[...] = _rms(x, g_ref[...]).astype(hn_s.dtype)

    o_ref[...] += _swiglu_step(hn_s[...], wg_ref, wu_ref, wd_ref)


def _ffn(x, gain, wg, wu, wd):
    n, d = x.shape
    d_ff = wg.shape[1]
    tm = _row_tile(n, 832)
    tf = _row_tile(d_ff, 256, LANES)
    row = pl.BlockSpec((tm, d), lambda i, f: (i, 0))
    return pl.pallas_call(
        _ffn_kernel,
        grid=(n // tm, d_ff // tf),
        in_specs=[row, pl.BlockSpec((1, d), lambda i, f: (0, 0)),
                  pl.BlockSpec((d, tf), lambda i, f: (0, f)), pl.BlockSpec((d, tf), lambda i, f: (0, f)),
                  pl.BlockSpec((tf, d), lambda i, f: (f, 0))],
        out_specs=row,
        out_shape=jax.ShapeDtypeStruct((n, d), F32),
        scratch_shapes=[pltpu.VMEM((tm, d), BF16)],
        compiler_params=_params(("parallel", "arbitrary")),
        name="ffn",
    )(x, gain.reshape(1, d), wg, wu, wd)


MOE_BLOCK = 1280
MOE_TILE = 256
COMBINE_TILE = 320


def _router_kernel(x_ref, g_ref, w_ref, o_ref, *, n_experts):
    hn = _rms(x_ref[...], g_ref[...]).astype(BF16)
    logits = jnp.dot(hn, w_ref[...], preferred_element_type=F32)
    lane = lax.broadcasted_iota(jnp.int32, logits.shape, 1).astype(F32)
    logits = jnp.where(lane < n_experts, logits, -jnp.inf)
    m1 = jnp.max(logits, axis=-1, keepdims=True)
    i1 = jnp.min(jnp.where(logits == m1, lane, float(LANES)), axis=-1, keepdims=True)
    rest = jnp.where(lane == i1, -jnp.inf, logits)
    m2 = jnp.max(rest, axis=-1, keepdims=True)
    i2 = jnp.min(jnp.where(rest == m2, lane, float(LANES)), axis=-1, keepdims=True)
    e2 = jnp.exp(m2 - m1)
    den = 1.0 + e2
    o_ref[...] = (jnp.where(lane == 0.0, 1.0 / den, 0.0) + jnp.where(lane == 1.0, e2 / den, 0.0)
                  + jnp.where(lane == 2.0, i1, 0.0) + jnp.where(lane == 3.0, i2, 0.0))


def _router(x, gain, w_router):
    n, d = x.shape
    n_experts = w_router.shape[1]
    tm = _row_tile(n, 832)
    w = jnp.concatenate([w_router, jnp.zeros((d, LANES - n_experts), w_router.dtype)], axis=1).astype(BF16)
    return pl.pallas_call(
        functools.partial(_router_kernel, n_experts=n_experts),
        grid=(n // tm,),
        in_specs=[pl.BlockSpec((tm, d), lambda i: (i, 0)), _full((1, d)), _full((d, LANES))],
        out_specs=pl.BlockSpec((tm, LANES), lambda i: (i, 0)),
        out_shape=jax.ShapeDtypeStruct((n, LANES), F32),
        compiler_params=_params(("parallel",)),
        name="router",
    )(x, gain.reshape(1, d), w)


def _start_row_gather(ids_ref, first, n_rows, src_hbm, dst, sem):
    def body(r, carry):
        pltpu.make_async_copy(src_hbm.at[pl.ds(ids_ref[first + r], 1)], dst.at[pl.ds(r, 1)], sem).start()
        return carry

    lax.fori_loop(0, n_rows, body, 0, unroll=8)


def _dispatch_kernel(ids_ref, on_ref, x_hbm, g_ref, o_ref, buf, sem):
    i = pl.program_id(0)
    slot = i % 2
    start = lambda tile, s: _start_row_gather(ids_ref, tile * MOE_TILE, MOE_TILE, x_hbm, buf.at[s], sem.at[s])

    @pl.when((i == 0) & (on_ref[0] > 0))
    def _():
        start(0, 0)

    @pl.when(i + 1 < pl.num_programs(0))
    def _():
        @pl.when(on_ref[i + 1] > 0)
        def _():
            start(i + 1, 1 - slot)

    @pl.when(on_ref[i] > 0)
    def _():
        pltpu.make_async_copy(buf.at[slot], buf.at[slot], sem.at[slot]).wait()
        o_ref[...] = _rms(buf[slot], g_ref[...]).astype(o_ref.dtype)

    @pl.when(on_ref[i] == 0)
    def _():
        o_ref[...] = jnp.zeros_like(o_ref)


def _dispatch(x, gain, row_ids, tile_on):
    n, d = x.shape
    n_slots = row_ids.shape[0]
    return pl.pallas_call(
        _dispatch_kernel,
        grid_spec=pltpu.PrefetchScalarGridSpec(
            num_scalar_prefetch=2,
            grid=(n_slots // MOE_TILE,),
            in_specs=[pl.BlockSpec(memory_space=pl.ANY), pl.BlockSpec((1, d), lambda i, ids, on: (0, 0))],
            out_specs=pl.BlockSpec((MOE_TILE, d), lambda i, ids, on: (i, 0)),
            scratch_shapes=[pltpu.VMEM((2, MOE_TILE, d), F32), pltpu.SemaphoreType.DMA((2,))],
        ),
        out_shape=jax.ShapeDtypeStruct((n_slots, d), BF16),
        compiler_params=_params(("arbitrary",)),
        name="moe_dispatch",
    )(row_ids, tile_on, x, gain.reshape(1, d))


def _experts_kernel(be_ref, br_ref, x_ref, w_ref, wg_ref, wu_ref, wd_ref, o_ref):
    b, f = pl.program_id(0), pl.program_id(1)
    rows = br_ref[b]

    @pl.when(f == 0)
    def _():
        o_ref[...] = jnp.zeros_like(o_ref)

    for used in range(MOE_TILE, MOE_BLOCK + 1, MOE_TILE):
        @pl.when(rows == used)
        def _():
            o_ref[0:used, :] += _swiglu_step(x_ref[0:used, :], wg_ref.at[0], wu_ref.at[0], wd_ref.at[0])

    @pl.when((f == pl.num_programs(1) - 1) & (rows > 0))
    def _():
        o_ref[...] = o_ref[...] * w_ref[...]


def _experts(xs, slot_w, blk_expert, blk_rows, wg, wu, wd):
    n_slots, d = xs.shape
    d_ff = wg.shape[2]
    tf = _row_tile(d_ff, 256, LANES)
    n_f = d_ff // tf
    fcol = lambda b, f, br: jnp.where(br[b] > 0, f, n_f - 1)
    return pl.pallas_call(
        _experts_kernel,
        grid_spec=pltpu.PrefetchScalarGridSpec(
            num_scalar_prefetch=2,
            grid=(n_slots // MOE_BLOCK, n_f),
            in_specs=[pl.BlockSpec((MOE_BLOCK, d), lambda b, f, be, br: (b, 0)),
                      pl.BlockSpec((MOE_BLOCK, 1), lambda b, f, be, br: (b, 0)),
                      pl.BlockSpec((1, d, tf), lambda b, f, be, br: (be[b], 0, fcol(b, f, br))),
                      pl.BlockSpec((1, d, tf), lambda b, f, be, br: (be[b], 0, fcol(b, f, br))),
                      pl.BlockSpec((1, tf, d), lambda b, f, be, br: (be[b], fcol(b, f, br), 0))],
            out_specs=pl.BlockSpec((MOE_BLOCK, d), lambda b, f, be, br: (b, 0)),
        ),
        out_shape=jax.ShapeDtypeStruct((n_slots, d), F32),
        compiler_params=_params(("arbitrary", "arbitrary")),
        name="moe_experts",
    )(blk_expert, blk_rows, xs, slot_w, wg, wu, wd)


def _combine_kernel(s0_ref, s1_ref, y_hbm, x_ref, o_ref, buf, sem):
    tc = x_ref.shape[0]
    i = pl.program_id(0)
    slot = i % 2

    def start(tile, s):
        _start_row_gather(s0_ref, tile * tc, tc, y_hbm, buf.at[s, 0], sem.at[s])
        _start_row_gather(s1_ref, tile * tc, tc, y_hbm, buf.at[s, 1], sem.at[s])

    @pl.when(i == 0)
    def _():
        start(0, 0)

    @pl.when(i + 1 < pl.num_programs(0))
    def _():
        start(i + 1, 1 - slot)

    pltpu.make_async_copy(buf.at[slot], buf.at[slot], sem.at[slot]).wait()
    o_ref[...] = x_ref[...] + buf[slot, 0] + buf[slot, 1]


def _combine(x, y, slot0, slot1):
    n, d = x.shape
    tc = _row_tile(n, COMBINE_TILE, SUBLANES)
    row = pl.BlockSpec((tc, d), lambda i, s0, s1: (i, 0))
    return pl.pallas_call(
        _combine_kernel,
        grid_spec=pltpu.PrefetchScalarGridSpec(
            num_scalar_prefetch=2,
            grid=(n // tc,),
            in_specs=[pl.BlockSpec(memory_space=pl.ANY), row],
            out_specs=row,
            scratch_shapes=[pltpu.VMEM((2, 2, tc, d), F32), pltpu.SemaphoreType.DMA((2,))],
        ),
        out_shape=jax.ShapeDtypeStruct((n, d), F32),
        compiler_params=_params(("arbitrary",)),
        name="moe_combine",
    )(slot0, slot1, y, x)


def _routing_tables(route, n_experts):
    n = route.shape[0]
    expert = route[:, 2:4].astype(jnp.int32).T.reshape(-1)
    weight = route[:, 0:2].T.reshape(-1)
    token = jnp.tile(jnp.arange(n, dtype=jnp.int32), 2)
    onehot = (expert[:, None] == jnp.arange(n_experts, dtype=jnp.int32)).astype(jnp.int32)
    rank = jnp.sum((jnp.cumsum(onehot, axis=0) - 1) * onehot, axis=1)
    count = jnp.sum(onehot, axis=0)
    padded = (count + MOE_BLOCK - 1) // MOE_BLOCK * MOE_BLOCK
    ends = jnp.cumsum(padded)
    starts = ends - padded
    slot = starts[expert] + rank
    n_slots = (2 * n + n_experts * (MOE_BLOCK - 1) + MOE_BLOCK - 1) // MOE_BLOCK * MOE_BLOCK
    per_slot = jnp.zeros((n_slots, 2), F32).at[slot].set(jnp.stack([token.astype(F32), weight], axis=1))
    row_ids = per_slot[:, 0].astype(jnp.int32)
    slot_w = per_slot[:, 1:2]
    blk_first = jnp.arange(n_slots // MOE_BLOCK, dtype=jnp.int32) * MOE_BLOCK
    blk_expert = jnp.minimum(jnp.sum((ends[None, :] <= blk_first[:, None]).astype(jnp.int32), axis=1), n_experts - 1)
    used = (count + MOE_TILE - 1) // MOE_TILE * MOE_TILE
    blk_rows = jnp.clip(starts[blk_expert] + used[blk_expert] - blk_first, 0, MOE_BLOCK)
    blk_rows = jnp.where(blk_first < ends[-1], blk_rows, 0).astype(jnp.int32)
    tile_first = jnp.arange(n_slots // MOE_TILE, dtype=jnp.int32) * MOE_TILE
    tile_on = (tile_first % MOE_BLOCK < jnp.repeat(blk_rows, MOE_BLOCK // MOE_TILE)).astype(jnp.int32)
    return row_ids, slot_w, blk_expert, blk_rows, tile_on, slot[:n], slot[n:]


def _moe(x, gain, w_router, wg, wu, wd):
    route = _router(x, gain, w_router)
    row_ids, slot_w, blk_expert, blk_rows, tile_on, slot0, slot1 = _routing_tables(route, w_router.shape[1])
    xs = _dispatch(x, gain, row_ids, tile_on)
    y = _experts(xs, slot_w, blk_expert, blk_rows, wg, wu, wd)
    return _combine(x, y, slot0, slot1)


def kernel(x_prompt, x_sample, cache_cmp, cache_sel, state_win, state_lru, state_conv, page_table, ln_mix, w_in, q_norm, k_norm, cmp_pe, cmp_w1, cmp_w2, conv_w, conv_b, lru_wa, lru_ba, lru_wx, lru_bx, lru_lambda, out_norm_attn, out_norm_lru, w_out, ln_ffn, ffn_w_gate, ffn_w_up, ffn_w_down, moe_router, moe_w_gate, moe_w_up, moe_w_down):
    n_batch, seq_len, d_model = x_prompt.shape
    dec_batch, dec_seq, _ = x_sample.shape
    depth = ln_mix.shape[0]
    page = cache_cmp.shape[2]
    n_pages = page_table.shape[1]
    past_len = n_pages * page
    win_buf = state_win.shape[2]
    lru_width = d_model - D_ATTN
    assert dec_seq == 1 and lru_width == D_ATTN
    assert seq_len % Q_BLOCK == 0 and seq_len >= WIN_SPAN and seq_len % SEL_CHUNK == 0 and seq_len >= win_buf
    assert past_len % Q_BLOCK == 0 and win_buf <= past_len and page % CMP_BLOCK == 0
    n_p = n_batch * seq_len
    nc = seq_len // CMP_BLOCK
    nc_past = past_len // CMP_BLOCK
    kv_row = (2, N_KV, HEAD_DIM)

    pos = jnp.concatenate([jnp.tile(jnp.arange(seq_len), n_batch), jnp.full((dec_batch,), past_len)])
    tables = _rope_tables(pos)
    end_p = _rope_tables((jnp.arange(nc) + 1) * CMP_BLOCK - 1)
    end_s = _rope_tables((jnp.arange(nc_past) + 1) * CMP_BLOCK - 1)

    x = jnp.concatenate([x_prompt.reshape(n_p, d_model), x_sample.reshape(dec_batch, d_model)], axis=0)
    outs = [[] for _ in range(9)]
    win_acc = jnp.zeros((depth * dec_batch * win_buf * UNITS, HEAD_DIM), F32)
    for l in range(depth):
        proj = _matmul([x], [_reorder_w_in(w_in[l])], gain=ln_mix[l], tm_target=832, tn_target=1280)
        q_b, sel_f, sel_b, win_f, win_b, gates = _postproj(proj, tables, q_norm[l], k_norm[l, 1], k_norm[l, 2])
        c_rows = proj[:, COL_CMP:COL_CMP + ROW_WIDTH]

        cmp_w = (cmp_pe[l], cmp_w1[l].reshape(2, CMP_BLOCK // 2, 2 * HEAD_DIM, HEAD_DIM).astype(BF16),
                 cmp_w2[l].astype(BF16), k_norm[l, 0].reshape(1, HEAD_DIM))
        lru_w = (conv_w[l], conv_b[l].reshape(1, -1), lru_wa[l].astype(BF16), lru_wx[l].astype(BF16),
                 lru_ba[l].reshape(1, -1), lru_bx[l].reshape(1, -1), lru_lambda[l].reshape(1, -1),
                 out_norm_lru[l].reshape(1, -1))

        kc_p, vc_p = _compress_prompt(c_rows[:n_p].reshape(n_batch * nc, CMP_ROW), n_batch, nc, cmp_w, end_p)
        o_p = _attn_prompt(q_b, kc_p, vc_p, sel_b, win_b, gates, out_norm_attn[l], n_batch, seq_len)
        r_p, h_p = _lru_prompt(proj, n_batch, seq_len, lru_w)

        kc_s, vc_s = _compress_sample(cache_cmp, l, page_table, cmp_w, end_s, nb=2 if dec_batch % 2 == 0 else 1)
        o_s, win_acc = _attn_sample(q_b[n_p:], kc_s, vc_s, sel_f[n_p:], cache_sel, state_win, win_f[n_p:],
                                    gates[n_p:], out_norm_attn[l], l, page_table, win_acc)
        ux_s = proj[n_p:, COL_UX:COL_UX + lru_width]
        r_s, h_s = _lru_sample(ux_s, proj[n_p:, COL_UY:COL_UY + lru_width], state_conv[l], state_lru[l], lru_w)

        w_o = w_out[l].astype(BF16)
        x = _matmul([jnp.concatenate([o_p, o_s], axis=0), jnp.concatenate([r_p, r_s], axis=0)],
                    [w_o[:D_ATTN], w_o[D_ATTN:]], res=x, tm_target=832, tn_target=1024)

        k = l // 2
        if l % 2 == 0:
            x = _ffn(x, ln_ffn[l], ffn_w_gate[k], ffn_w_up[k], ffn_w_down[k])
        else:
            x = _moe(x, ln_ffn[l], moe_router[k], moe_w_gate[k], moe_w_up[k], moe_w_down[k])

        conv_tail = jnp.stack([proj[(b + 1) * seq_len - (CONV_W - 1):(b + 1) * seq_len, COL_UX:COL_UX + lru_width]
                               for b in range(n_batch)])
        new = (c_rows[:n_p].reshape((n_batch, seq_len) + kv_row), c_rows[n_p:].reshape((dec_batch, 1) + kv_row),
               sel_f[:n_p].reshape((n_batch, seq_len) + kv_row), sel_f[n_p:].reshape((dec_batch, 1) + kv_row),
               win_f[:n_p].reshape((n_batch, seq_len) + kv_row)[:, seq_len - win_buf:],
               h_p.reshape(n_batch, lru_width), h_s,
               conv_tail,
               jnp.concatenate([state_conv[l][:, 1:], ux_s[:, None, :]], axis=1))
        for acc, val in zip(outs, new):
            acc.append(val)

    stacked = [jnp.stack(v) for v in outs]
    stacked.insert(5, win_acc.reshape((depth, dec_batch, win_buf) + kv_row))
    return (x[:n_p].reshape(n_batch, seq_len, d_model), x[n_p:].reshape(dec_batch, 1, d_model), *stacked)
```

```python
import functools
import math

import jax
import jax.numpy as jnp
from jax import lax
from jax.experimental import pallas as pl
from jax.experimental.pallas import tpu as pltpu

N_HEADS = 8
HEAD_DIM = 128
N_KV = 2
Q_PER_KV = N_HEADS // N_KV
D_ATTN = N_HEADS * HEAD_DIM
KV_WIDTH = N_KV * HEAD_DIM
ROW_WIDTH = 2 * KV_WIDTH
UNITS = 2 * N_KV
ROT_DIM = HEAD_DIM // 4
ROPE_THETA = 500000.0
CMP_BLOCK = 32
SEL_BLOCK = 64
TOP_N = 16
WINDOW = 512
Q_BLOCK = 128
LRU_BLOCKS = 8
CONV_W = 4
LRU_C = 8.0
TOP_K = 2
EPS = 1e-6
SCALE = HEAD_DIM ** -0.5
NEG_INF = -1e30

LANES = 128
SUBLANES = 8
VMEM_LIMIT = 56 * 1024 * 1024

F32 = jnp.float32
BF16 = jnp.bfloat16


def _row_tile(n, target, mult=16):
    best = None
    for t in range(mult, min(n, target) + 1, mult):
        if n % t == 0:
            best = t
    return best if best is not None else n


def _params(semantics):
    return pltpu.CompilerParams(dimension_semantics=semantics, vmem_limit_bytes=VMEM_LIMIT)


def _full(shape):
    zeros = (0,) * len(shape)
    return pl.BlockSpec(shape, lambda *_: zeros)


def _rms(x, gain):
    return x * lax.rsqrt(jnp.mean(x * x, axis=-1, keepdims=True) + EPS) * gain


def _rope_apply(y, cos_t, sin_lo, sin_hi):
    half = ROT_DIM // 2
    return y * cos_t + pltpu.roll(y, HEAD_DIM - half, 1) * sin_lo + pltpu.roll(y, half, 1) * sin_hi


def _softplus(x):
    return jnp.maximum(x, 0.0) + jnp.log1p(jnp.exp(-jnp.abs(x)))


def _topk_mask(score, n_pick):
    n_lanes = score.shape[-1]
    lane = lax.broadcasted_iota(jnp.int32, score.shape, 1).astype(F32)
    sel = jnp.zeros(score.shape, F32)
    for _ in range(n_pick):
        m = jnp.max(score, axis=-1, keepdims=True)
        idx = jnp.min(jnp.where(score == m, lane, float(n_lanes)), axis=-1, keepdims=True)
        pick = lane == idx
        sel = jnp.where(pick, 1.0, sel)
        score = jnp.where(pick, -2.0, score)
    return sel


def _topk_mask_cols(score, n_pick):
    n_rows = score.shape[0]
    row = lax.broadcasted_iota(jnp.int32, score.shape, 0).astype(F32)
    sel = jnp.zeros(score.shape, F32)
    for _ in range(n_pick):
        m = jnp.max(score, axis=0, keepdims=True)
        idx = jnp.min(jnp.where(score == m, row, float(n_rows)), axis=0, keepdims=True)
        pick = row == idx
        sel = jnp.where(pick, 1.0, sel)
        score = jnp.where(pick, -2.0, score)
    return sel


def _masked_softmax_rows(s, mask):
    sm = jnp.where(mask, s, NEG_INF)
    m = jnp.max(sm, axis=-1, keepdims=True)
    e = jnp.where(mask, jnp.exp(sm - m), 0.0)
    l = jnp.sum(e, axis=-1, keepdims=True)
    return e / jnp.where(l > 0.0, l, 1.0)


def _dot_nt(a, b):
    return lax.dot_general(a, b, (((1,), (1,)), ((), ())), preferred_element_type=F32)


def _matmul_kernel(*refs, n_lhs, has_res, has_gain):
    o_ref = refs[-1]
    lhs = refs[0][...]
    if has_gain:
        lhs = _rms(lhs, refs[2 * n_lhs + has_res][...]).astype(BF16)
    acc = jnp.dot(lhs, refs[n_lhs][...], preferred_element_type=F32)
    for k in range(1, n_lhs):
        acc = acc + jnp.dot(refs[k][...], refs[n_lhs + k][...], preferred_element_type=F32)
    if has_res:
        acc = acc + refs[2 * n_lhs][...]
    o_ref[...] = acc


def _matmul(xs, ws, res=None, gain=None, *, tm_target, tn_target):
    n = xs[0].shape[0]
    d_out = ws[0].shape[1]
    tm = _row_tile(n, tm_target)
    tn = _row_tile(d_out, tn_target, LANES)
    in_specs = [pl.BlockSpec((tm, x.shape[1]), lambda i, j: (i, 0)) for x in xs]
    in_specs += [pl.BlockSpec((w.shape[0], tn), lambda i, j: (0, j)) for w in ws]
    args = list(xs) + list(ws)
    if res is not None:
        in_specs.append(pl.BlockSpec((tm, tn), lambda i, j: (i, j)))
        args.append(res)
    if gain is not None:
        in_specs.append(pl.BlockSpec((1, gain.shape[0]), lambda i, j: (0, 0)))
        args.append(gain.reshape(1, -1))
    return pl.pallas_call(
        functools.partial(_matmul_kernel, n_lhs=len(xs), has_res=res is not None, has_gain=gain is not None),
        grid=(n // tm, d_out // tn),
        in_specs=in_specs,
        out_specs=pl.BlockSpec((tm, tn), lambda i, j: (i, j)),
        out_shape=jax.ShapeDtypeStruct((n, d_out), F32),
        compiler_params=_params(("parallel", "parallel")),
        name="matmul",
    )(*args)


PROJ_WIDTH = 5120
COL_UX, COL_UY, COL_CMP, COL_SEL, COL_WIN, COL_GATE = 1024, 2048, 3072, 3584, 4096, 4608
GATE_WIDTH = 2 * LANES


def _reorder_w_in(w):
    lru = D_ATTN
    c_kv = D_ATTN
    c_g = c_kv + 3 * ROW_WIDTH
    c_ux = c_g + 3 * N_HEADS
    c_uy = c_ux + lru
    n_g = 3 * Q_PER_KV
    zpad = lambda k: jnp.zeros((w.shape[0], k), w.dtype)
    parts = [w[:, :D_ATTN], w[:, c_ux:c_uy], w[:, c_uy:c_uy + lru], w[:, c_kv:c_g],
             w[:, c_g:c_g + n_g], zpad(LANES - n_g), w[:, c_g + n_g:c_ux], zpad(LANES - n_g)]
    width = sum(p.shape[1] for p in parts)
    parts.append(zpad(PROJ_WIDTH - width))
    return jnp.concatenate(parts, axis=1).astype(BF16)


def _postproj_kernel(q_ref, sel_ref, win_ref, gate_ref, cos_ref, slo_ref, shi_ref,
                     qg_ref, ksg_ref, kwg_ref,
                     qo_ref, sel_f_ref, sel_b_ref, win_f_ref, win_b_ref, gate_o_ref):
    cos_t, slo, shi = cos_ref[...], slo_ref[...], shi_ref[...]
    for h in range(N_HEADS):
        cols = slice(h * HEAD_DIM, (h + 1) * HEAD_DIM)
        y = _rope_apply(_rms(q_ref[:, cols], qg_ref[...]), cos_t, slo, shi)
        qo_ref[:, cols] = y.astype(qo_ref.dtype)
    for src, gain_ref, out_f, out_b in ((sel_ref, ksg_ref, sel_f_ref, sel_b_ref),
                                        (win_ref, kwg_ref, win_f_ref, win_b_ref)):
        for g in range(N_KV):
            cols = slice(g * HEAD_DIM, (g + 1) * HEAD_DIM)
            y = _rope_apply(_rms(src[:, cols], gain_ref[...]), cos_t, slo, shi)
            out_f[:, cols] = y
            out_b[:, cols] = y.astype(out_b.dtype)
        v = src[:, KV_WIDTH:]
        out_f[:, KV_WIDTH:] = v
        out_b[:, KV_WIDTH:] = v.astype(out_b.dtype)
    gate_o_ref[...] = jax.nn.sigmoid(gate_ref[...])


def _postproj(proj, tables, q_gain, ks_gain, kw_gain):
    n = proj.shape[0]
    tm = _row_tile(n, 416)
    rows = lambda width, col: pl.BlockSpec((tm, width), lambda i: (i, col))
    tab = pl.BlockSpec((tm, HEAD_DIM), lambda i: (i, 0))
    gain = _full((1, HEAD_DIM))
    out = lambda width: pl.BlockSpec((tm, width), lambda i: (i, 0))
    sds = jax.ShapeDtypeStruct
    return pl.pallas_call(
        _postproj_kernel,
        grid=(n // tm,),
        in_specs=[rows(D_ATTN, 0), rows(ROW_WIDTH, COL_SEL // ROW_WIDTH), rows(ROW_WIDTH, COL_WIN // ROW_WIDTH),
                  rows(GATE_WIDTH, COL_GATE // GATE_WIDTH), tab, tab, tab, gain, gain, gain],
        out_specs=[out(D_ATTN), out(ROW_WIDTH), out(ROW_WIDTH), out(ROW_WIDTH), out(ROW_WIDTH), out(GATE_WIDTH)],
        out_shape=[sds((n, D_ATTN), BF16), sds((n, ROW_WIDTH), F32), sds((n, ROW_WIDTH), BF16),
                   sds((n, ROW_WIDTH), F32), sds((n, ROW_WIDTH), BF16), sds((n, GATE_WIDTH), F32)],
        compiler_params=_params(("parallel",)),
        name="postproj",
    )(proj, proj, proj, proj, *tables, q_gain.reshape(1, -1), ks_gain.reshape(1, -1), kw_gain.reshape(1, -1))


def _rope_tables(pos):
    half = ROT_DIM // 2
    inv_freq = ROPE_THETA ** (-2.0 * jnp.arange(half, dtype=F32) / ROT_DIM)
    ang = pos.astype(F32)[:, None] * inv_freq
    cos, sin = jnp.cos(ang), jnp.sin(ang)
    n = pos.shape[0]
    ones = jnp.ones((n, HEAD_DIM - ROT_DIM), F32)
    zeros = jnp.zeros((n, HEAD_DIM - half), F32)
    cos_t = jnp.concatenate([cos, cos, ones], axis=1)
    sin_lo = jnp.concatenate([-sin, zeros], axis=1)
    sin_hi = jnp.concatenate([jnp.zeros((n, half), F32), sin, jnp.zeros((n, HEAD_DIM - ROT_DIM), F32)], axis=1)
    return cos_t, sin_lo, sin_hi


CMP_ROW = CMP_BLOCK * ROW_WIDTH


def _compress_rows(piece, n_rows, group, w1_ref, w2_ref, gain_ref, cos_ref, slo_ref, shi_ref, k_ref, v_ref, perm_s):
    def store(out_ref, cols, part):
        perm_s[...] = part
        half = group // 2
        for b0 in range(0, n_rows, group):
            out_ref[b0:b0 + half, cols] = perm_s[pl.ds(b0, half, stride=2), :].astype(out_ref.dtype)
            out_ref[b0 + half:b0 + group, cols] = perm_s[pl.ds(b0 + 1, half, stride=2), :].astype(out_ref.dtype)

    for n in range(2):
        acc = jnp.zeros((N_KV * n_rows, HEAD_DIM), F32)
        for tp in range(CMP_BLOCK // 2):
            lhs = jnp.concatenate([jnp.concatenate([piece(n, 2 * tp, g), piece(n, 2 * tp + 1, g)], axis=1)
                                   for g in range(N_KV)], axis=0)
            acc = acc + jnp.dot(lhs, w1_ref[n, tp], preferred_element_type=F32)
        hid = jax.nn.gelu(acc).astype(BF16)
        comp = jnp.dot(hid, w2_ref[n], preferred_element_type=F32)
        for g in range(N_KV):
            part = comp[g * n_rows:(g + 1) * n_rows]
            cols = slice(g * HEAD_DIM, (g + 1) * HEAD_DIM)
            if n == 0:
                store(k_ref, cols, _rope_apply(_rms(part, gain_ref[...]), cos_ref[...], slo_ref[...], shi_ref[...]))
            else:
                store(v_ref, cols, part)


def _compress_prompt_kernel(x_ref, pe_ref, *rest):
    def piece(n, t, g):
        c0 = t * ROW_WIDTH + n * KV_WIDTH + g * HEAD_DIM
        return (x_ref[:, c0:c0 + HEAD_DIM] + pe_ref[n, t:t + 1, :]).astype(BF16)

    _compress_rows(piece, x_ref.shape[0], x_ref.shape[0], *rest)


def _compress_sample_kernel(pt_ref, *refs, n_page_refs, page, group):
    del pt_ref
    pages = refs[:n_page_refs]
    pe_ref = refs[n_page_refs]
    rest = refs[n_page_refs + 1:-2] + refs[-1:]
    y_s = refs[-2]
    n_tok = 2 * page
    out_row = lax.broadcasted_iota(jnp.int32, (n_tok, n_tok), 0)
    src_tok = lax.broadcasted_iota(jnp.int32, (n_tok, n_tok), 1)
    regroup = ((out_row % SUBLANES) * CMP_BLOCK + out_row // SUBLANES == src_tok).astype(BF16)
    pe_tok = [jnp.concatenate([jnp.concatenate([pe_ref[n]] * N_KV, axis=1)] * (n_tok // CMP_BLOCK), axis=0)
              for n in range(2)]
    for pair in range(n_page_refs // 2):
        for n in range(2):
            tok = jnp.concatenate([jnp.concatenate([pages[2 * pair + k][pl.ds(n * N_KV + g, page, stride=UNITS), :]
                                                    for g in range(N_KV)], axis=1) for k in range(2)], axis=0)
            y = jnp.dot(regroup, (tok + pe_tok[n]).astype(BF16), preferred_element_type=F32)
            y_s[n, :, pair * SUBLANES:(pair + 1) * SUBLANES, :] = y.reshape(CMP_BLOCK, SUBLANES, KV_WIDTH)

    _compress_rows(lambda n, t, g: y_s[n, t, :, g * HEAD_DIM:(g + 1) * HEAD_DIM].astype(BF16), y_s.shape[2], group,
                   *rest)


def _compress_weight_specs():
    return [_full((2, CMP_BLOCK, HEAD_DIM)), _full((2, CMP_BLOCK // 2, 2 * HEAD_DIM, HEAD_DIM)),
            _full((2, HEAD_DIM, HEAD_DIM)), _full((1, HEAD_DIM))]


def _compress_prompt(c_rows, n_batch, nc, weights, end_tables):
    tab = pl.BlockSpec((nc, HEAD_DIM), lambda i: (0, 0))
    out = pl.BlockSpec((nc, KV_WIDTH), lambda i: (i, 0))
    sds = jax.ShapeDtypeStruct((n_batch * nc, KV_WIDTH), BF16)
    k_cmp, v_cmp = pl.pallas_call(
        _compress_prompt_kernel,
        grid=(n_batch,),
        in_specs=[pl.BlockSpec((nc, CMP_ROW), lambda i: (i, 0))] + _compress_weight_specs() + [tab, tab, tab],
        out_specs=[out, out],
        out_shape=[sds, sds],
        scratch_shapes=[pltpu.VMEM((nc, HEAD_DIM), F32)],
        compiler_params=_params(("parallel",)),
        name="compress_prompt",
    )(c_rows, *weights, *end_tables)
    return k_cmp.reshape(n_batch, nc, KV_WIDTH), v_cmp.reshape(n_batch, nc, KV_WIDTH)


def _compress_sample(cache, layer, page_table, weights, end_tables, nb):
    depth, n_phys, page = cache.shape[:3]
    n_batch, n_pages = page_table.shape
    assert n_pages % 2 == 0 and 2 * page == SUBLANES * CMP_BLOCK
    nc = n_pages * page // CMP_BLOCK
    pages = cache.reshape(depth * n_phys * page * UNITS, HEAD_DIM)
    base = layer * n_phys
    page_specs = [pl.BlockSpec((page * UNITS, HEAD_DIM), lambda i, pt, bb=bb, p=p: (base + pt[i * nb + bb, p], 0))
                  for bb in range(nb) for p in range(n_pages)]
    n_rows = nb * nc
    tab = pl.BlockSpec((n_rows, HEAD_DIM), lambda i, pt: (0, 0))
    wspecs = [pl.BlockSpec(s.block_shape, lambda i, pt, z=(0,) * len(s.block_shape): z) for s in _compress_weight_specs()]
    out = pl.BlockSpec((n_rows, KV_WIDTH), lambda i, pt: (i, 0))
    sds = jax.ShapeDtypeStruct((n_batch * nc, KV_WIDTH), BF16)
    tables = [jnp.tile(t, (nb, 1)) for t in end_tables]
    k_cmp, v_cmp = pl.pallas_call(
        functools.partial(_compress_sample_kernel, n_page_refs=nb * n_pages, page=page, group=nc),
        grid_spec=pltpu.PrefetchScalarGridSpec(
            num_scalar_prefetch=1,
            grid=(n_batch // nb,),
            in_specs=page_specs + wspecs + [tab, tab, tab],
            out_specs=[out, out],
            scratch_shapes=[pltpu.VMEM((2, CMP_BLOCK, n_rows, KV_WIDTH), F32), pltpu.VMEM((n_rows, HEAD_DIM), F32)],
        ),
        out_shape=[sds, sds],
        compiler_params=_params(("arbitrary",)),
        name="compress_sample",
    )(page_table, *([pages] * (nb * n_pages)), *weights, *tables)
    return k_cmp.reshape(n_batch, nc, KV_WIDTH), v_cmp.reshape(n_batch, nc, KV_WIDTH)


SEL_CHUNK = 256
WIN_SPAN = WINDOW + Q_BLOCK


def _attn_prompt_kernel(q_ref, kc_ref, vc_ref, sel_ref, win_ref, gate_ref, gain_ref, o_ref, *, seq_len):
    nc = seq_len // CMP_BLOCK
    ns = seq_len // SEL_BLOCK
    i = pl.program_id(1)
    q0 = i * Q_BLOCK
    tpos = q0 + lax.broadcasted_iota(jnp.int32, (Q_BLOCK, 1), 0)
    tlane = q0 + lax.broadcasted_iota(jnp.int32, (1, Q_BLOCK), 1)
    kcol = lambda g: slice(g * HEAD_DIM, (g + 1) * HEAD_DIM)
    vcol = lambda g: slice(KV_WIDTH + g * HEAD_DIM, KV_WIDTH + (g + 1) * HEAD_DIM)
    rows = lambda h: slice(h * Q_BLOCK, (h + 1) * Q_BLOCK)
    n_rows = Q_PER_KV * Q_BLOCK

    qs, o_c, o_w, sel = [], [], [], []
    for g in range(N_KV):
        q_g = jnp.concatenate([q_ref[:, (g * Q_PER_KV + h) * HEAD_DIM:(g * Q_PER_KV + h + 1) * HEAD_DIM]
                               for h in range(Q_PER_KV)], axis=0)
        qs.append(q_g)

        col = lax.broadcasted_iota(jnp.int32, (Q_BLOCK, nc), 1)
        blk = jnp.where(col < ns, 2 * col, 2 * (col - ns) + 1)
        c_mask = ((blk + 1) * CMP_BLOCK - 1) <= tpos
        s_c = (_dot_nt(q_g, kc_ref[0, :, kcol(g)]) * SCALE).reshape(Q_PER_KV, Q_BLOCK, nc)
        p_c = _masked_softmax_rows(s_c, c_mask[None])
        o_c.append(jnp.dot(p_c.reshape(n_rows, nc).astype(BF16), vc_ref[0, :, kcol(g)], preferred_element_type=F32))
        p_tok = p_c[0] + p_c[1] + p_c[2] + p_c[3]

        p_blk = p_tok.T
        imp = p_blk[:ns] + p_blk[ns:]
        sblk = lax.broadcasted_iota(jnp.int32, (ns, Q_BLOCK), 0)
        allowed = sblk * SEL_BLOCK <= tlane
        forced = (sblk == 0) | (sblk == tlane // SEL_BLOCK)
        score = jnp.where(forced, 1e30, jnp.where(allowed, imp, -1.0))
        picked = _topk_mask_cols(score, min(TOP_N, ns))
        picked = jnp.concatenate([picked, jnp.zeros((LANES - ns, Q_BLOCK), F32)], axis=0)
        sel.append(picked.T.astype(BF16))

        w0 = pl.multiple_of(jnp.maximum(q0 - WINDOW, 0), Q_BLOCK)
        wpos = w0 + lax.broadcasted_iota(jnp.int32, (1, WIN_SPAN), 1)
        dist = tpos - wpos
        w_mask = ((dist >= 0) & (dist <= WINDOW))[None]
        s_w = (_dot_nt(q_g, win_ref[pl.ds(w0, WIN_SPAN), kcol(g)]) * SCALE).reshape(Q_PER_KV, Q_BLOCK, WIN_SPAN)
        p_w = _masked_softmax_rows(s_w, w_mask)
        o_w.append(jnp.dot(p_w.reshape(n_rows, WIN_SPAN).astype(BF16), win_ref[pl.ds(w0, WIN_SPAN), vcol(g)],
                           preferred_element_type=F32))

    def sel_step(j, carry):
        k0 = pl.multiple_of(j * SEL_CHUNK, SEL_CHUNK)
        kpos = k0 + lax.broadcasted_iota(jnp.int32, (1, SEL_CHUNK), 1)
        expand = (kpos // SEL_BLOCK == lax.broadcasted_iota(jnp.int32, (LANES, SEL_CHUNK), 0)).astype(BF16)
        causal = kpos <= tpos
        out = []
        for g in range(N_KV):
            mask = (jnp.dot(sel[g], expand, preferred_element_type=F32) > 0.5) & causal
            bias = jnp.where(mask, 0.0, NEG_INF)
            k_blk = sel_ref[pl.ds(k0, SEL_CHUNK), kcol(g)]
            v_blk = sel_ref[pl.ds(k0, SEL_CHUNK), vcol(g)]
            heads = []
            for h in range(Q_PER_KV):
                m, l, acc = carry[g][h]
                sm = _dot_nt(qs[g][rows(h)], k_blk) * SCALE + bias
                m_new = jnp.maximum(m, jnp.max(sm, axis=-1, keepdims=True))
                p = jnp.exp(sm - m_new)
                alpha = jnp.exp(m - m_new)
                l = alpha * l + jnp.sum(p, axis=-1, keepdims=True)
                acc = alpha * acc + jnp.dot(p.astype(BF16), v_blk, preferred_element_type=F32)
                heads.append((m_new, l, acc))
            out.append(tuple(heads))
        return tuple(out)

    n_steps = (q0 + Q_BLOCK + SEL_CHUNK - 1) // SEL_CHUNK
    head0 = (jnp.full((Q_BLOCK, 1), NEG_INF, F32), jnp.zeros((Q_BLOCK, 1), F32), jnp.zeros((Q_BLOCK, HEAD_DIM), F32))
    swept = lax.fori_loop(0, n_steps, sel_step, ((head0,) * Q_PER_KV,) * N_KV)

    gates = gate_ref[...]
    slabs = []
    for g in range(N_KV):
        for h in range(Q_PER_KV):
            o_s = swept[g][h][2] / swept[g][h][1]
            c = g * LANES + 3 * h
            slabs.append(gates[:, c:c + 1] * o_c[g][rows(h)] + gates[:, c + 1:c + 2] * o_s
                         + gates[:, c + 2:c + 3] * o_w[g][rows(h)])
    o = jnp.concatenate(slabs, axis=1)
    o_ref[...] = _rms(o, gain_ref[...]).astype(o_ref.dtype)


def _attn_prompt(q_b, k_cmp, v_cmp, sel_b, win_b, gates, gain, n_batch, seq_len):
    n_qb = seq_len // Q_BLOCK
    nc = seq_len // CMP_BLOCK
    qrow = lambda width: pl.BlockSpec((Q_BLOCK, width), lambda b, i: (b * n_qb + i, 0))
    cmp_spec = pl.BlockSpec((1, nc, KV_WIDTH), lambda b, i: (b, 0, 0))
    seq_spec = pl.BlockSpec((seq_len, ROW_WIDTH), lambda b, i: (b, 0))
    return pl.pallas_call(
        functools.partial(_attn_prompt_kernel, seq_len=seq_len),
        grid=(n_batch, n_qb),
        in_specs=[qrow(D_ATTN), cmp_spec, cmp_spec, seq_spec, seq_spec, qrow(GATE_WIDTH), _full((1, D_ATTN))],
        out_specs=qrow(D_ATTN),
        out_shape=jax.ShapeDtypeStruct((n_batch * seq_len, D_ATTN), BF16),
        compiler_params=_params(("parallel", "arbitrary")),
        name="attn_prompt",
    )(q_b, k_cmp, v_cmp, sel_b, win_b, gates, gain.reshape(1, -1))


def _unit_rows(refs, n_tokens, first_unit):
    cols = [jnp.concatenate([r[pl.ds(first_unit + g, n_tokens, stride=UNITS), :] for r in refs], axis=0)
            for g in range(N_KV)]
    return jnp.concatenate(cols, axis=1).astype(BF16)


def _own_group(x, row):
    return jnp.where(row < Q_PER_KV, x[:, :HEAD_DIM], x[:, HEAD_DIM:])


def _split_groups(q):
    row = lax.broadcasted_iota(jnp.int32, q.shape[:-1] + (1,), q.ndim - 2)
    zero = jnp.zeros_like(q)
    return jnp.concatenate([jnp.where(row < Q_PER_KV, q, zero), jnp.where(row < Q_PER_KV, zero, q)], axis=-1)


def _select_sample_kernel(q_ref, kc_ref, vc_ref, oc_ref, sel_ref, *, ns_past):
    q2 = _split_groups(q_ref[...])
    bt = q2.shape[0]
    s_c = jnp.einsum("bhk,bck->bhc", q2, kc_ref[...], preferred_element_type=F32) * SCALE
    m_c = jnp.max(s_c, axis=-1, keepdims=True)
    e_c = jnp.exp(s_c - m_c)
    p_c = e_c / jnp.sum(e_c, axis=-1, keepdims=True)
    o2 = jnp.einsum("bhc,bck->bhk", p_c.astype(BF16), vc_ref[...], preferred_element_type=F32)
    row = lax.broadcasted_iota(jnp.int32, (bt, N_HEADS, 1), 1)
    oc_ref[...] = jnp.where(row < Q_PER_KV, o2[..., :HEAD_DIM], o2[..., HEAD_DIM:])

    pair = p_c + jnp.where(row % 2 == 0, pltpu.roll(p_c, N_HEADS - 1, 1), pltpu.roll(p_c, 1, 1))
    p_grp = pair + jnp.where(row % Q_PER_KV < 2, pltpu.roll(pair, N_HEADS - 2, 1), pltpu.roll(pair, 2, 1))
    imp = (p_grp[..., :ns_past] + p_grp[..., ns_past:]).reshape(bt * N_HEADS, ns_past)

    sblk = lax.broadcasted_iota(jnp.int32, imp.shape, 1)
    score = jnp.where(sblk == 0, 1e30, imp)
    sel = _topk_mask(score, min(TOP_N, ns_past + 1) - 1).reshape(bt, N_HEADS, ns_past)
    sel_ref[...] = jnp.concatenate([sel, jnp.zeros((bt, N_HEADS, LANES - ns_past), F32)], axis=-1)


def _select_sample(q3, k_cmp, v_cmp):
    n_batch, nc_past, _ = k_cmp.shape
    ns_past = nc_past // (SEL_BLOCK // CMP_BLOCK)
    assert ns_past <= LANES
    bt = _row_tile(n_batch, 32, 1)
    per_b = lambda rows, width: pl.BlockSpec((bt, rows, width), lambda i: (i, 0, 0))
    return pl.pallas_call(
        functools.partial(_select_sample_kernel, ns_past=ns_past),
        grid=(n_batch // bt,),
        in_specs=[per_b(N_HEADS, HEAD_DIM), per_b(nc_past, KV_WIDTH), per_b(nc_past, KV_WIDTH)],
        out_specs=[per_b(N_HEADS, HEAD_DIM), per_b(N_HEADS, LANES)],
        out_shape=[jax.ShapeDtypeStruct((n_batch, N_HEADS, HEAD_DIM), F32),
                   jax.ShapeDtypeStruct((n_batch, N_HEADS, LANES), F32)],
        compiler_params=_params(("parallel",)),
        name="select_sample",
    )(q3, k_cmp, v_cmp)


def _attn_sample_kernel(pt_ref, *refs, n_pages, page, win_buf, past_len):
    del pt_ref
    q_ref, oc_ref, sel_ref, snew_ref, win_ref, wnew_ref, gate_ref, gain_ref = refs[:8]
    pages = refs[8:8 + n_pages]
    o_ref, nwin_ref = refs[-2:]

    q = q_ref[0]
    row = lax.broadcasted_iota(jnp.int32, (N_HEADS, 1), 0)
    q2 = _split_groups(q)
    qf = q.astype(F32)
    o_c = oc_ref[0]

    kpos = lax.broadcasted_iota(jnp.int32, (1, past_len), 1)
    expand = (kpos // SEL_BLOCK == lax.broadcasted_iota(jnp.int32, (LANES, past_len), 0)).astype(BF16)
    chosen = jnp.dot(sel_ref[0].astype(BF16), expand, preferred_element_type=F32) > 0.5

    k_sel = _unit_rows(pages, page, 0)
    v_sel = _unit_rows(pages, page, N_KV)
    s_s = _dot_nt(q2, k_sel) * SCALE
    k_new = _own_group(snew_ref[0, :, :KV_WIDTH], row)
    v_new = _own_group(snew_ref[0, :, KV_WIDTH:], row)
    s_new = jnp.sum(qf * k_new, axis=-1, keepdims=True) * SCALE
    sm = jnp.where(chosen, s_s, NEG_INF)
    m_s = jnp.maximum(jnp.max(sm, axis=-1, keepdims=True), s_new)
    p_s = jnp.where(chosen, jnp.exp(sm - m_s), 0.0)
    p_new = jnp.exp(s_new - m_s)
    l_s = jnp.sum(p_s, axis=-1, keepdims=True) + p_new
    o_s = (_own_group(jnp.dot(p_s.astype(BF16), v_sel, preferred_element_type=F32), row) + p_new * v_new) / l_s

    k_win = _unit_rows([win_ref], win_buf, 0)
    v_win = _unit_rows([win_ref], win_buf, N_KV)
    wpos = past_len - win_buf + lax.broadcasted_iota(jnp.int32, (1, win_buf), 1)
    w_mask = ((past_len - wpos) <= WINDOW) & (wpos >= 0)
    s_w = jnp.where(w_mask, _dot_nt(q2, k_win) * SCALE, NEG_INF)
    kw_new = _own_group(wnew_ref[0, :, :KV_WIDTH], row)
    vw_new = _own_group(wnew_ref[0, :, KV_WIDTH:], row)
    sw_new = jnp.sum(qf * kw_new, axis=-1, keepdims=True) * SCALE
    m_w = jnp.maximum(jnp.max(s_w, axis=-1, keepdims=True), sw_new)
    p_w = jnp.where(w_mask, jnp.exp(s_w - m_w), 0.0)
    pw_new = jnp.exp(sw_new - m_w)
    l_w = jnp.sum(p_w, axis=-1, keepdims=True) + pw_new
    o_w = (_own_group(jnp.dot(p_w.astype(BF16), v_win, preferred_element_type=F32), row) + pw_new * vw_new) / l_w

    g_own = _own_group(gate_ref[0], row)
    lane = lax.broadcasted_iota(jnp.int32, (N_HEADS, LANES), 1)
    first = 3 * (row % Q_PER_KV)
    pick = lambda j: jnp.sum(jnp.where(lane == first + j, g_own, 0.0), axis=-1, keepdims=True)
    o = pick(0) * o_c + pick(1) * o_s + pick(2) * o_w
    ms = jnp.sum(jnp.sum(o * o, axis=-1, keepdims=True), axis=0, keepdims=True) / D_ATTN
    o_ref[0] = (o * lax.rsqrt(ms + EPS) * gain_ref[...]).astype(o_ref.dtype)

    kept = (win_buf - 1) * UNITS
    nwin_ref[0:kept, :] = win_ref[UNITS:win_buf * UNITS, :]
    for u in range(UNITS):
        nwin_ref[kept + u:kept + u + 1, :] = wnew_ref[0, :, u * HEAD_DIM:(u + 1) * HEAD_DIM]


def _attn_sample(q_b, k_cmp, v_cmp, sel_new, cache_sel, state_win, win_new, gates, gain, layer, page_table, win_acc):
    depth, n_phys, page = cache_sel.shape[:3]
    n_batch, n_pages = page_table.shape
    win_buf = state_win.shape[2]
    past_len = n_pages * page
    q3 = q_b.reshape(n_batch, N_HEADS, HEAD_DIM)
    o_c, sel = _select_sample(q3, k_cmp, v_cmp)
    pages = cache_sel.reshape(depth * n_phys * page * UNITS, HEAD_DIM)
    wins = state_win.reshape(depth * n_batch * win_buf * UNITS, HEAD_DIM)
    base = layer * n_phys
    per_b = lambda shape: pl.BlockSpec((1,) + shape, lambda b, pt: (b, 0, 0))
    page_specs = [pl.BlockSpec((page * UNITS, HEAD_DIM), lambda b, pt, p=p: (base + pt[b, p], 0))
                  for p in range(n_pages)]
    in_specs = [per_b((N_HEADS, HEAD_DIM)), per_b((N_HEADS, HEAD_DIM)), per_b((N_HEADS, LANES)),
                per_b((1, ROW_WIDTH)),
                pl.BlockSpec((win_buf * UNITS, HEAD_DIM), lambda b, pt: (layer * n_batch + b, 0)),
                per_b((1, ROW_WIDTH)), per_b((1, GATE_WIDTH)),
                pl.BlockSpec((N_HEADS, HEAD_DIM), lambda b, pt: (0, 0))] + page_specs
    args = [page_table, q3, o_c, sel, sel_new.reshape(n_batch, 1, ROW_WIDTH), wins,
            win_new.reshape(n_batch, 1, ROW_WIDTH), gates.reshape(n_batch, 1, GATE_WIDTH),
            gain.reshape(N_HEADS, HEAD_DIM)] + [pages] * n_pages + [win_acc]
    in_specs.append(pl.BlockSpec(memory_space=pl.ANY))
    out, win_acc = pl.pallas_call(
        functools.partial(_attn_sample_kernel, n_pages=n_pages, page=page, win_buf=win_buf, past_len=past_len),
        grid_spec=pltpu.PrefetchScalarGridSpec(
            num_scalar_prefetch=1,
            grid=(n_batch,),
            in_specs=in_specs,
            out_specs=[per_b((N_HEADS, HEAD_DIM)),
                       pl.BlockSpec((win_buf * UNITS, HEAD_DIM), lambda b, pt: (layer * n_batch + b, 0))],
        ),
        out_shape=[jax.ShapeDtypeStruct((n_batch, N_HEADS, HEAD_DIM), BF16),
                   jax.ShapeDtypeStruct(win_acc.shape, F32)],
        input_output_aliases={len(args) - 1: 1},
        compiler_params=_params(("arbitrary",)),
        name="attn_sample",
    )(*args)
    return out.reshape(n_batch, D_ATTN), win_acc


def _lru_gates(xc, wa_ref, wx_ref, ba, bx, lam):
    width = xc.shape[1]
    bd = width // LRU_BLOCKS
    xb = xc.astype(BF16)
    ra = jnp.concatenate([jnp.dot(xb[:, n * bd:(n + 1) * bd], wa_ref[n], preferred_element_type=F32)
                          for n in range(LRU_BLOCKS)], axis=1)
    rx = jnp.concatenate([jnp.dot(xb[:, n * bd:(n + 1) * bd], wx_ref[n], preferred_element_type=F32)
                          for n in range(LRU_BLOCKS)], axis=1)
    r = jax.nn.sigmoid(ra + ba)
    i = jax.nn.sigmoid(rx + bx)
    log_a = -LRU_C * r * _softplus(-lam)
    a = jnp.exp(log_a)
    b = jnp.sqrt(-jnp.tanh(log_a) * (a * a + 1.0)) * (i * xc)
    return a, b


def _scan_rows(a8, b8, h):
    row = lax.broadcasted_iota(jnp.int32, a8.shape, 0)
    s = 1
    while s < SUBLANES:
        a_prev = jnp.where(row >= s, pltpu.roll(a8, s, 0), 1.0)
        b_prev = jnp.where(row >= s, pltpu.roll(b8, s, 0), 0.0)
        b8 = a8 * b_prev + b8
        a8 = a8 * a_prev
        s *= 2
    return a8 * h + b8


def _lru_prompt_kernel(ux_ref, uy_ref, cw_ref, cb_ref, wa_ref, wx_ref, ba_ref, bx_ref, lam_ref, gain_ref,
                       r_ref, h_ref, x_s, h_s, a_s, b_s):
    tt = ux_ref.shape[0]
    i = pl.program_id(1)

    @pl.when(i == 0)
    def _():
        x_s[0:SUBLANES, :] = jnp.zeros((SUBLANES, x_s.shape[1]), F32)
        h_s[...] = jnp.zeros_like(h_s)

    @pl.when(i > 0)
    def _():
        x_s[0:SUBLANES, :] = x_s[tt:tt + SUBLANES, :]

    x_s[SUBLANES:, :] = ux_ref[...]
    xc = cb_ref[...]
    for j in range(CONV_W):
        k = CONV_W - 1 - j
        xc = xc + x_s[SUBLANES - k:SUBLANES - k + tt, :] * cw_ref[j:j + 1, :]
    a, b = _lru_gates(xc, wa_ref, wx_ref, ba_ref[...], bx_ref[...], lam_ref[...])
    a_s[...] = a
    b_s[...] = b

    def group(k, h):
        r0 = pl.multiple_of(k * SUBLANES, SUBLANES)
        hs = _scan_rows(a_s[pl.ds(r0, SUBLANES), :], b_s[pl.ds(r0, SUBLANES), :], h)
        a_s[pl.ds(r0, SUBLANES), :] = hs
        return hs[SUBLANES - 1:SUBLANES, :]

    h_last = lax.fori_loop(0, tt // SUBLANES, group, h_s[...])
    h_s[...] = h_last
    h_ref[0] = h_last
    y = a_s[...] * jax.nn.gelu(uy_ref[...])
    r_ref[...] = _rms(y, gain_ref[...]).astype(r_ref.dtype)


def _lru_prompt(proj, n_batch, seq_len, weights):
    width = D_ATTN
    tt = _row_tile(seq_len, 512, SUBLANES)
    n_t = seq_len // tt
    rows = lambda col: pl.BlockSpec((tt, width), lambda b, i: (b * n_t + i, col))
    vec = _full((1, width))
    blk = _full((LRU_BLOCKS, width // LRU_BLOCKS, width // LRU_BLOCKS))
    return pl.pallas_call(
        _lru_prompt_kernel,
        grid=(n_batch, n_t),
        in_specs=[rows(COL_UX // width), rows(COL_UY // width), _full((CONV_W, width)), vec, blk, blk,
                  vec, vec, vec, vec],
        out_specs=[pl.BlockSpec((tt, width), lambda b, i: (b * n_t + i, 0)),
                   pl.BlockSpec((1, 1, width), lambda b, i: (b, 0, 0))],
        out_shape=[jax.ShapeDtypeStruct((n_batch * seq_len, width), BF16),
                   jax.ShapeDtypeStruct((n_batch, 1, width), F32)],
        scratch_shapes=[pltpu.VMEM((tt + SUBLANES, width), F32), pltpu.VMEM((1, width), F32),
                        pltpu.VMEM((tt, width), F32), pltpu.VMEM((tt, width), F32)],
        compiler_params=_params(("parallel", "arbitrary")),
        name="lru_prompt",
    )(proj, proj, *weights)


def _lru_sample_kernel(ux_ref, uy_ref, c0_ref, c1_ref, c2_ref, h0_ref, cw_ref, cb_ref, wa_ref, wx_ref,
                       ba_ref, bx_ref, lam_ref, gain_ref, r_ref, h_ref):
    xc = (cb_ref[...] + c0_ref[...] * cw_ref[0:1, :] + c1_ref[...] * cw_ref[1:2, :]
          + c2_ref[...] * cw_ref[2:3, :] + ux_ref[...] * cw_ref[3:4, :])
    a, b = _lru_gates(xc, wa_ref, wx_ref, ba_ref[...], bx_ref[...], lam_ref[...])
    h = a * h0_ref[...] + b
    h_ref[...] = h
    r_ref[...] = _rms(h * jax.nn.gelu(uy_ref[...]), gain_ref[...]).astype(r_ref.dtype)


def _lru_sample(ux, uy, conv_state, h0, weights):
    n, width = ux.shape
    mat = _full((n, width))
    vec = _full((1, width))
    blk = _full((LRU_BLOCKS, width // LRU_BLOCKS, width // LRU_BLOCKS))
    return pl.pallas_call(
        _lru_sample_kernel,
        grid=(1,),
        in_specs=[mat] * 6 + [_full((CONV_W, width)), vec, blk, blk, vec, vec, vec, vec],
        out_specs=[mat, mat],
        out_shape=[jax.ShapeDtypeStruct((n, width), BF16), jax.ShapeDtypeStruct((n, width), F32)],
        compiler_params=_params(("arbitrary",)),
        name="lru_sample",
    )(ux, uy, conv_state[:, 0], conv_state[:, 1], conv_state[:, 2], h0, *weights)


def _swiglu_step(x, wg_ref, wu_ref, wd_ref):
    g = jnp.dot(x, wg_ref[...].astype(BF16), preferred_element_type=F32)
    u = jnp.dot(x, wu_ref[...].astype(BF16), preferred_element_type=F32)
    hid = (jax.nn.silu(g) * u).astype(BF16)
    return jnp.dot(hid, wd_ref[...].astype(BF16), preferred_element_type=F32)


def _ffn_kernel(x_ref, g_ref, wg_ref, wu_ref, wd_ref, o_ref, hn_s):
    @pl.when(pl.program_id(1) == 0)
    def _():
        x = x_ref[...]
        o_ref[...] = x
        hn_s[...] = _rms(x, g_ref[...]).astype(hn_s.dtype)

    o_ref[...] += _swiglu_step(hn_s[...], wg_ref, wu_ref, wd_ref)


def _ffn(x, gain, wg, wu, wd):
    n, d = x.shape
    d_ff = wg.shape[1]
    tm = _row_tile(n, 832)
    tf = _row_tile(d_ff, 256, LANES)
    row = pl.BlockSpec((tm, d), lambda i, f: (i, 0))
    return pl.pallas_call(
        _ffn_kernel,
        grid=(n // tm, d_ff // tf),
        in_specs=[row, pl.BlockSpec((1, d), lambda i, f: (0, 0)),
                  pl.BlockSpec((d, tf), lambda i, f: (0, f)), pl.BlockSpec((d, tf), lambda i, f: (0, f)),
                  pl.BlockSpec((tf, d), lambda i, f: (f, 0))],
        out_specs=row,
        out_shape=jax.ShapeDtypeStruct((n, d), F32),
        scratch_shapes=[pltpu.VMEM((tm, d), BF16)],
        compiler_params=_params(("parallel", "arbitrary")),
        name="ffn",
    )(x, gain.reshape(1, d), wg, wu, wd)


MOE_BLOCK = 1280
MOE_TILE = 256
COMBINE_TILE = 320


def _router_kernel(x_ref, g_ref, w_ref, o_ref, *, n_experts):
    hn = _rms(x_ref[...], g_ref[...]).astype(BF16)
    logits = jnp.dot(hn, w_ref[...], preferred_element_type=F32)
    lane = lax.broadcasted_iota(jnp.int32, logits.shape, 1).astype(F32)
    logits = jnp.where(lane < n_experts, logits, -jnp.inf)
    m1 = jnp.max(logits, axis=-1, keepdims=True)
    i1 = jnp.min(jnp.where(logits == m1, lane, float(LANES)), axis=-1, keepdims=True)
    rest = jnp.where(lane == i1, -jnp.inf, logits)
    m2 = jnp.max(rest, axis=-1, keepdims=True)
    i2 = jnp.min(jnp.where(rest == m2, lane, float(LANES)), axis=-1, keepdims=True)
    e2 = jnp.exp(m2 - m1)
    den = 1.0 + e2
    o_ref[...] = (jnp.where(lane == 0.0, 1.0 / den, 0.0) + jnp.where(lane == 1.0, e2 / den, 0.0)
                  + jnp.where(lane == 2.0, i1, 0.0) + jnp.where(lane == 3.0, i2, 0.0))


def _router(x, gain, w_router):
    n, d = x.shape
    n_experts = w_router.shape[1]
    tm = _row_tile(n, 832)
    w = jnp.concatenate([w_router, jnp.zeros((d, LANES - n_experts), w_router.dtype)], axis=1).astype(BF16)
    return pl.pallas_call(
        functools.partial(_router_kernel, n_experts=n_experts),
        grid=(n // tm,),
        in_specs=[pl.BlockSpec((tm, d), lambda i: (i, 0)), _full((1, d)), _full((d, LANES))],
        out_specs=pl.BlockSpec((tm, LANES), lambda i: (i, 0)),
        out_shape=jax.ShapeDtypeStruct((n, LANES), F32),
        compiler_params=_params(("parallel",)),
        name="router",
    )(x, gain.reshape(1, d), w)


def _start_row_gather(ids_ref, first, n_rows, src_hbm, dst, sem):
    def body(r, carry):
        pltpu.make_async_copy(src_hbm.at[pl.ds(ids_ref[first + r], 1)], dst.at[pl.ds(r, 1)], sem).start()
        return carry

    lax.fori_loop(0, n_rows, body, 0, unroll=8)


def _dispatch_kernel(ids_ref, on_ref, x_hbm, g_ref, o_ref, buf, sem):
    i = pl.program_id(0)
    slot = i % 2
    start = lambda tile, s: _start_row_gather(ids_ref, tile * MOE_TILE, MOE_TILE, x_hbm, buf.at[s], sem.at[s])

    @pl.when((i == 0) & (on_ref[0] > 0))
    def _():
        start(0, 0)

    @pl.when(i + 1 < pl.num_programs(0))
    def _():
        @pl.when(on_ref[i + 1] > 0)
        def _():
            start(i + 1, 1 - slot)

    @pl.when(on_ref[i] > 0)
    def _():
        pltpu.make_async_copy(buf.at[slot], buf.at[slot], sem.at[slot]).wait()
        o_ref[...] = _rms(buf[slot], g_ref[...]).astype(o_ref.dtype)

    @pl.when(on_ref[i] == 0)
    def _():
        o_ref[...] = jnp.zeros_like(o_ref)


def _dispatch(x, gain, row_ids, tile_on):
    n, d = x.shape
    n_slots = row_ids.shape[0]
    return pl.pallas_call(
        _dispatch_kernel,
        grid_spec=pltpu.PrefetchScalarGridSpec(
            num_scalar_prefetch=2,
            grid=(n_slots // MOE_TILE,),
            in_specs=[pl.BlockSpec(memory_space=pl.ANY), pl.BlockSpec((1, d), lambda i, ids, on: (0, 0))],
            out_specs=pl.BlockSpec((MOE_TILE, d), lambda i, ids, on: (i, 0)),
            scratch_shapes=[pltpu.VMEM((2, MOE_TILE, d), F32), pltpu.SemaphoreType.DMA((2,))],
        ),
        out_shape=jax.ShapeDtypeStruct((n_slots, d), BF16),
        compiler_params=_params(("arbitrary",)),
        name="moe_dispatch",
    )(row_ids, tile_on, x, gain.reshape(1, d))


def _experts_kernel(be_ref, br_ref, x_ref, w_ref, wg_ref, wu_ref, wd_ref, o_ref):
    b, f = pl.program_id(0), pl.program_id(1)
    rows = br_ref[b]

    @pl.when(f == 0)
    def _():
        o_ref[...] = jnp.zeros_like(o_ref)

    for used in range(MOE_TILE, MOE_BLOCK + 1, MOE_TILE):
        @pl.when(rows == used)
        def _():
            o_ref[0:used, :] += _swiglu_step(x_ref[0:used, :], wg_ref.at[0], wu_ref.at[0], wd_ref.at[0])

    @pl.when((f == pl.num_programs(1) - 1) & (rows > 0))
    def _():
        o_ref[...] = o_ref[...] * w_ref[...]


def _experts(xs, slot_w, blk_expert, blk_rows, wg, wu, wd):
    n_slots, d = xs.shape
    d_ff = wg.shape[2]
    tf = _row_tile(d_ff, 256, LANES)
    n_f = d_ff // tf
    fcol = lambda b, f, br: jnp.where(br[b] > 0, f, n_f - 1)
    return pl.pallas_call(
        _experts_kernel,
        grid_spec=pltpu.PrefetchScalarGridSpec(
            num_scalar_prefetch=2,
            grid=(n_slots // MOE_BLOCK, n_f),
            in_specs=[pl.BlockSpec((MOE_BLOCK, d), lambda b, f, be, br: (b, 0)),
                      pl.BlockSpec((MOE_BLOCK, 1), lambda b, f, be, br: (b, 0)),
                      pl.BlockSpec((1, d, tf), lambda b, f, be, br: (be[b], 0, fcol(b, f, br))),
                      pl.BlockSpec((1, d, tf), lambda b, f, be, br: (be[b], 0, fcol(b, f, br))),
                      pl.BlockSpec((1, tf, d), lambda b, f, be, br: (be[b], fcol(b, f, br), 0))],
            out_specs=pl.BlockSpec((MOE_BLOCK, d), lambda b, f, be, br: (b, 0)),
        ),
        out_shape=jax.ShapeDtypeStruct((n_slots, d), F32),
        compiler_params=_params(("arbitrary", "arbitrary")),
        name="moe_experts",
    )(blk_expert, blk_rows, xs, slot_w, wg, wu, wd)


def _combine_kernel(s0_ref, s1_ref, y_hbm, x_ref, head_ref, tail_ref, buf, sem, *, head_tiles):
    tc = x_ref.shape[0]
    i = pl.program_id(0)
    slot = i % 2

    def start(tile, s):
        _start_row_gather(s0_ref, tile * tc, tc, y_hbm, buf.at[s, 0], sem.at[s])
        _start_row_gather(s1_ref, tile * tc, tc, y_hbm, buf.at[s, 1], sem.at[s])

    @pl.when(i == 0)
    def _():
        start(0, 0)

    @pl.when(i + 1 < pl.num_programs(0))
    def _():
        start(i + 1, 1 - slot)

    pltpu.make_async_copy(buf.at[slot], buf.at[slot], sem.at[slot]).wait()

    @pl.when(i < head_tiles)
    def _():
        head_ref[...] = x_ref[...] + buf[slot, 0] + buf[slot, 1]

    @pl.when(i >= head_tiles)
    def _():
        tail_ref[...] = x_ref[...] + buf[slot, 0] + buf[slot, 1]


def _combine(x, y, slot0, slot1, n_head):
    n, d = x.shape
    tc = _row_tile(math.gcd(n, n_head), COMBINE_TILE, SUBLANES)
    head_tiles = n_head // tc
    row = pl.BlockSpec((tc, d), lambda i, s0, s1: (i, 0))
    return pl.pallas_call(
        functools.partial(_combine_kernel, head_tiles=head_tiles),
        grid_spec=pltpu.PrefetchScalarGridSpec(
            num_scalar_prefetch=2,
            grid=(n // tc,),
            in_specs=[pl.BlockSpec(memory_space=pl.ANY), row],
            out_specs=[pl.BlockSpec((tc, d), lambda i, s0, s1: (jnp.minimum(i, head_tiles - 1), 0)),
                       pl.BlockSpec((tc, d), lambda i, s0, s1: (jnp.maximum(i - head_tiles, 0), 0))],
            scratch_shapes=[pltpu.VMEM((2, 2, tc, d), F32), pltpu.SemaphoreType.DMA((2,))],
        ),
        out_shape=[jax.ShapeDtypeStruct((n_head, d), F32), jax.ShapeDtypeStruct((n - n_head, d), F32)],
        compiler_params=_params(("arbitrary",)),
        name="moe_combine",
    )(slot0, slot1, y, x)


def _routing_tables(route, n_experts):
    n = route.shape[0]
    expert = route[:, 2:4].astype(jnp.int32).T.reshape(-1)
    weight = route[:, 0:2].T.reshape(-1)
    token = jnp.tile(jnp.arange(n, dtype=jnp.int32), 2)
    onehot = (expert[:, None] == jnp.arange(n_experts, dtype=jnp.int32)).astype(jnp.int32)
    rank = jnp.sum((jnp.cumsum(onehot, axis=0) - 1) * onehot, axis=1)
    count = jnp.sum(onehot, axis=0)
    padded = (count + MOE_BLOCK - 1) // MOE_BLOCK * MOE_BLOCK
    ends = jnp.cumsum(padded)
    starts = ends - padded
    slot = starts[expert] + rank
    n_slots = (2 * n + n_experts * (MOE_BLOCK - 1) + MOE_BLOCK - 1) // MOE_BLOCK * MOE_BLOCK
    per_slot = jnp.zeros((n_slots, 2), F32).at[slot].set(jnp.stack([token.astype(F32), weight], axis=1))
    row_ids = per_slot[:, 0].astype(jnp.int32)
    slot_w = per_slot[:, 1:2]
    blk_first = jnp.arange(n_slots // MOE_BLOCK, dtype=jnp.int32) * MOE_BLOCK
    blk_expert = jnp.minimum(jnp.sum((ends[None, :] <= blk_first[:, None]).astype(jnp.int32), axis=1), n_experts - 1)
    used = (count + MOE_TILE - 1) // MOE_TILE * MOE_TILE
    blk_rows = jnp.clip(starts[blk_expert] + used[blk_expert] - blk_first, 0, MOE_BLOCK)
    blk_rows = jnp.where(blk_first < ends[-1], blk_rows, 0).astype(jnp.int32)
    tile_first = jnp.arange(n_slots // MOE_TILE, dtype=jnp.int32) * MOE_TILE
    tile_on = (tile_first % MOE_BLOCK < jnp.repeat(blk_rows, MOE_BLOCK // MOE_TILE)).astype(jnp.int32)
    return row_ids, slot_w, blk_expert, blk_rows, tile_on, slot[:n], slot[n:]


def _moe(x, gain, w_router, wg, wu, wd, n_head):
    route = _router(x, gain, w_router)
    row_ids, slot_w, blk_expert, blk_rows, tile_on, slot0, slot1 = _routing_tables(route, w_router.shape[1])
    xs = _dispatch(x, gain, row_ids, tile_on)
    y = _experts(xs, slot_w, blk_expert, blk_rows, wg, wu, wd)
    return _combine(x, y, slot0, slot1, n_head)


def kernel(x_prompt, x_sample, cache_cmp, cache_sel, state_win, state_lru, state_conv, page_table, ln_mix, w_in, q_norm, k_norm, cmp_pe, cmp_w1, cmp_w2, conv_w, conv_b, lru_wa, lru_ba, lru_wx, lru_bx, lru_lambda, out_norm_attn, out_norm_lru, w_out, ln_ffn, ffn_w_gate, ffn_w_up, ffn_w_down, moe_router, moe_w_gate, moe_w_up, moe_w_down):
    n_batch, seq_len, d_model = x_prompt.shape
    dec_batch, dec_seq, _ = x_sample.shape
    depth = ln_mix.shape[0]
    page = cache_cmp.shape[2]
    n_pages = page_table.shape[1]
    past_len = n_pages * page
    win_buf = state_win.shape[2]
    lru_width = d_model - D_ATTN
    assert dec_seq == 1 and lru_width == D_ATTN
    assert seq_len % Q_BLOCK == 0 and seq_len >= WIN_SPAN and seq_len % SEL_CHUNK == 0 and seq_len >= win_buf
    assert past_len % Q_BLOCK == 0 and win_buf <= past_len and page % CMP_BLOCK == 0
    n_p = n_batch * seq_len
    nc = seq_len // CMP_BLOCK
    nc_past = past_len // CMP_BLOCK
    kv_row = (2, N_KV, HEAD_DIM)

    pos = jnp.concatenate([jnp.tile(jnp.arange(seq_len), n_batch), jnp.full((dec_batch,), past_len)])
    tables = _rope_tables(pos)
    end_p = _rope_tables((jnp.arange(nc) + 1) * CMP_BLOCK - 1)
    end_s = _rope_tables((jnp.arange(nc_past) + 1) * CMP_BLOCK - 1)

    x = jnp.concatenate([x_prompt.reshape(n_p, d_model), x_sample.reshape(dec_batch, d_model)], axis=0)
    outs = [[] for _ in range(9)]
    win_acc = jnp.zeros((depth * dec_batch * win_buf * UNITS, HEAD_DIM), F32)
    for l in range(depth):
        proj = _matmul([x], [_reorder_w_in(w_in[l])], gain=ln_mix[l], tm_target=832, tn_target=1280)
        q_b, sel_f, sel_b, win_f, win_b, gates = _postproj(proj, tables, q_norm[l], k_norm[l, 1], k_norm[l, 2])
        c_rows = proj[:, COL_CMP:COL_CMP + ROW_WIDTH]

        cmp_w = (cmp_pe[l], cmp_w1[l].reshape(2, CMP_BLOCK // 2, 2 * HEAD_DIM, HEAD_DIM).astype(BF16),
                 cmp_w2[l].astype(BF16), k_norm[l, 0].reshape(1, HEAD_DIM))
        lru_w = (conv_w[l], conv_b[l].reshape(1, -1), lru_wa[l].astype(BF16), lru_wx[l].astype(BF16),
                 lru_ba[l].reshape(1, -1), lru_bx[l].reshape(1, -1), lru_lambda[l].reshape(1, -1),
                 out_norm_lru[l].reshape(1, -1))

        kc_p, vc_p = _compress_prompt(c_rows[:n_p].reshape(n_batch * nc, CMP_ROW), n_batch, nc, cmp_w, end_p)
        o_p = _attn_prompt(q_b, kc_p, vc_p, sel_b, win_b, gates, out_norm_attn[l], n_batch, seq_len)
        r_p, h_p = _lru_prompt(proj, n_batch, seq_len, lru_w)

        kc_s, vc_s = _compress_sample(cache_cmp, l, page_table, cmp_w, end_s, nb=2 if dec_batch % 2 == 0 else 1)
        o_s, win_acc = _attn_sample(q_b[n_p:], kc_s, vc_s, sel_f[n_p:], cache_sel, state_win, win_f[n_p:],
                                    gates[n_p:], out_norm_attn[l], l, page_table, win_acc)
        ux_s = proj[n_p:, COL_UX:COL_UX + lru_width]
        r_s, h_s = _lru_sample(ux_s, proj[n_p:, COL_UY:COL_UY + lru_width], state_conv[l], state_lru[l], lru_w)

        w_o = w_out[l].astype(BF16)
        x = _matmul([jnp.concatenate([o_p, o_s], axis=0), jnp.concatenate([r_p, r_s], axis=0)],
                    [w_o[:D_ATTN], w_o[D_ATTN:]], res=x, tm_target=832, tn_target=1024)

        k = l // 2
        if l % 2 == 0:
            x = _ffn(x, ln_ffn[l], ffn_w_gate[k], ffn_w_up[k], ffn_w_down[k])
        else:
            halves = _moe(x, ln_ffn[l], moe_router[k], moe_w_gate[k], moe_w_up[k], moe_w_down[k], n_p)
            x = jnp.concatenate(halves, axis=0) if l + 1 < depth else None

        conv_tail = jnp.stack([proj[(b + 1) * seq_len - (CONV_W - 1):(b + 1) * seq_len, COL_UX:COL_UX + lru_width]
                               for b in range(n_batch)])
        new = (c_rows[:n_p].reshape((n_batch, seq_len) + kv_row), c_rows[n_p:].reshape((dec_batch, 1) + kv_row),
               sel_f[:n_p].reshape((n_batch, seq_len) + kv_row), sel_f[n_p:].reshape((dec_batch, 1) + kv_row),
               win_f[:n_p].reshape((n_batch, seq_len) + kv_row)[:, seq_len - win_buf:],
               h_p.reshape(n_batch, lru_width), h_s,
               conv_tail,
               jnp.concatenate([state_conv[l][:, 1:], ux_s[:, None, :]], axis=1))
        for acc, val in zip(outs, new):
            acc.append(val)

    stacked = [jnp.stack(v) for v in outs]
    stacked.insert(5, win_acc.reshape((depth, dec_batch, win_buf) + kv_row))
    y_p, y_s = halves if x is None else (x[:n_p], x[n_p:])
    return (y_p.reshape(n_batch, seq_len, d_model), y_s.reshape(dec_batch, 1, d_model), *stacked)
```
